```python
import jax, jax.numpy as jnp
from jax import lax
import numpy as np

D_MODEL = 2048
BATCH = 8
SEQ = 2048
DEPTH = 1

CTX_LEN = 256
GRID_W = 64
D_MIX = D_MODEL
DN_HEADS = 8
DN_HEAD_DIM = D_MIX // 2 // DN_HEADS
DN_WIDTH = DN_HEADS * DN_HEAD_DIM
DN_CHUNK = 64
CONV_K = 5
MLP_WIDTH = D_MIX - DN_WIDTH
MLP_GROUPS = 8
MLP_GROUP_DIM = MLP_WIDTH // MLP_GROUPS
MLP_CHUNK = 128
CHUNK_ROWS = MLP_CHUNK // GRID_W
D_FF = ((8 * D_MODEL // 3 + 127) // 128) * 128
N_MOD = 9
IN_COLS = 4 * DN_WIDTH + 4 * DN_HEADS + 2 * MLP_WIDTH
EPS = 1e-6

kernel_name = "hybrid_deltanet_chunkmlp_macaron_dit"


def _rmsnorm(x, g):
    xf = x.astype(jnp.float32)
    y = xf * lax.rsqrt(jnp.mean(xf * xf, axis=-1, keepdims=True) + EPS)
    return (y * g.astype(jnp.float32)).astype(x.dtype)


def _l2norm(x):
    xf = x.astype(jnp.float32)
    return xf * lax.rsqrt(jnp.sum(xf * xf, axis=-1, keepdims=True) + EPS)


def _modulate(h, shift, scale):
    return h * (1.0 + scale) + shift


def _swiglu(h, w_in, w_out):
    a, b = jnp.split(h @ w_in, 2, axis=-1)
    return (jax.nn.silu(a) * b) @ w_out


def _split_proj(z):
    sizes = [DN_WIDTH] * 4 + [2 * DN_HEADS] * 2 + [MLP_WIDTH] * 2
    idx = [int(i) for i in np.cumsum(sizes)[:-1]]
    q, k, v, gate, a, b, u, vm = jnp.split(z, idx, axis=-1)
    return jnp.concatenate([q, k, v], axis=-1), gate, a, b, u, vm


def _short_conv(x, w):
    C = x.shape[-1]
    y = lax.conv_general_dilated(
        x, w[:, None, :].astype(x.dtype), window_strides=(1,),
        padding=[(CONV_K // 2, CONV_K // 2)],
        dimension_numbers=('NWC', 'WIO', 'NWC'), feature_group_count=C)
    return jax.nn.silu(y)


def _dn_prepare(z_qkv, z_a, z_b, conv_w, a_log, dt_bias):
    B, T, _ = z_qkv.shape
    qkv = _short_conv(z_qkv, conv_w)
    q, k, v = (t.reshape(B, T, DN_HEADS, DN_HEAD_DIM) for t in jnp.split(qkv, 3, axis=-1))
    q = _l2norm(q) * (DN_HEAD_DIM ** -0.5)
    k = _l2norm(k)
    a = z_a.astype(jnp.float32).reshape(B, T, 2, DN_HEADS)
    g = -jnp.exp(a_log.astype(jnp.float32)) * jax.nn.softplus(a + dt_bias.astype(jnp.float32))
    beta = jax.nn.sigmoid(z_b.astype(jnp.float32).reshape(B, T, 2, DN_HEADS))
    return q, k, v, g, beta


def _chunk_gated_delta(q, k, v, g, beta, state0):
    f32 = jnp.float32
    q, k, v, g, beta = (t.astype(f32) for t in (q, k, v, g, beta))
    B, T, H, _ = q.shape
    Dv = v.shape[-1]
    C = DN_CHUNK
    N = T // C

    def chunks(t):
        return jnp.swapaxes(t.reshape((B, N, C, H) + t.shape[3:]), 2, 3)

    qc, kc, vc, gc, bc = map(chunks, (q, k, v, g, beta))
    gc = jnp.cumsum(gc, axis=-1)
    tri = jnp.tril(jnp.ones((C, C), dtype=bool))
    strict = jnp.tril(jnp.ones((C, C), dtype=bool), -1)
    decay = jnp.exp(jnp.where(tri, gc[..., :, None] - gc[..., None, :], -jnp.inf))
    kb = kc * bc[..., None]
    L = jnp.where(strict, jnp.einsum('bnhid,bnhjd->bnhij', kb, kc) * decay, 0.0)
    eye = jnp.eye(C, dtype=f32)
    Tm = lax.linalg.triangular_solve(eye + L, jnp.broadcast_to(eye, L.shape),
                                     left_side=True, lower=True)
    u = jnp.einsum('bnhij,bnhje->bnhie', Tm, vc * bc[..., None])
    w = jnp.einsum('bnhij,bnhjd->bnhid', Tm, kb * jnp.exp(gc)[..., None])
    qk = jnp.where(tri, jnp.einsum('bnhid,bnhjd->bnhij', qc, kc) * decay, 0.0)
    q_dec = qc * jnp.exp(gc)[..., None]
    k_dec = kc * jnp.exp(gc[..., -1:] - gc)[..., None]
    g_last = jnp.exp(gc[..., -1])

    def step(S, xs):
        qk_i, qd_i, w_i, u_i, kd_i, gl_i = xs
        v_new = u_i - jnp.einsum('bhcd,bhde->bhce', w_i, S)
        o_i = jnp.einsum('bhcd,bhde->bhce', qd_i, S) + jnp.einsum('bhij,bhje->bhie', qk_i, v_new)
        S = S * gl_i[..., None, None] + jnp.einsum('bhcd,bhce->bhde', kd_i, v_new)
        return S, o_i

    xs = tuple(jnp.moveaxis(t, 1, 0) for t in (qk, q_dec, w, u, k_dec, g_last))
    S, o = lax.scan(step, state0.astype(f32), xs)
    o = jnp.transpose(o, (1, 0, 3, 2, 4)).reshape(B, T, H, Dv)
    return o, S


def _bidir_delta(dn_ctx, dn_lat):
    qc, kc, vc, gc, bc = dn_ctx
    ql, kl, vl, gl, bl = dn_lat
    B = ql.shape[0]
    o_ctx, o_lat = 0.0, 0.0
    for d in range(2):
        f = (lambda t: jnp.flip(t, axis=1)) if d == 1 else (lambda t: t)
        s0 = jnp.zeros((B, DN_HEADS, DN_HEAD_DIM, DN_HEAD_DIM), jnp.float32)
        oc, s_ctx = _chunk_gated_delta(f(qc), f(kc), f(vc), f(gc[:, :, d]), f(bc[:, :, d]), s0)
        ol, _ = _chunk_gated_delta(f(ql), f(kl), f(vl), f(gl[:, :, d]), f(bl[:, :, d]), s_ctx)
        o_ctx = o_ctx + f(oc)
        o_lat = o_lat + f(ol)
    return o_ctx, o_lat


def _gated_head_norm(o, z_gate, g):
    B, T = o.shape[:2]
    zg = z_gate.astype(jnp.float32).reshape(B, T, DN_HEADS, DN_HEAD_DIM)
    y = _rmsnorm(o, g) * jax.nn.silu(zg)
    return y.reshape(B, T, DN_WIDTH).astype(z_gate.dtype)


def _chunk_mlp(z_u, z_v, n_chunks, w_s, b_s, v_g):
    B, T, _ = z_u.shape
    u = jax.nn.gelu(z_u)
    v = jax.nn.gelu(z_v).reshape(B, n_chunks, MLP_CHUNK, MLP_GROUPS, MLP_GROUP_DIM)
    v = _rmsnorm(v, v_g.reshape(MLP_GROUPS, MLP_GROUP_DIM))
    s = jnp.einsum('gpq,bnqgc->bnpgc', w_s, v) + b_s.T[None, None, :, :, None]
    return u * s.reshape(B, T, MLP_WIDTH)


def setup_inputs(seed: int = 0) -> dict:
    key = jax.random.key(seed)
    ks = jax.random.split(key, 24)
    nrm = jax.random.normal
    f32 = jnp.float32
    dt = jnp.exp(jax.random.uniform(ks[12], (DEPTH, 2, DN_HEADS), f32,
                                    float(np.log(1e-3)), float(np.log(1e-1))))
    return {
        'x': nrm(ks[0], (BATCH, SEQ, D_MODEL), f32),
        'c': nrm(ks[1], (BATCH, D_MODEL), f32),
        'ctx': nrm(ks[2], (BATCH, CTX_LEN, D_MODEL), f32),
        'c_ctx': nrm(ks[3], (D_MODEL,), f32),
        'w_mod': nrm(ks[4], (DEPTH, D_MODEL, N_MOD * D_MODEL), f32) * (0.5 * D_MODEL ** -0.5),
        'b_mod': nrm(ks[5], (DEPTH, N_MOD * D_MODEL), f32) * 0.01,
        'norm_g': 1.0 + 0.1 * nrm(ks[6], (DEPTH, 3, D_MODEL), f32),
        'ffn1_w_in': nrm(ks[7], (DEPTH, D_MODEL, 2 * D_FF), f32) * D_MODEL ** -0.5,
        'ffn1_w_out': nrm(ks[8], (DEPTH, D_FF, D_MODEL), f32) * D_FF ** -0.5,
        'w_in': nrm(ks[9], (DEPTH, D_MODEL, IN_COLS), f32) * D_MODEL ** -0.5,
        'conv_w': nrm(ks[10], (DEPTH, CONV_K, 3 * DN_WIDTH), f32) * CONV_K ** -0.5,
        'a_log': jnp.log(jax.random.uniform(ks[11], (DEPTH, 2, DN_HEADS), f32, 1.0, 16.0)),
        'dt_bias': dt + jnp.log(-jnp.expm1(-dt)),
        'head_norm_g': 1.0 + 0.1 * nrm(ks[13], (DEPTH, DN_HEAD_DIM), f32),
        'spatial_w': nrm(ks[14], (DEPTH, MLP_GROUPS, MLP_CHUNK, MLP_CHUNK), f32) * MLP_CHUNK ** -0.5,
        'spatial_b': 1.0 + 0.1 * nrm(ks[15], (DEPTH, MLP_GROUPS, MLP_CHUNK), f32),
        'mlp_norm_g': 1.0 + 0.1 * nrm(ks[16], (DEPTH, MLP_WIDTH), f32),
        'w_out': nrm(ks[17], (DEPTH, D_MIX, D_MODEL), f32) * D_MIX ** -0.5,
        'ffn2_w_in': nrm(ks[18], (DEPTH, D_MODEL, 2 * D_FF), f32) * D_MODEL ** -0.5,
        'ffn2_w_out': nrm(ks[19], (DEPTH, D_FF, D_MODEL), f32) * D_FF ** -0.5,
        'final_g': 1.0 + 0.1 * nrm(ks[20], (D_MODEL,), f32),
    }


def reference(x, c, ctx, c_ctx, w_mod, b_mod, norm_g, ffn1_w_in, ffn1_w_out, w_in, conv_w,
              a_log, dt_bias, head_norm_g, spatial_w, spatial_b, mlp_norm_g, w_out,
              ffn2_w_in, ffn2_w_out, final_g):
    rows = x.shape[1] // GRID_W
    n_chunks_lat = rows // CHUNK_ROWS
    n_chunks_ctx = ctx.shape[1] // MLP_CHUNK
    for l in range(DEPTH):
        last = l == DEPTH - 1
        m = jnp.split((jax.nn.silu(c) @ w_mod[l] + b_mod[l])[:, None, :], N_MOD, axis=-1)
        mc = jnp.split((jax.nn.silu(c_ctx) @ w_mod[l] + b_mod[l])[None, None, :], N_MOD, axis=-1)

        x = x + 0.5 * m[2] * _swiglu(_modulate(_rmsnorm(x, norm_g[l, 0]), m[0], m[1]),
                                     ffn1_w_in[l], ffn1_w_out[l])
        ctx = ctx + 0.5 * mc[2] * _swiglu(_modulate(_rmsnorm(ctx, norm_g[l, 0]), mc[0], mc[1]),
                                          ffn1_w_in[l], ffn1_w_out[l])

        h_x = _modulate(_rmsnorm(x, norm_g[l, 1]), m[3], m[4])
        h_c = _modulate(_rmsnorm(ctx, norm_g[l, 1]), mc[3], mc[4])
        qkv_x, gate_x, a_x, b_x, u_x, v_x = _split_proj(h_x @ w_in[l])
        qkv_c, gate_c, a_c, b_c, u_c, v_c = _split_proj(h_c @ w_in[l])

        dn_x = _dn_prepare(qkv_x, a_x, b_x, conv_w[l], a_log[l], dt_bias[l])
        dn_c = _dn_prepare(qkv_c, a_c, b_c, conv_w[l], a_log[l], dt_bias[l])
        o_c_dn, o_x_dn = _bidir_delta(dn_c, dn_x)
        o_x_a = _gated_head_norm(o_x_dn, gate_x, head_norm_g[l])
        o_x_b = _chunk_mlp(u_x, v_x, n_chunks_lat, spatial_w[l], spatial_b[l], mlp_norm_g[l])
        x = x + m[5] * (jnp.concatenate([o_x_a, o_x_b], axis=-1) @ w_out[l])

        if not last:
            o_c_a = _gated_head_norm(o_c_dn, gate_c, head_norm_g[l])
            o_c_b = _chunk_mlp(u_c, v_c, n_chunks_ctx, spatial_w[l], spatial_b[l], mlp_norm_g[l])
            ctx = ctx + mc[5] * (jnp.concatenate([o_c_a, o_c_b], axis=-1) @ w_out[l])
            ctx = ctx + 0.5 * mc[8] * _swiglu(_modulate(_rmsnorm(ctx, norm_g[l, 2]), mc[6], mc[7]),
                                              ffn2_w_in[l], ffn2_w_out[l])

        x = x + 0.5 * m[8] * _swiglu(_modulate(_rmsnorm(x, norm_g[l, 2]), m[6], m[7]),
                                     ffn2_w_in[l], ffn2_w_out[l])
    return _rmsnorm(x, final_g)
```

```python
import functools

import jax
import jax.numpy as jnp
import numpy as np
from jax import lax
from jax.experimental import pallas as pl
from jax.experimental.pallas import tpu as pltpu

F32 = jnp.float32
BF16 = jnp.bfloat16

EPS = 1e-6
LANE = 128
N_MOD = 9
DN_HEADS = 8
DN_HEAD_DIM = 128
DN_CHUNK = 64
CONV_K = 5
MLP_GROUPS = 8
MLP_CHUNK = 128
V7X_VMEM_LIMIT = 60000 * 1024


def _cparams(sem):
    return pltpu.CompilerParams(dimension_semantics=sem, vmem_limit_bytes=V7X_VMEM_LIMIT)


def _rms(x, g):
    return x * lax.rsqrt(jnp.mean(x * x, axis=-1, keepdims=True) + EPS) * g


def _silu(x):
    return x * jax.nn.sigmoid(x)


def _dot(a, b):
    return jnp.dot(a, b, preferred_element_type=F32)


def _dot_nt(a, b):
    return lax.dot_general(a, b, (((1,), (1,)), ((), ())), preferred_element_type=F32)


def _mod_kernel(c_ref, w_ref, b_ref, o_ref):
    s = _silu(c_ref[...]).astype(BF16)
    o_ref[...] = _dot(s, w_ref[...].astype(BF16)) + b_ref[...]


def _mod_call(c_all, w_mod, b_mod, tn=1024):
    rows, d = c_all.shape
    n = w_mod.shape[1]
    return pl.pallas_call(
        _mod_kernel,
        grid=(n // tn,),
        in_specs=[pl.BlockSpec((rows, d), lambda j: (0, 0)),
                  pl.BlockSpec((d, tn), lambda j: (0, j)),
                  pl.BlockSpec((1, tn), lambda j: (0, j))],
        out_specs=pl.BlockSpec((rows, tn), lambda j: (0, j)),
        out_shape=jax.ShapeDtypeStruct((rows, n), F32),
        compiler_params=_cparams(("arbitrary",)),
        name="mod",
    )(c_all, w_mod, b_mod)


def _ffn_kernel(*refs, mode, mod_base, tf, nf):
    if mode == "ffn2":
        x_ref, m_ref, g_ref, wi_ref, wo_ref, g2_ref, out_ref, h_s, acc_s = refs
    elif mode == "ffn1":
        x_ref, m_ref, g_ref, wi_ref, wo_ref, g2_ref, out_ref, h2_ref, h_s, acc_s = refs
    else:
        x_ref, m_ref, g_ref, wi_ref, wo_ref, g2_ref, h2_ref, h_s, acc_s = refs
    j = pl.program_id(1)

    def mvec(k):
        return m_ref[0, k:k + 1, :]

    @pl.when(j == 0)
    def _():
        h = _rms(x_ref[...], g_ref[...]) * (1.0 + mvec(mod_base + 1)) + mvec(mod_base)
        h_s[...] = h.astype(BF16)
        acc_s[...] = jnp.zeros_like(acc_s)

    ab = _dot(h_s[...], wi_ref[...])
    act = (_silu(ab[:, :tf]) * ab[:, tf:]).astype(BF16)
    acc_s[...] += _dot(act, wo_ref[...])

    @pl.when(j == nf - 1)
    def _():
        xn = x_ref[...] + 0.5 * mvec(mod_base + 2) * acc_s[...]
        if mode == "ffn2":
            out_ref[...] = _rms(xn, g2_ref[...])
        else:
            if mode == "ffn1":
                out_ref[...] = xn
            h2 = _rms(xn, g2_ref[...]) * (1.0 + mvec(mod_base + 4)) + mvec(mod_base + 3)
            h2_ref[...] = h2.astype(BF16)


def _prep_ffn_weights(w_in, w_out, tf):
    d, f2 = w_in.shape
    f = f2 // 2
    nf = -(-f // tf)
    pad = nf * tf - f
    a = jnp.pad(w_in[:, :f].astype(BF16), ((0, 0), (0, pad))).reshape(d, nf, 1, tf)
    b = jnp.pad(w_in[:, f:].astype(BF16), ((0, 0), (0, pad))).reshape(d, nf, 1, tf)
    wi = jnp.concatenate([a, b], axis=2).reshape(d, nf * 2 * tf)
    wo = jnp.pad(w_out.astype(BF16), ((0, pad), (0, 0)))
    return wi, wo


def _ffn_call(x2d, m3, g, wi, wo, g2, *, mode, mod_base, rows_per_mod, mod_offset, tm, tf):
    r, d = x2d.shape
    nf = wo.shape[0] // tf
    tiles_per_mod = rows_per_mod // tm
    kern = functools.partial(_ffn_kernel, mode=mode, mod_base=mod_base, tf=tf, nf=nf)
    x_spec = pl.BlockSpec((tm, d), lambda i, j: (i, 0))
    in_specs = [x_spec,
                pl.BlockSpec((1, N_MOD, d), lambda i, j: (mod_offset + i // tiles_per_mod, 0, 0)),
                pl.BlockSpec((1, d), lambda i, j: (0, 0)),
                pl.BlockSpec((d, 2 * tf), lambda i, j: (0, j)),
                pl.BlockSpec((tf, d), lambda i, j: (j, 0)),
                pl.BlockSpec((1, d), lambda i, j: (0, 0))]
    o_f32 = (jax.ShapeDtypeStruct((r, d), F32), x_spec)
    o_bf = (jax.ShapeDtypeStruct((r, d), BF16), x_spec)
    outs = {"ffn2": [o_f32], "ffn1": [o_f32, o_bf], "ffn1ctx": [o_bf]}[mode]
    return pl.pallas_call(
        kern,
        grid=(r // tm, nf),
        in_specs=in_specs,
        out_specs=[o[1] for o in outs],
        out_shape=[o[0] for o in outs],
        scratch_shapes=[pltpu.VMEM((tm, d), BF16), pltpu.VMEM((tm, d), F32)],
        compiler_params=_cparams(("parallel", "arbitrary")),
        name=mode,
    )(x2d, m3, g, wi, wo, g2)


def _inproj_kernel(*refs, tm, with_mlp):
    if with_mlp:
        (h_ref, w_ref, wab_ref, sw_ref, sb_ref, mng_ref,
         qkv_ref, ab_ref, gate_ref, ob_ref, u_s) = refs
    else:
        h_ref, w_ref, wab_ref, qkv_ref, ab_ref = refs
    j = pl.program_id(1)
    z = _dot(h_ref[...], w_ref[...])

    @pl.when(j == 0)
    def _():
        ab_ref[...] = _dot(h_ref[...], wab_ref[...])

    @pl.when(j < 3)
    def _():
        qkv_ref[...] = z.astype(BF16)

    if not with_mlp:
        return

    @pl.when(j == 3)
    def _():
        gate_ref[...] = z.astype(BF16)

    @pl.when(j == 4)
    def _():
        u_s[...] = jax.nn.gelu(z).astype(BF16)

    @pl.when(j == 5)
    def _():
        v = jax.nn.gelu(z)
        for g in range(MLP_GROUPS):
            cols = slice(g * LANE, (g + 1) * LANE)
            vn = _rms(v[:, cols], mng_ref[:, cols]).astype(BF16)
            for ci in range(tm // MLP_CHUNK):
                rows = slice(ci * MLP_CHUNK, (ci + 1) * MLP_CHUNK)
                s = _dot(sw_ref[g], vn[rows, :]) + sb_ref[g]
                ob_ref[rows, cols] = (u_s[rows, cols].astype(F32) * s).astype(BF16)


def _prep_inproj_weights(w_in, hg):
    wq = DN_HEADS * DN_HEAD_DIM
    n_ab = 4 * DN_HEADS
    w = w_in.astype(BF16)
    w_main = jnp.concatenate([w[:, :4 * wq], w[:, 4 * wq + n_ab:]], axis=1)
    wa = w[:, 4 * wq:4 * wq + 2 * DN_HEADS].reshape(-1, 2, DN_HEADS // hg, hg)
    wb = w[:, 4 * wq + 2 * DN_HEADS:4 * wq + n_ab].reshape(-1, 2, DN_HEADS // hg, hg)
    grp = jnp.concatenate([jnp.moveaxis(wa, 2, 1), jnp.moveaxis(wb, 2, 1)], axis=2)
    grp = grp.reshape(w.shape[0], DN_HEADS // hg, 4 * hg)
    w_ab = jnp.pad(grp, ((0, 0), (0, 0), (0, LANE - 4 * hg))).reshape(w.shape[0], -1)
    return w_main, w_ab


def _inproj_call(h2, w_main, w_ab, sw, sb, mng, *, tm, with_mlp):
    n, d = h2.shape
    cb = 1024
    nab = w_ab.shape[1]
    kern = functools.partial(_inproj_kernel, tm=tm, with_mlp=with_mlp)
    in_specs = [pl.BlockSpec((tm, d), lambda i, j: (i, 0)),
                pl.BlockSpec((d, cb), lambda i, j: (0, j)),
                pl.BlockSpec((d, nab), lambda i, j: (0, 0))]
    out_specs = [pl.BlockSpec((tm, cb), lambda i, j: (i, jnp.minimum(j, 2))),
                 pl.BlockSpec((tm, nab), lambda i, j: (i, 0))]
    out_shape = [jax.ShapeDtypeStruct((n, 3 * cb), BF16), jax.ShapeDtypeStruct((n, nab), F32)]
    args = [h2, w_main, w_ab]
    scratch = []
    if with_mlp:
        in_specs += [pl.BlockSpec((MLP_GROUPS, MLP_CHUNK, MLP_CHUNK), lambda i, j: (0, 0, 0)),
                     pl.BlockSpec((MLP_GROUPS, MLP_CHUNK, LANE), lambda i, j: (0, 0, 0)),
                     pl.BlockSpec((1, cb), lambda i, j: (0, 0))]
        out_specs += [pl.BlockSpec((tm, cb), lambda i, j: (i, 0))] * 2
        out_shape += [jax.ShapeDtypeStruct((n, cb), BF16)] * 2
        args += [sw, sb, mng]
        scratch = [pltpu.VMEM((tm, cb), BF16)]
    return pl.pallas_call(
        kern,
        grid=(n // tm, 6 if with_mlp else 3),
        in_specs=in_specs,
        out_specs=out_specs,
        out_shape=out_shape,
        scratch_shapes=scratch,
        compiler_params=_cparams(("parallel", "arbitrary")),
        name="inproj" if with_mlp else "inproj_ctx",
    )(*args)


def _tri_inverse(lmat, upper, ii, jj):
    c = lmat.shape[0]
    eye = (ii == jj).astype(F32)
    b16 = (ii // 16) == (jj // 16)
    b32 = (ii // 32) == (jj // 32)
    if upper:
        m1 = b32 & ((ii // 16) < (jj // 16))
        m2 = (ii // 32) < (jj // 32)
    else:
        m1 = b32 & ((ii // 16) > (jj // 16))
        m2 = (ii // 32) > (jj // 32)
    a1 = jnp.where(b16, lmat, 0.0)
    a1b = a1.astype(BF16)
    a2 = _dot(a1b, a1b)
    a2b = a2.astype(BF16)
    a4 = _dot(a2b, a2b)
    a4b = a4.astype(BF16)
    a8b = _dot(a4b, a4b).astype(BF16)
    p = eye - a1
    p = p + _dot(p.astype(BF16), a2b)
    p = p + _dot(p.astype(BF16), a4b)
    p = p + _dot(p.astype(BF16), a8b)
    for m in (m1, m2):
        pb = p.astype(BF16)
        xm = _dot(jnp.where(m, lmat, 0.0).astype(BF16), pb)
        p = p - _dot(pb, xm.astype(BF16))
    del c
    return p


def _dn_kernel(qx_ref, kx_ref, vx_ref, qc_ref, kc_ref, vc_ref, abx_ref, abc_ref, gate_ref,
               cwq_ref, cwk_ref, cwv_ref, alog_ref, dtb_ref, hng_ref,
               out_ref,
               qn_s, kn_s, vv_s, col_s, row_s, st_s, o_s, pu_s, pwq_s, pkt_s, pqk_s, pgl_s,
               *, hg, tx, tc):
    C = DN_CHUNK
    D = DN_HEAD_DIM
    ncc, ncx = tc // C, tx // C
    nch = ncc + ncx
    nchain = 2 * hg

    ii = lax.broadcasted_iota(jnp.int32, (C, C), 0)
    jj = lax.broadcasted_iota(jnp.int32, (C, C), 1)
    lane_t = lax.broadcasted_iota(jnp.int32, (C, LANE), 1)

    def conv_phase(srcs, t_len, row0):
        n_ch = t_len // C

        def body(n, carry):
            base = pl.multiple_of(n * C, C)
            pbase = pl.multiple_of(jnp.maximum(base - 16, 0), 16)
            nbase = pl.multiple_of(jnp.minimum(base + C, t_len - 16), 16)
            has_prev = jnp.where(n > 0, 1.0, 0.0)
            has_next = jnp.where(n < n_ch - 1, 1.0, 0.0)
            for src, cw_ref, dst, kind in srcs:
                main = src[0, pl.ds(base, C), :].astype(F32)
                prev = src[0, pl.ds(pbase, 16), :].astype(F32) * has_prev
                nxt = src[0, pl.ds(nbase, 16), :].astype(F32) * has_next
                xcat = jnp.concatenate([prev, main, nxt], axis=0)
                y = cw_ref[0:1, :] * xcat[14:14 + C]
                for j in range(1, CONV_K):
                    y = y + cw_ref[j:j + 1, :] * xcat[14 + j:14 + j + C]
                y = _silu(y)
                if kind != "v":
                    parts = []
                    for hl in range(hg):
                        yh = y[:, hl * D:(hl + 1) * D]
                        yh = yh * lax.rsqrt(jnp.sum(yh * yh, axis=-1, keepdims=True) + EPS)
                        if kind == "q":
                            yh = yh * (D ** -0.5)
                        parts.append(yh)
                    y = jnp.concatenate(parts, axis=1) if hg > 1 else parts[0]
                dst[pl.ds(row0 + base, C), :] = y.astype(BF16)
            return carry

        lax.fori_loop(0, n_ch, body, 0)

    conv_phase(((qc_ref, cwq_ref, qn_s, "q"), (kc_ref, cwk_ref, kn_s, "k"), (vc_ref, cwv_ref, vv_s, "v")), tc, 0)
    conv_phase(((qx_ref, cwq_ref, qn_s, "q"), (kx_ref, cwk_ref, kn_s, "k"), (vx_ref, cwv_ref, vv_s, "v")), tx, tc)

    tril = (ii >= jj).astype(F32)
    triu = (ii <= jj).astype(F32)
    ones = jnp.ones((C, C), F32)
    neg_a = -jnp.exp(alog_ref[0])
    dtb = dtb_ref[0]

    def gate_phase(ab_ref, n_ch, chunk0):
        def body(n, carry):
            t = ab_ref[0, pl.ds(pl.multiple_of(n * C, C), C), :]
            g = neg_a * jax.nn.softplus(t + dtb)
            g = jnp.where(lane_t < nchain, g, 0.0)
            hp = lax.Precision.HIGHEST
            gc_f = jnp.dot(tril, g, precision=hp, preferred_element_type=F32)
            gc_b = jnp.dot(triu, g, precision=hp, preferred_element_type=F32)
            gt = jnp.dot(ones, g, precision=hp, preferred_element_type=F32)
            gc = jnp.where(lane_t < hg, gc_f, gc_b)
            beta = pltpu.roll(jax.nn.sigmoid(t), LANE - nchain, 1)
            col_s[chunk0 + n, 0] = gc
            col_s[chunk0 + n, 1] = beta
            col_s[chunk0 + n, 2] = gt
            row_s[chunk0 + n] = gc.T[0:16, :]
            return carry

        lax.fori_loop(0, n_ch, body, 0)

    gate_phase(abc_ref, ncc, 0)
    gate_phase(abx_ref, ncx, ncc)

    def chunk_of(s):
        cf = s
        cb = jnp.where(s < ncc, ncc - 1 - s, nch + ncc - 1 - s)
        return cf, cb

    def bcast(tile, r):
        return jnp.broadcast_to(tile[:, r:r + 1], (C, LANE))

    def stage1(s, slot):
        cf, cb = chunk_of(s)
        for d in range(2):
            c = cf if d == 0 else cb
            row0 = pl.multiple_of(c * C, C)
            gc_t = col_s[c, 0]
            be_t = col_s[c, 1]
            gt_t = col_s[c, 2]
            e1_t = jnp.exp(gc_t)
            ca_t = be_t * e1_t
            e2_t = jnp.exp(gt_t - gc_t)
            gl_t = jnp.exp(gt_t)
            rows_t = row_s[c]
            tri = (ii <= jj) if d else (ii >= jj)
            strict = (ii < jj) if d else (ii > jj)
            for hl in range(hg):
                r = d * hg + hl
                cols = slice(hl * D, (hl + 1) * D)
                qn = qn_s[pl.ds(row0, C), cols]
                kn = kn_s[pl.ds(row0, C), cols]
                vv = vv_s[pl.ds(row0, C), cols]
                knf = kn.astype(F32)
                diff = bcast(gc_t, r)[:, :C] - rows_t[r:r + 1, :]
                dec = jnp.exp(jnp.where(tri, diff, -1e30))
                gmat = _dot_nt(jnp.concatenate([qn, kn], axis=0), kn)
                qkm = gmat[:C] * dec
                lmat = jnp.where(strict, gmat[C:] * dec, 0.0) * bcast(be_t, r)[:, :C]
                tinv = _tri_inverse(lmat, bool(d), ii, jj)
                vb = (vv.astype(F32) * bcast(be_t, r)).astype(BF16)
                kbg = (knf * bcast(ca_t, r)).astype(BF16)
                uw = _dot(tinv.astype(BF16), jnp.concatenate([vb, kbg], axis=1))
                qd = (qn.astype(F32) * bcast(e1_t, r)).astype(BF16)
                kd = knf * bcast(e2_t, r)
                pu_s[slot, r] = uw[:, :D]
                pwq_s[slot, r] = jnp.concatenate([uw[:, D:].astype(BF16), qd], axis=0)
                pkt_s[slot, r] = kd.T.astype(BF16)
                pqk_s[slot, r] = qkm.astype(BF16)
                pgl_s[slot, r] = jnp.broadcast_to(gl_t[0:8, r:r + 1], (8, LANE))

    def stage2(s, slot):
        cf, cb = chunk_of(s)
        for d in range(2):
            c = cf if d == 0 else cb
            row0 = pl.multiple_of(c * C, C)
            for hl in range(hg):
                r = d * hg + hl
                cols = slice(hl * D, (hl + 1) * D)
                st = st_s[r]
                stb = st.astype(BF16)
                ws = _dot(pwq_s[slot, r], stb)
                vn = (pu_s[slot, r] - ws[:C]).astype(BF16)
                o = ws[C:] + _dot(pqk_s[slot, r], vn)
                gl = pgl_s[slot, r]
                st_s[r] = st * jnp.concatenate([gl] * (D // 8), axis=0) + _dot(pkt_s[slot, r], vn)
                o_s[pl.ds(row0, C), cols] += o

    st_s[...] = jnp.zeros_like(st_s)
    o_s[...] = jnp.zeros_like(o_s)
    stage1(0, 0)

    def loop_body(s, carry):
        stage1(s, s % 2)
        stage2(s - 1, (s - 1) % 2)
        return carry

    lax.fori_loop(1, nch, loop_body, 0)
    stage2(nch - 1, (nch - 1) % 2)

    hn = hng_ref[...]
    for hl in range(hg):
        cols = slice(hl * D, (hl + 1) * D)
        o = o_s[tc:tc + tx, cols]
        zg = gate_ref[0, :, cols].astype(F32)
        out_ref[0, :, cols] = (_rms(o, hn) * _silu(zg)).astype(BF16)


def _dn_call(qkv_x, qkv_c, ab_x, ab_c, gate, conv_w, alog_g, dtb_g, hng, *, hg):
    b, tx, w3 = qkv_x.shape
    tc = qkv_c.shape[1]
    nhg = DN_HEADS // hg
    wd = hg * DN_HEAD_DIM
    nch = (tx + tc) // DN_CHUNK
    nchain = 2 * hg
    kern = functools.partial(_dn_kernel, hg=hg, tx=tx, tc=tc)

    def col(t, off):
        return pl.BlockSpec((1, t, wd), lambda bi, gi: (bi, 0, off * nhg + gi))

    def cw(off):
        return pl.BlockSpec((CONV_K, wd), lambda bi, gi: (0, off * nhg + gi))

    def ab(t):
        return pl.BlockSpec((1, t, LANE), lambda bi, gi: (bi, 0, gi))

    small = pl.BlockSpec((1, 1, LANE), lambda bi, gi: (gi, 0, 0))
    return pl.pallas_call(
        kern,
        grid=(b, nhg),
        in_specs=[col(tx, 0), col(tx, 1), col(tx, 2), col(tc, 0), col(tc, 1), col(tc, 2),
                  ab(tx), ab(tc),
                  pl.BlockSpec((1, tx, wd), lambda bi, gi: (bi, 0, gi)),
                  cw(0), cw(1), cw(2), small, small,
                  pl.BlockSpec((1, LANE), lambda bi, gi: (0, 0))],
        out_specs=pl.BlockSpec((1, tx, wd), lambda bi, gi: (bi, 0, gi)),
        out_shape=jax.ShapeDtypeStruct((b, tx, DN_HEADS * DN_HEAD_DIM), BF16),
        scratch_shapes=[pltpu.VMEM((tc + tx, wd), BF16)] * 3 + [
            pltpu.VMEM((nch, 3, DN_CHUNK, LANE), F32),
            pltpu.VMEM((nch, 16, DN_CHUNK), F32),
            pltpu.VMEM((nchain, DN_HEAD_DIM, DN_HEAD_DIM), F32),
            pltpu.VMEM((tc + tx, wd), F32),
            pltpu.VMEM((2, nchain, DN_CHUNK, DN_HEAD_DIM), F32),
            pltpu.VMEM((2, nchain, 2 * DN_CHUNK, DN_HEAD_DIM), BF16),
            pltpu.VMEM((2, nchain, DN_HEAD_DIM, DN_CHUNK), BF16),
            pltpu.VMEM((2, nchain, DN_CHUNK, DN_CHUNK), BF16),
            pltpu.VMEM((2, nchain, 8, LANE), F32)],
        compiler_params=_cparams(("parallel", "arbitrary")),
        name="deltanet",
    )(qkv_x, qkv_x, qkv_x, qkv_c, qkv_c, qkv_c, ab_x, ab_c, gate, conv_w, conv_w, conv_w, alog_g, dtb_g, hng)


def _outproj_kernel(x_ref, oa_ref, ob_ref, w_ref, m_ref, out_ref):
    half = oa_ref.shape[1]
    y = _dot(oa_ref[...], w_ref[:half, :]) + _dot(ob_ref[...], w_ref[half:, :])
    out_ref[...] = x_ref[...] + m_ref[0, 5:6, :] * y


def _outproj_call(x2d, oa, ob, w, m3, *, rows_per_mod, tm):
    n, d = x2d.shape
    half = oa.shape[1]
    tiles_per_mod = rows_per_mod // tm
    return pl.pallas_call(
        _outproj_kernel,
        grid=(n // tm,),
        in_specs=[pl.BlockSpec((tm, d), lambda i: (i, 0)),
                  pl.BlockSpec((tm, half), lambda i: (i, 0)),
                  pl.BlockSpec((tm, half), lambda i: (i, 0)),
                  pl.BlockSpec((2 * half, d), lambda i: (0, 0), pipeline_mode=pl.Buffered(1)),
                  pl.BlockSpec((1, N_MOD, d), lambda i: (i // tiles_per_mod, 0, 0))],
        out_specs=pl.BlockSpec((tm, d), lambda i: (i, 0)),
        out_shape=jax.ShapeDtypeStruct((n, d), F32),
        compiler_params=_cparams(("parallel",)),
        name="outproj",
    )(x2d, oa, ob, w, m3)


def _chain_rows(p, hg):
    nhg = DN_HEADS // hg
    rows = jnp.moveaxis(p.reshape(2, nhg, hg), 1, 0).reshape(nhg, 1, 2 * hg)
    return jnp.pad(rows, ((0, 0), (0, 0), (0, LANE - 2 * hg)))


FFN_TM = 512
FFN_TF = 512
PROJ_TM = 1024
OUT_TM = 512
DN_HG = 4
MOD_ROWS = 16


def kernel(x, c, ctx, c_ctx, w_mod, b_mod, norm_g, ffn1_w_in, ffn1_w_out, w_in, conv_w, a_log, dt_bias, head_norm_g, spatial_w, spatial_b, mlp_norm_g, w_out, ffn2_w_in, ffn2_w_out, final_g):
    bsz, t, d = x.shape
    tc = ctx.shape[1]
    assert w_mod.shape[0] == 1, "single-layer block"
    assert bsz + 1 <= MOD_ROWS

    c_all = jnp.concatenate([c, c_ctx[None, :], jnp.zeros((MOD_ROWS - bsz - 1, d), F32)], axis=0)
    m3 = _mod_call(c_all, w_mod[0], b_mod).reshape(MOD_ROWS, N_MOD, d)

    g0, g1, g2 = norm_g[0, 0:1], norm_g[0, 1:2], norm_g[0, 2:3]
    wi1, wo1 = _prep_ffn_weights(ffn1_w_in[0], ffn1_w_out[0], FFN_TF)
    wi2, wo2 = _prep_ffn_weights(ffn2_w_in[0], ffn2_w_out[0], FFN_TF)

    x1, h_x = _ffn_call(x.reshape(bsz * t, d), m3, g0, wi1, wo1, g1, mode="ffn1", mod_base=0,
                        rows_per_mod=t, mod_offset=0, tm=FFN_TM, tf=FFN_TF)
    (h_c,) = _ffn_call(ctx.reshape(bsz * tc, d), m3, g0, wi1, wo1, g1, mode="ffn1ctx", mod_base=0,
                       rows_per_mod=bsz * tc, mod_offset=bsz, tm=FFN_TM, tf=FFN_TF)

    w_main, w_ab = _prep_inproj_weights(w_in[0], DN_HG)
    sb = jnp.broadcast_to(spatial_b[0][:, :, None], (MLP_GROUPS, MLP_CHUNK, LANE))
    qkv_x, ab_x, gate_x, o_b = _inproj_call(h_x, w_main, w_ab, spatial_w[0].astype(BF16), sb, mlp_norm_g,
                                            tm=PROJ_TM, with_mlp=True)
    qkv_c, ab_c = _inproj_call(h_c, w_main, w_ab, None, None, None, tm=PROJ_TM, with_mlp=False)

    wq = DN_HEADS * DN_HEAD_DIM
    o_a = _dn_call(qkv_x.reshape(bsz, t, 3 * wq), qkv_c.reshape(bsz, tc, 3 * wq),
                   ab_x.reshape(bsz, t, -1), ab_c.reshape(bsz, tc, -1), gate_x.reshape(bsz, t, wq),
                   conv_w[0], _chain_rows(a_log[0], DN_HG), _chain_rows(dt_bias[0], DN_HG), head_norm_g,
                   hg=DN_HG)

    x2 = _outproj_call(x1, o_a.reshape(bsz * t, wq), o_b, w_out[0].astype(BF16), m3, rows_per_mod=t, tm=OUT_TM)
    (out,) = _ffn_call(x2, m3, g2, wi2, wo2, final_g[None, :], mode="ffn2", mod_base=6,
                       rows_per_mod=t, mod_offset=0, tm=FFN_TM, tf=FFN_TF)
    return out.reshape(bsz, t, d)
```

```python
import functools

import jax
import jax.numpy as jnp
import numpy as np
from jax import lax
from jax.experimental import pallas as pl
from jax.experimental.pallas import tpu as pltpu

F32 = jnp.float32
BF16 = jnp.bfloat16

EPS = 1e-6
LANE = 128
N_MOD = 9
DN_HEADS = 8
DN_HEAD_DIM = 128
DN_CHUNK = 64
CONV_K = 5
MLP_GROUPS = 8
MLP_CHUNK = 128
V7X_VMEM_LIMIT = 60000 * 1024


def _cparams(sem):
    return pltpu.CompilerParams(dimension_semantics=sem, vmem_limit_bytes=V7X_VMEM_LIMIT)


def _rms(x, g):
    return x * lax.rsqrt(jnp.mean(x * x, axis=-1, keepdims=True) + EPS) * g


def _silu(x):
    return x * jax.nn.sigmoid(x)


def _dot(a, b):
    return jnp.dot(a, b, preferred_element_type=F32)


def _dot_nt(a, b):
    return lax.dot_general(a, b, (((1,), (1,)), ((), ())), preferred_element_type=F32)


def _mod_kernel(c_ref, w_ref, b_ref, o_ref):
    s = _silu(c_ref[...]).astype(BF16)
    o_ref[...] = _dot(s, w_ref[...].astype(BF16)) + b_ref[...]


def _mod_call(c_all, w_mod, b_mod, tn=1024):
    rows, d = c_all.shape
    n = w_mod.shape[1]
    return pl.pallas_call(
        _mod_kernel,
        grid=(n // tn,),
        in_specs=[pl.BlockSpec((rows, d), lambda j: (0, 0)),
                  pl.BlockSpec((d, tn), lambda j: (0, j)),
                  pl.BlockSpec((1, tn), lambda j: (0, j))],
        out_specs=pl.BlockSpec((rows, tn), lambda j: (0, j)),
        out_shape=jax.ShapeDtypeStruct((rows, n), F32),
        compiler_params=_cparams(("arbitrary",)),
        name="mod",
    )(c_all, w_mod, b_mod)


def _ffn_kernel(*refs, mode, mod_base, tf, nf):
    if mode == "ffn2":
        x_ref, m_ref, g_ref, wi_ref, wo_ref, g2_ref, out_ref, h_s, acc_s = refs
    elif mode == "ffn1":
        x_ref, m_ref, g_ref, wi_ref, wo_ref, g2_ref, out_ref, h2_ref, h_s, acc_s = refs
    else:
        x_ref, m_ref, g_ref, wi_ref, wo_ref, g2_ref, h2_ref, h_s, acc_s = refs
    j = pl.program_id(1)

    def mvec(k):
        return m_ref[0, k:k + 1, :]

    @pl.when(j == 0)
    def _():
        h = _rms(x_ref[...], g_ref[...]) * (1.0 + mvec(mod_base + 1)) + mvec(mod_base)
        h_s[...] = h.astype(BF16)
        acc_s[...] = jnp.zeros_like(acc_s)

    ab = _dot(h_s[...], wi_ref[...])
    act = (_silu(ab[:, :tf]) * ab[:, tf:]).astype(BF16)
    acc_s[...] += _dot(act, wo_ref[...])

    @pl.when(j == nf - 1)
    def _():
        xn = x_ref[...] + 0.5 * mvec(mod_base + 2) * acc_s[...]
        if mode == "ffn2":
            out_ref[...] = _rms(xn, g2_ref[...])
        else:
            if mode == "ffn1":
                out_ref[...] = xn
            h2 = _rms(xn, g2_ref[...]) * (1.0 + mvec(mod_base + 4)) + mvec(mod_base + 3)
            h2_ref[...] = h2.astype(BF16)


def _prep_ffn_weights(w_in, w_out, tf):
    d, f2 = w_in.shape
    f = f2 // 2
    nf = -(-f // tf)
    pad = nf * tf - f
    a = jnp.pad(w_in[:, :f].astype(BF16), ((0, 0), (0, pad))).reshape(d, nf, 1, tf)
    b = jnp.pad(w_in[:, f:].astype(BF16), ((0, 0), (0, pad))).reshape(d, nf, 1, tf)
    wi = jnp.concatenate([a, b], axis=2).reshape(d, nf * 2 * tf)
    wo = jnp.pad(w_out.astype(BF16), ((0, pad), (0, 0)))
    return wi, wo


def _ffn_call(x2d, m3, g, wi, wo, g2, *, mode, mod_base, rows_per_mod, mod_offset, tm, tf):
    r, d = x2d.shape
    nf = wo.shape[0] // tf
    tiles_per_mod = rows_per_mod // tm
    kern = functools.partial(_ffn_kernel, mode=mode, mod_base=mod_base, tf=tf, nf=nf)
    x_spec = pl.BlockSpec((tm, d), lambda i, j: (i, 0))
    in_specs = [x_spec,
                pl.BlockSpec((1, N_MOD, d), lambda i, j: (mod_offset + i // tiles_per_mod, 0, 0)),
                pl.BlockSpec((1, d), lambda i, j: (0, 0)),
                pl.BlockSpec((d, 2 * tf), lambda i, j: (0, j)),
                pl.BlockSpec((tf, d), lambda i, j: (j, 0)),
                pl.BlockSpec((1, d), lambda i, j: (0, 0))]
    o_f32 = (jax.ShapeDtypeStruct((r, d), F32), x_spec)
    o_bf = (jax.ShapeDtypeStruct((r, d), BF16), x_spec)
    outs = {"ffn2": [o_f32], "ffn1": [o_f32, o_bf], "ffn1ctx": [o_bf]}[mode]
    return pl.pallas_call(
        kern,
        grid=(r // tm, nf),
        in_specs=in_specs,
        out_specs=[o[1] for o in outs],
        out_shape=[o[0] for o in outs],
        scratch_shapes=[pltpu.VMEM((tm, d), BF16), pltpu.VMEM((tm, d), F32)],
        compiler_params=_cparams(("parallel", "arbitrary")),
        name=mode,
    )(x2d, m3, g, wi, wo, g2)


def _inproj_kernel(*refs, tm, with_mlp):
    if with_mlp:
        (h_ref, w_ref, wab_ref, sw_ref, sb_ref, mng_ref,
         qkv_ref, ab_ref, gate_ref, ob_ref, u_s) = refs
    else:
        h_ref, w_ref, wab_ref, qkv_ref, ab_ref = refs
    j = pl.program_id(1)
    z = _dot(h_ref[...], w_ref[...])

    @pl.when(j == 0)
    def _():
        ab_ref[...] = _dot(h_ref[...], wab_ref[...])

    @pl.when(j < 3)
    def _():
        qkv_ref[...] = z.astype(BF16)

    if not with_mlp:
        return

    @pl.when(j == 3)
    def _():
        gate_ref[...] = z.astype(BF16)

    @pl.when(j == 4)
    def _():
        u_s[...] = jax.nn.gelu(z).astype(BF16)

    @pl.when(j == 5)
    def _():
        v = jax.nn.gelu(z)
        for g in range(MLP_GROUPS):
            cols = slice(g * LANE, (g + 1) * LANE)
            vn = _rms(v[:, cols], mng_ref[:, cols]).astype(BF16)
            for ci in range(tm // MLP_CHUNK):
                rows = slice(ci * MLP_CHUNK, (ci + 1) * MLP_CHUNK)
                s = _dot(sw_ref[g], vn[rows, :]) + sb_ref[g]
                ob_ref[rows, cols] = (u_s[rows, cols].astype(F32) * s).astype(BF16)


def _prep_inproj_weights(w_in, hg):
    wq = DN_HEADS * DN_HEAD_DIM
    n_ab = 4 * DN_HEADS
    w = w_in.astype(BF16)
    w_main = jnp.concatenate([w[:, :4 * wq], w[:, 4 * wq + n_ab:]], axis=1)
    wa = w[:, 4 * wq:4 * wq + 2 * DN_HEADS].reshape(-1, 2, DN_HEADS // hg, hg)
    wb = w[:, 4 * wq + 2 * DN_HEADS:4 * wq + n_ab].reshape(-1, 2, DN_HEADS // hg, hg)
    grp = jnp.concatenate([jnp.moveaxis(wa, 2, 1), jnp.moveaxis(wb, 2, 1)], axis=2)
    grp = grp.reshape(w.shape[0], DN_HEADS // hg, 4 * hg)
    w_ab = jnp.pad(grp, ((0, 0), (0, 0), (0, LANE - 4 * hg))).reshape(w.shape[0], -1)
    return w_main, w_ab


def _inproj_call(h2, w_main, w_ab, sw, sb, mng, *, tm, with_mlp):
    n, d = h2.shape
    cb = 1024
    nab = w_ab.shape[1]
    kern = functools.partial(_inproj_kernel, tm=tm, with_mlp=with_mlp)
    in_specs = [pl.BlockSpec((tm, d), lambda i, j: (i, 0)),
                pl.BlockSpec((d, cb), lambda i, j: (0, j)),
                pl.BlockSpec((d, nab), lambda i, j: (0, 0))]
    out_specs = [pl.BlockSpec((tm, cb), lambda i, j: (i, jnp.minimum(j, 2))),
                 pl.BlockSpec((tm, nab), lambda i, j: (i, 0))]
    out_shape = [jax.ShapeDtypeStruct((n, 3 * cb), BF16), jax.ShapeDtypeStruct((n, nab), F32)]
    args = [h2, w_main, w_ab]
    scratch = []
    if with_mlp:
        in_specs += [pl.BlockSpec((MLP_GROUPS, MLP_CHUNK, MLP_CHUNK), lambda i, j: (0, 0, 0)),
                     pl.BlockSpec((MLP_GROUPS, MLP_CHUNK, LANE), lambda i, j: (0, 0, 0)),
                     pl.BlockSpec((1, cb), lambda i, j: (0, 0))]
        out_specs += [pl.BlockSpec((tm, cb), lambda i, j: (i, 0))] * 2
        out_shape += [jax.ShapeDtypeStruct((n, cb), BF16)] * 2
        args += [sw, sb, mng]
        scratch = [pltpu.VMEM((tm, cb), BF16)]
    return pl.pallas_call(
        kern,
        grid=(n // tm, 6 if with_mlp else 3),
        in_specs=in_specs,
        out_specs=out_specs,
        out_shape=out_shape,
        scratch_shapes=scratch,
        compiler_params=_cparams(("parallel", "arbitrary")),
        name="inproj" if with_mlp else "inproj_ctx",
    )(*args)


def _tri_inverse(lmats, uppers, ii, jj):
    eye = (ii == jj).astype(F32)
    b16 = (ii // 16) == (jj // 16)
    b32 = (ii // 32) == (jj // 32)
    m1 = {False: b32 & ((ii // 16) > (jj // 16)), True: b32 & ((ii // 16) < (jj // 16))}
    m2 = {False: (ii // 32) > (jj // 32), True: (ii // 32) < (jj // 32)}
    a1 = [jnp.where(b16, l, 0.0) for l in lmats]
    a1b = [a.astype(BF16) for a in a1]
    a2b = [_dot(a, a).astype(BF16) for a in a1b]
    a4b = [_dot(a, a).astype(BF16) for a in a2b]
    a8b = [_dot(a, a).astype(BF16) for a in a4b]
    p = [eye - a for a in a1]
    for ab in (a2b, a4b, a8b):
        p = [pi + _dot(pi.astype(BF16), a) for pi, a in zip(p, ab)]
    for masks in (m1, m2):
        pb = [pi.astype(BF16) for pi in p]
        xm = [_dot(jnp.where(masks[u], l, 0.0).astype(BF16), b).astype(BF16)
              for l, u, b in zip(lmats, uppers, pb)]
        p = [pi - _dot(b, x) for pi, b, x in zip(p, pb, xm)]
    return p


def _dn_kernel(qx_ref, kx_ref, vx_ref, qc_ref, kc_ref, vc_ref, abx_ref, abc_ref, gate_ref,
               cwq_ref, cwk_ref, cwv_ref, alog_ref, dtb_ref, hng_ref,
               out_ref,
               qn_s, kn_s, vv_s, col_s, row_s, st_s, o_s, pu_s, pwq_s, pkt_s, pqk_s, pgl_s,
               *, hg, tx, tc, unroll):
    C = DN_CHUNK
    D = DN_HEAD_DIM
    ncc, ncx = tc // C, tx // C
    nch = ncc + ncx
    nchain = 2 * hg

    ii = lax.broadcasted_iota(jnp.int32, (C, C), 0)
    jj = lax.broadcasted_iota(jnp.int32, (C, C), 1)
    lane_t = lax.broadcasted_iota(jnp.int32, (C, LANE), 1)

    def conv_phase(srcs, t_len, row0):
        n_ch = t_len // C

        def body(n, carry):
            base = pl.multiple_of(n * C, C)
            pbase = pl.multiple_of(jnp.maximum(base - 16, 0), 16)
            nbase = pl.multiple_of(jnp.minimum(base + C, t_len - 16), 16)
            has_prev = jnp.where(n > 0, 1.0, 0.0)
            has_next = jnp.where(n < n_ch - 1, 1.0, 0.0)
            for src, cw_ref, dst, kind in srcs:
                main = src[0, pl.ds(base, C), :].astype(F32)
                prev = src[0, pl.ds(pbase, 16), :].astype(F32) * has_prev
                nxt = src[0, pl.ds(nbase, 16), :].astype(F32) * has_next
                xcat = jnp.concatenate([prev, main, nxt], axis=0)
                y = cw_ref[0:1, :] * xcat[14:14 + C]
                for j in range(1, CONV_K):
                    y = y + cw_ref[j:j + 1, :] * xcat[14 + j:14 + j + C]
                y = _silu(y)
                if kind != "v":
                    parts = []
                    for hl in range(hg):
                        yh = y[:, hl * D:(hl + 1) * D]
                        yh = yh * lax.rsqrt(jnp.sum(yh * yh, axis=-1, keepdims=True) + EPS)
                        if kind == "q":
                            yh = yh * (D ** -0.5)
                        parts.append(yh)
                    y = jnp.concatenate(parts, axis=1) if hg > 1 else parts[0]
                dst[pl.ds(row0 + base, C), :] = y.astype(BF16)
            return carry

        lax.fori_loop(0, n_ch, body, 0)

    conv_phase(((qc_ref, cwq_ref, qn_s, "q"), (kc_ref, cwk_ref, kn_s, "k"), (vc_ref, cwv_ref, vv_s, "v")), tc, 0)
    conv_phase(((qx_ref, cwq_ref, qn_s, "q"), (kx_ref, cwk_ref, kn_s, "k"), (vx_ref, cwv_ref, vv_s, "v")), tx, tc)

    tril = (ii >= jj).astype(F32)
    triu = (ii <= jj).astype(F32)
    ones = jnp.ones((C, C), F32)
    neg_a = -jnp.exp(alog_ref[0])
    dtb = dtb_ref[0]

    def gate_phase(ab_ref, n_ch, chunk0):
        def body(n, carry):
            t = ab_ref[0, pl.ds(pl.multiple_of(n * C, C), C), :]
            g = neg_a * jax.nn.softplus(t + dtb)
            g = jnp.where(lane_t < nchain, g, 0.0)
            hp = lax.Precision.HIGHEST
            gc_f = jnp.dot(tril, g, precision=hp, preferred_element_type=F32)
            gc_b = jnp.dot(triu, g, precision=hp, preferred_element_type=F32)
            gt = jnp.dot(ones, g, precision=hp, preferred_element_type=F32)
            gc = jnp.where(lane_t < hg, gc_f, gc_b)
            beta = pltpu.roll(jax.nn.sigmoid(t), LANE - nchain, 1)
            col_s[chunk0 + n, 0] = gc
            col_s[chunk0 + n, 1] = beta
            col_s[chunk0 + n, 2] = gt
            row_s[chunk0 + n] = gc.T[0:16, :]
            return carry

        lax.fori_loop(0, n_ch, body, 0)

    gate_phase(abc_ref, ncc, 0)
    gate_phase(abx_ref, ncx, ncc)

    def chunk_of(s):
        cf = s
        cb = jnp.where(s < ncc, ncc - 1 - s, nch + ncc - 1 - s)
        return cf, cb

    def bcast(tile, r):
        return jnp.broadcast_to(tile[:, r:r + 1], (C, LANE))

    def stage1(it, par):
        jobs = []
        for u in range(unroll):
            cf, cb = chunk_of(it * unroll + u)
            for d in range(2):
                c = cf if d == 0 else cb
                gc_t = col_s[c, 0]
                be_t = col_s[c, 1]
                gt_t = col_s[c, 2]
                e1_t = jnp.exp(gc_t)
                tiles = dict(gc=gc_t, be=be_t, e1=e1_t, ca=be_t * e1_t, e2=jnp.exp(gt_t - gc_t),
                             gl=jnp.exp(gt_t), rows=row_s[c])
                for hl in range(hg):
                    jobs.append((par * unroll + u, d * hg + hl, pl.multiple_of(c * C, C), tiles, d, hl))
        tri = {0: ii >= jj, 1: ii <= jj}
        strict = {0: ii > jj, 1: ii < jj}

        def rd(ref, job):
            return ref[pl.ds(job[2], C), job[5] * D:(job[5] + 1) * D]

        qn = [rd(qn_s, j) for j in jobs]
        kn = [rd(kn_s, j) for j in jobs]
        gmat = [_dot_nt(jnp.concatenate([q, k], axis=0), k) for q, k in zip(qn, kn)]
        dec = [jnp.exp(jnp.where(tri[j[4]], bcast(j[3]["gc"], j[1])[:, :C] - j[3]["rows"][j[1]:j[1] + 1, :], -1e30))
               for j in jobs]
        lmat = [jnp.where(strict[j[4]], g[C:] * dc, 0.0) * bcast(j[3]["be"], j[1])[:, :C]
                for j, g, dc in zip(jobs, gmat, dec)]
        for j, g, dc in zip(jobs, gmat, dec):
            pqk_s[j[0], j[1]] = (g[:C] * dc).astype(BF16)
        tinv = _tri_inverse(lmat, [bool(j[4]) for j in jobs], ii, jj)
        knf = [k.astype(F32) for k in kn]
        rhs = [jnp.concatenate([(rd(vv_s, j).astype(F32) * bcast(j[3]["be"], j[1])).astype(BF16),
                                (kf * bcast(j[3]["ca"], j[1])).astype(BF16)], axis=1)
               for j, kf in zip(jobs, knf)]
        uw = [_dot(t.astype(BF16), r) for t, r in zip(tinv, rhs)]
        for j, x, q, kf in zip(jobs, uw, qn, knf):
            slot, r = j[0], j[1]
            qd = (q.astype(F32) * bcast(j[3]["e1"], r)).astype(BF16)
            pu_s[slot, r] = x[:, :D]
            pwq_s[slot, r] = jnp.concatenate([x[:, D:].astype(BF16), qd], axis=0)
            pkt_s[slot, r] = (kf * bcast(j[3]["e2"], r)).T.astype(BF16)
            pgl_s[slot, r] = jnp.broadcast_to(j[3]["gl"][0:8, r:r + 1], (8, LANE))

    def stage2(it, par):
        for u in range(unroll):
            slot = par * unroll + u
            cf, cb = chunk_of(it * unroll + u)
            chains = [(d * hg + hl, pl.multiple_of((cb if d else cf) * C, C), hl)
                      for d in range(2) for hl in range(hg)]
            st = [st_s[r] for r, _, _ in chains]
            ws = [_dot(pwq_s[slot, r], s.astype(BF16)) for (r, _, _), s in zip(chains, st)]
            vn = [(pu_s[slot, r] - w[:C]).astype(BF16) for (r, _, _), w in zip(chains, ws)]
            o = [w[C:] + _dot(pqk_s[slot, r], v) for (r, _, _), w, v in zip(chains, ws, vn)]
            kv = [_dot(pkt_s[slot, r], v) for (r, _, _), v in zip(chains, vn)]
            for (r, row0, hl), s, x, y in zip(chains, st, kv, o):
                st_s[r] = s * jnp.concatenate([pgl_s[slot, r]] * (D // 8), axis=0) + x
                o_s[pl.ds(row0, C), hl * D:(hl + 1) * D] += y

    st_s[...] = jnp.zeros_like(st_s)
    o_s[...] = jnp.zeros_like(o_s)
    stage1(0, 0)

    def loop_body(it, carry):
        stage1(it, it % 2)
        stage2(it - 1, (it - 1) % 2)
        return carry

    n_it = nch // unroll
    lax.fori_loop(1, n_it, loop_body, 0)
    stage2(n_it - 1, (n_it - 1) % 2)

    hn = hng_ref[...]
    for hl in range(hg):
        cols = slice(hl * D, (hl + 1) * D)
        o = o_s[tc:tc + tx, cols]
        zg = gate_ref[0, :, cols].astype(F32)
        out_ref[0, :, cols] = (_rms(o, hn) * _silu(zg)).astype(BF16)


def _dn_call(qkv_x, qkv_c, ab_x, ab_c, gate, conv_w, alog_g, dtb_g, hng, *, hg, unroll):
    b, tx, w3 = qkv_x.shape
    tc = qkv_c.shape[1]
    assert (tc // DN_CHUNK) % unroll == 0 and (tx // DN_CHUNK) % unroll == 0
    nslot = 2 * unroll
    nhg = DN_HEADS // hg
    wd = hg * DN_HEAD_DIM
    nch = (tx + tc) // DN_CHUNK
    nchain = 2 * hg
    kern = functools.partial(_dn_kernel, hg=hg, tx=tx, tc=tc, unroll=unroll)

    def col(t, off):
        return pl.BlockSpec((1, t, wd), lambda bi, gi: (bi, 0, off * nhg + gi))

    def cw(off):
        return pl.BlockSpec((CONV_K, wd), lambda bi, gi: (0, off * nhg + gi))

    def ab(t):
        return pl.BlockSpec((1, t, LANE), lambda bi, gi: (bi, 0, gi))

    small = pl.BlockSpec((1, 1, LANE), lambda bi, gi: (gi, 0, 0))
    return pl.pallas_call(
        kern,
        grid=(b, nhg),
        in_specs=[col(tx, 0), col(tx, 1), col(tx, 2), col(tc, 0), col(tc, 1), col(tc, 2),
                  ab(tx), ab(tc),
                  pl.BlockSpec((1, tx, wd), lambda bi, gi: (bi, 0, gi)),
                  cw(0), cw(1), cw(2), small, small,
                  pl.BlockSpec((1, LANE), lambda bi, gi: (0, 0))],
        out_specs=pl.BlockSpec((1, tx, wd), lambda bi, gi: (bi, 0, gi)),
        out_shape=jax.ShapeDtypeStruct((b, tx, DN_HEADS * DN_HEAD_DIM), BF16),
        scratch_shapes=[pltpu.VMEM((tc + tx, wd), BF16)] * 3 + [
            pltpu.VMEM((nch, 3, DN_CHUNK, LANE), F32),
            pltpu.VMEM((nch, 16, DN_CHUNK), F32),
            pltpu.VMEM((nchain, DN_HEAD_DIM, DN_HEAD_DIM), F32),
            pltpu.VMEM((tc + tx, wd), F32),
            pltpu.VMEM((nslot, nchain, DN_CHUNK, DN_HEAD_DIM), F32),
            pltpu.VMEM((nslot, nchain, 2 * DN_CHUNK, DN_HEAD_DIM), BF16),
            pltpu.VMEM((nslot, nchain, DN_HEAD_DIM, DN_CHUNK), BF16),
            pltpu.VMEM((nslot, nchain, DN_CHUNK, DN_CHUNK), BF16),
            pltpu.VMEM((nslot, nchain, 8, LANE), F32)],
        compiler_params=_cparams(("parallel", "arbitrary")),
        name="deltanet",
    )(qkv_x, qkv_x, qkv_x, qkv_c, qkv_c, qkv_c, ab_x, ab_c, gate, conv_w, conv_w, conv_w, alog_g, dtb_g, hng)


def _outproj_kernel(x_ref, oa_ref, ob_ref, w_ref, m_ref, out_ref):
    half = oa_ref.shape[1]
    y = _dot(oa_ref[...], w_ref[:half, :]) + _dot(ob_ref[...], w_ref[half:, :])
    out_ref[...] = x_ref[...] + m_ref[0, 5:6, :] * y


def _outproj_call(x2d, oa, ob, w, m3, *, rows_per_mod, tm):
    n, d = x2d.shape
    half = oa.shape[1]
    tiles_per_mod = rows_per_mod // tm
    return pl.pallas_call(
        _outproj_kernel,
        grid=(n // tm,),
        in_specs=[pl.BlockSpec((tm, d), lambda i: (i, 0)),
                  pl.BlockSpec((tm, half), lambda i: (i, 0)),
                  pl.BlockSpec((tm, half), lambda i: (i, 0)),
                  pl.BlockSpec((2 * half, d), lambda i: (0, 0), pipeline_mode=pl.Buffered(1)),
                  pl.BlockSpec((1, N_MOD, d), lambda i: (i // tiles_per_mod, 0, 0))],
        out_specs=pl.BlockSpec((tm, d), lambda i: (i, 0)),
        out_shape=jax.ShapeDtypeStruct((n, d), F32),
        compiler_params=_cparams(("parallel",)),
        name="outproj",
    )(x2d, oa, ob, w, m3)


def _chain_rows(p, hg):
    nhg = DN_HEADS // hg
    rows = jnp.moveaxis(p.reshape(2, nhg, hg), 1, 0).reshape(nhg, 1, 2 * hg)
    return jnp.pad(rows, ((0, 0), (0, 0), (0, LANE - 2 * hg)))


FFN_TM = 512
FFN_TF = 512
PROJ_TM = 1024
OUT_TM = 512
DN_HG = 4
DN_UNROLL = 2
MOD_ROWS = 16


def kernel(x, c, ctx, c_ctx, w_mod, b_mod, norm_g, ffn1_w_in, ffn1_w_out, w_in, conv_w, a_log, dt_bias, head_norm_g, spatial_w, spatial_b, mlp_norm_g, w_out, ffn2_w_in, ffn2_w_out, final_g):
    bsz, t, d = x.shape
    tc = ctx.shape[1]
    assert w_mod.shape[0] == 1, "single-layer block"
    assert bsz + 1 <= MOD_ROWS

    c_all = jnp.concatenate([c, c_ctx[None, :], jnp.zeros((MOD_ROWS - bsz - 1, d), F32)], axis=0)
    m3 = _mod_call(c_all, w_mod[0], b_mod).reshape(MOD_ROWS, N_MOD, d)

    g0, g1, g2 = norm_g[0, 0:1], norm_g[0, 1:2], norm_g[0, 2:3]
    wi1, wo1 = _prep_ffn_weights(ffn1_w_in[0], ffn1_w_out[0], FFN_TF)
    wi2, wo2 = _prep_ffn_weights(ffn2_w_in[0], ffn2_w_out[0], FFN_TF)

    x1, h_x = _ffn_call(x.reshape(bsz * t, d), m3, g0, wi1, wo1, g1, mode="ffn1", mod_base=0,
                        rows_per_mod=t, mod_offset=0, tm=FFN_TM, tf=FFN_TF)
    (h_c,) = _ffn_call(ctx.reshape(bsz * tc, d), m3, g0, wi1, wo1, g1, mode="ffn1ctx", mod_base=0,
                       rows_per_mod=bsz * tc, mod_offset=bsz, tm=FFN_TM, tf=FFN_TF)

    w_main, w_ab = _prep_inproj_weights(w_in[0], DN_HG)
    sb = jnp.broadcast_to(spatial_b[0][:, :, None], (MLP_GROUPS, MLP_CHUNK, LANE))
    qkv_x, ab_x, gate_x, o_b = _inproj_call(h_x, w_main, w_ab, spatial_w[0].astype(BF16), sb, mlp_norm_g,
                                            tm=PROJ_TM, with_mlp=True)
    qkv_c, ab_c = _inproj_call(h_c, w_main, w_ab, None, None, None, tm=PROJ_TM, with_mlp=False)

    wq = DN_HEADS * DN_HEAD_DIM
    o_a = _dn_call(qkv_x.reshape(bsz, t, 3 * wq), qkv_c.reshape(bsz, tc, 3 * wq),
                   ab_x.reshape(bsz, t, -1), ab_c.reshape(bsz, tc, -1), gate_x.reshape(bsz, t, wq),
                   conv_w[0], _chain_rows(a_log[0], DN_HG), _chain_rows(dt_bias[0], DN_HG), head_norm_g,
                   hg=DN_HG, unroll=DN_UNROLL)

    x2 = _outproj_call(x1, o_a.reshape(bsz * t, wq), o_b, w_out[0].astype(BF16), m3, rows_per_mod=t, tm=OUT_TM)
    (out,) = _ffn_call(x2, m3, g2, wi2, wo2, final_g[None, :], mode="ffn2", mod_base=6,
                       rows_per_mod=t, mod_offset=0, tm=FFN_TM, tf=FFN_TF)
    return out.reshape(bsz, t, d)
```

```python
import functools

import jax
import jax.numpy as jnp
import numpy as np
from jax import lax
from jax.experimental import pallas as pl
from jax.experimental.pallas import tpu as pltpu

F32 = jnp.float32
BF16 = jnp.bfloat16

EPS = 1e-6
LANE = 128
N_MOD = 9
DN_HEADS = 8
DN_HEAD_DIM = 128
DN_CHUNK = 64
CONV_K = 5
MLP_GROUPS = 8
MLP_CHUNK = 128
V7X_VMEM_LIMIT = 60000 * 1024
ROW_CHUNK = 16
ROW_UNROLL = 8


def _cparams(sem):
    return pltpu.CompilerParams(dimension_semantics=sem, vmem_limit_bytes=V7X_VMEM_LIMIT)


def _rms(x, g):
    return x * lax.rsqrt(jnp.mean(x * x, axis=-1, keepdims=True) + EPS) * g


def _silu(x):
    return x * jax.nn.sigmoid(x)


def _dot(a, b):
    return jnp.dot(a, b, preferred_element_type=F32)


def _dot_nt(a, b):
    return lax.dot_general(a, b, (((1,), (1,)), ((), ())), preferred_element_type=F32)


def _mod_kernel(c_ref, w_ref, b_ref, o_ref):
    s = _silu(c_ref[...]).astype(BF16)
    o_ref[...] = _dot(s, w_ref[...].astype(BF16)) + b_ref[...]


def _mod_call(c_all, w_mod, b_mod, tn=1024):
    rows, d = c_all.shape
    n = w_mod.shape[1]
    return pl.pallas_call(
        _mod_kernel,
        grid=(n // tn,),
        in_specs=[pl.BlockSpec((rows, d), lambda j: (0, 0)),
                  pl.BlockSpec((d, tn), lambda j: (0, j)),
                  pl.BlockSpec((1, tn), lambda j: (0, j))],
        out_specs=pl.BlockSpec((rows, tn), lambda j: (0, j)),
        out_shape=jax.ShapeDtypeStruct((rows, n), F32),
        compiler_params=_cparams(("arbitrary",)),
        name="mod",
    )(c_all, w_mod, b_mod)


def _ffn_kernel(*refs, mode, mod_base, nf):
    if mode == "ffn2":
        (x_ref, m_ref, g_ref, wa_ref, wb_ref, wo_ref, g2_ref, oa_ref, ob_ref, wmix_ref,
         out_ref, h_s, acc_s, vec_s) = refs
    elif mode == "ffn1":
        x_ref, m_ref, g_ref, wa_ref, wb_ref, wo_ref, g2_ref, out_ref, h2_ref, h_s, acc_s, vec_s = refs
    else:
        x_ref, m_ref, g_ref, wa_ref, wb_ref, wo_ref, g2_ref, h2_ref, h_s, acc_s, vec_s = refs
    j = pl.program_id(1)
    tm, d = h_s.shape

    def mvec(k):
        return m_ref[0, k:k + 1, :]

    def tile_rows(v):
        return jnp.concatenate([v] * (ROW_CHUNK // 8), axis=0)

    def norm_scale(x, gain, shift=None):
        y = x * lax.rsqrt(jnp.mean(x * x, axis=-1, keepdims=True) + EPS) * tile_rows(gain)
        return y if shift is None else y + tile_rows(shift)

    def row_loop(fn):
        def body(r, carry):
            fn(pl.ds(pl.multiple_of(r * ROW_CHUNK, ROW_CHUNK), ROW_CHUNK))
            return carry
        lax.fori_loop(0, tm // ROW_CHUNK, body, 0, unroll=ROW_UNROLL)

    @pl.when(j == 0)
    def _():
        def put(k, v):
            vec_s[k] = jnp.broadcast_to(v, (8, d))
        put(0, g_ref[...] * (1.0 + mvec(mod_base + 1)))
        put(1, mvec(mod_base))
        put(2, 0.5 * mvec(mod_base + 2))
        if mode == "ffn2":
            put(3, g2_ref[...])
            put(5, mvec(5))
            half = oa_ref.shape[1]
            acc_s[...] = _dot(oa_ref[...], wmix_ref[:half, :]) + _dot(ob_ref[...], wmix_ref[half:, :])
        else:
            put(3, g2_ref[...] * (1.0 + mvec(mod_base + 4)))
            put(4, mvec(mod_base + 3))

        def prologue(rows):
            x = x_ref[rows, :]
            if mode == "ffn2":
                x = x + tile_rows(vec_s[5]) * acc_s[rows, :]
                out_ref[rows, :] = x
            h_s[rows, :] = norm_scale(x, vec_s[0], vec_s[1]).astype(BF16)
            acc_s[rows, :] = jnp.zeros((ROW_CHUNK, d), F32)

        row_loop(prologue)

    h = h_s[...]
    act = (_silu(_dot(h, wa_ref[...])) * _dot(h, wb_ref[...])).astype(BF16)
    acc_s[...] += _dot(act, wo_ref[...])

    @pl.when(j == nf - 1)
    def _():
        def epilogue(rows):
            x = out_ref[rows, :] if mode == "ffn2" else x_ref[rows, :]
            xn = x + tile_rows(vec_s[2]) * acc_s[rows, :]
            if mode == "ffn2":
                out_ref[rows, :] = norm_scale(xn, vec_s[3])
            else:
                if mode == "ffn1":
                    out_ref[rows, :] = xn
                h2_ref[rows, :] = norm_scale(xn, vec_s[3], vec_s[4]).astype(BF16)

        row_loop(epilogue)


def _prep_ffn_weights(w_in, w_out, tf):
    f = w_in.shape[1] // 2
    pad = -f % tf
    wa = jnp.pad(w_in[:, :f].astype(BF16), ((0, 0), (0, pad)))
    wb = jnp.pad(w_in[:, f:].astype(BF16), ((0, 0), (0, pad)))
    wo = jnp.pad(w_out.astype(BF16), ((0, pad), (0, 0)))
    return wa, wb, wo


def _ffn_call(x2d, m3, g, w3, g2, mix=None, *, mode, mod_base, rows_per_mod, mod_offset, tm, tf):
    r, d = x2d.shape
    wa, wb, wo = w3
    nf = wo.shape[0] // tf
    tiles_per_mod = rows_per_mod // tm
    kern = functools.partial(_ffn_kernel, mode=mode, mod_base=mod_base, nf=nf)
    x_spec = pl.BlockSpec((tm, d), lambda i, j: (i, 0))
    in_specs = [x_spec,
                pl.BlockSpec((1, N_MOD, d), lambda i, j: (mod_offset + i // tiles_per_mod, 0, 0)),
                pl.BlockSpec((1, d), lambda i, j: (0, 0)),
                pl.BlockSpec((d, tf), lambda i, j: (0, j)),
                pl.BlockSpec((d, tf), lambda i, j: (0, j)),
                pl.BlockSpec((tf, d), lambda i, j: (j, 0)),
                pl.BlockSpec((1, d), lambda i, j: (0, 0))]
    args = [x2d, m3, g, wa, wb, wo, g2]
    if mode == "ffn2":
        oa, ob, wmix = mix
        half = oa.shape[1]
        in_specs += [pl.BlockSpec((tm, half), lambda i, j: (i, 0)),
                     pl.BlockSpec((tm, half), lambda i, j: (i, 0)),
                     pl.BlockSpec((2 * half, d), lambda i, j: (0, 0), pipeline_mode=pl.Buffered(1))]
        args += [oa, ob, wmix]
    o_f32 = (jax.ShapeDtypeStruct((r, d), F32), x_spec)
    o_bf = (jax.ShapeDtypeStruct((r, d), BF16), x_spec)
    outs = {"ffn2": [o_f32], "ffn1": [o_f32, o_bf], "ffn1ctx": [o_bf]}[mode]
    return pl.pallas_call(
        kern,
        grid=(r // tm, nf),
        in_specs=in_specs,
        out_specs=[o[1] for o in outs],
        out_shape=[o[0] for o in outs],
        scratch_shapes=[pltpu.VMEM((tm, d), BF16), pltpu.VMEM((tm, d), F32), pltpu.VMEM((6, 8, d), F32)],
        compiler_params=_cparams(("parallel", "arbitrary")),
        name=mode,
    )(*args)


def _inproj_kernel(*refs, tm, with_mlp):
    if with_mlp:
        (h_ref, w_ref, wab_ref, sw_ref, sb_ref, mng_ref,
         qkv_ref, ab_ref, gate_ref, ob_ref, u_s) = refs
    else:
        h_ref, w_ref, wab_ref, qkv_ref, ab_ref = refs
    j = pl.program_id(1)
    z = _dot(h_ref[...], w_ref[...])

    @pl.when(j == 0)
    def _():
        ab_ref[...] = _dot(h_ref[...], wab_ref[...])

    @pl.when(j < 3)
    def _():
        qkv_ref[...] = z.astype(BF16)

    if not with_mlp:
        return

    @pl.when(j == 3)
    def _():
        gate_ref[...] = z.astype(BF16)

    @pl.when(j == 4)
    def _():
        u_s[...] = jax.nn.gelu(z).astype(BF16)

    @pl.when(j == 5)
    def _():
        v = jax.nn.gelu(z)
        for g in range(MLP_GROUPS):
            cols = slice(g * LANE, (g + 1) * LANE)
            vn = _rms(v[:, cols], mng_ref[:, cols]).astype(BF16)
            for ci in range(tm // MLP_CHUNK):
                rows = slice(ci * MLP_CHUNK, (ci + 1) * MLP_CHUNK)
                s = _dot(sw_ref[g], vn[rows, :]) + sb_ref[g]
                ob_ref[rows, cols] = (u_s[rows, cols].astype(F32) * s).astype(BF16)


def _prep_inproj_weights(w_in, hg):
    wq = DN_HEADS * DN_HEAD_DIM
    n_ab = 4 * DN_HEADS
    w = w_in.astype(BF16)
    w_main = jnp.concatenate([w[:, :4 * wq], w[:, 4 * wq + n_ab:]], axis=1)
    wa = w[:, 4 * wq:4 * wq + 2 * DN_HEADS].reshape(-1, 2, DN_HEADS // hg, hg)
    wb = w[:, 4 * wq + 2 * DN_HEADS:4 * wq + n_ab].reshape(-1, 2, DN_HEADS // hg, hg)
    grp = jnp.concatenate([jnp.moveaxis(wa, 2, 1), jnp.moveaxis(wb, 2, 1)], axis=2)
    grp = grp.reshape(w.shape[0], DN_HEADS // hg, 4 * hg)
    w_ab = jnp.pad(grp, ((0, 0), (0, 0), (0, LANE - 4 * hg))).reshape(w.shape[0], -1)
    return w_main, w_ab


def _inproj_call(h2, w_main, w_ab, sw, sb, mng, *, tm, with_mlp):
    n, d = h2.shape
    cb = 1024
    nab = w_ab.shape[1]
    kern = functools.partial(_inproj_kernel, tm=tm, with_mlp=with_mlp)
    in_specs = [pl.BlockSpec((tm, d), lambda i, j: (i, 0)),
                pl.BlockSpec((d, cb), lambda i, j: (0, j)),
                pl.BlockSpec((d, nab), lambda i, j: (0, 0))]
    out_specs = [pl.BlockSpec((tm, cb), lambda i, j: (i, jnp.minimum(j, 2))),
                 pl.BlockSpec((tm, nab), lambda i, j: (i, 0))]
    out_shape = [jax.ShapeDtypeStruct((n, 3 * cb), BF16), jax.ShapeDtypeStruct((n, nab), F32)]
    args = [h2, w_main, w_ab]
    scratch = []
    if with_mlp:
        in_specs += [pl.BlockSpec((MLP_GROUPS, MLP_CHUNK, MLP_CHUNK), lambda i, j: (0, 0, 0)),
                     pl.BlockSpec((MLP_GROUPS, MLP_CHUNK, LANE), lambda i, j: (0, 0, 0)),
                     pl.BlockSpec((1, cb), lambda i, j: (0, 0))]
        out_specs += [pl.BlockSpec((tm, cb), lambda i, j: (i, 0))] * 2
        out_shape += [jax.ShapeDtypeStruct((n, cb), BF16)] * 2
        args += [sw, sb, mng]
        scratch = [pltpu.VMEM((tm, cb), BF16)]
    return pl.pallas_call(
        kern,
        grid=(n // tm, 6 if with_mlp else 3),
        in_specs=in_specs,
        out_specs=out_specs,
        out_shape=out_shape,
        scratch_shapes=scratch,
        compiler_params=_cparams(("parallel", "arbitrary")),
        name="inproj" if with_mlp else "inproj_ctx",
    )(*args)


def _tri_inverse(lmats, uppers, ii, jj):
    eye = (ii == jj).astype(F32)
    b16 = (ii // 16) == (jj // 16)
    b32 = (ii // 32) == (jj // 32)
    m1 = {False: b32 & ((ii // 16) > (jj // 16)), True: b32 & ((ii // 16) < (jj // 16))}
    m2 = {False: (ii // 32) > (jj // 32), True: (ii // 32) < (jj // 32)}
    a1 = [jnp.where(b16, l, 0.0) for l in lmats]
    a1b = [a.astype(BF16) for a in a1]
    a2b = [_dot(a, a).astype(BF16) for a in a1b]
    a4b = [_dot(a, a).astype(BF16) for a in a2b]
    a8b = [_dot(a, a).astype(BF16) for a in a4b]
    p = [eye - a for a in a1]
    for ab in (a2b, a4b, a8b):
        p = [pi + _dot(pi.astype(BF16), a) for pi, a in zip(p, ab)]
    for masks in (m1, m2):
        pb = [pi.astype(BF16) for pi in p]
        xm = [_dot(jnp.where(masks[u], l, 0.0).astype(BF16), b).astype(BF16)
              for l, u, b in zip(lmats, uppers, pb)]
        p = [pi - _dot(b, x) for pi, b, x in zip(p, pb, xm)]
    return p


def _dn_kernel(qx_ref, kx_ref, vx_ref, qc_ref, kc_ref, vc_ref, abx_ref, abc_ref, gate_ref,
               cwq_ref, cwk_ref, cwv_ref, alog_ref, dtb_ref, hng_ref,
               out_ref,
               qn_s, kn_s, vv_s, col_s, row_s, st_s, o_s, pu_s, pwq_s, pkt_s, pqk_s, pgl_s,
               *, hg, tx, tc, unroll):
    C = DN_CHUNK
    D = DN_HEAD_DIM
    ncc, ncx = tc // C, tx // C
    nch = ncc + ncx
    nchain = 2 * hg

    ii = lax.broadcasted_iota(jnp.int32, (C, C), 0)
    jj = lax.broadcasted_iota(jnp.int32, (C, C), 1)
    lane_t = lax.broadcasted_iota(jnp.int32, (C, LANE), 1)

    def conv_phase(srcs, t_len, row0):
        n_ch = t_len // C

        def body(n, carry):
            base = pl.multiple_of(n * C, C)
            pbase = pl.multiple_of(jnp.maximum(base - 16, 0), 16)
            nbase = pl.multiple_of(jnp.minimum(base + C, t_len - 16), 16)
            has_prev = jnp.where(n > 0, 1.0, 0.0)
            has_next = jnp.where(n < n_ch - 1, 1.0, 0.0)
            for src, cw_ref, dst, kind in srcs:
                main = src[0, pl.ds(base, C), :].astype(F32)
                prev = src[0, pl.ds(pbase, 16), :].astype(F32) * has_prev
                nxt = src[0, pl.ds(nbase, 16), :].astype(F32) * has_next
                xcat = jnp.concatenate([prev, main, nxt], axis=0)
                y = cw_ref[0:1, :] * xcat[14:14 + C]
                for j in range(1, CONV_K):
                    y = y + cw_ref[j:j + 1, :] * xcat[14 + j:14 + j + C]
                y = _silu(y)
                if kind != "v":
                    parts = []
                    for hl in range(hg):
                        yh = y[:, hl * D:(hl + 1) * D]
                        yh = yh * lax.rsqrt(jnp.sum(yh * yh, axis=-1, keepdims=True) + EPS)
                        if kind == "q":
                            yh = yh * (D ** -0.5)
                        parts.append(yh)
                    y = jnp.concatenate(parts, axis=1) if hg > 1 else parts[0]
                dst[pl.ds(row0 + base, C), :] = y.astype(BF16)
            return carry

        lax.fori_loop(0, n_ch, body, 0)

    conv_phase(((qc_ref, cwq_ref, qn_s, "q"), (kc_ref, cwk_ref, kn_s, "k"), (vc_ref, cwv_ref, vv_s, "v")), tc, 0)
    conv_phase(((qx_ref, cwq_ref, qn_s, "q"), (kx_ref, cwk_ref, kn_s, "k"), (vx_ref, cwv_ref, vv_s, "v")), tx, tc)

    tril = (ii >= jj).astype(F32)
    triu = (ii <= jj).astype(F32)
    ones = jnp.ones((C, C), F32)
    sum_mats = jnp.concatenate([tril, triu, ones], axis=0).astype(BF16)
    neg_a = -jnp.exp(alog_ref[0])
    dtb = dtb_ref[0]

    def gate_phase(ab_ref, n_ch, chunk0):
        def body(n, carry):
            t = ab_ref[0, pl.ds(pl.multiple_of(n * C, C), C), :]
            g = neg_a * jax.nn.softplus(t + dtb)
            g = jnp.where(lane_t < nchain, g, 0.0)
            hi = g.astype(BF16)
            r1 = g - hi.astype(F32)
            mid = r1.astype(BF16)
            lo = (r1 - mid.astype(F32)).astype(BF16)
            cs = _dot(sum_mats, jnp.concatenate([hi, mid, lo], axis=1))
            cs = cs[:, :LANE] + cs[:, LANE:2 * LANE] + cs[:, 2 * LANE:]
            gt = cs[2 * C:]
            gc = jnp.where(lane_t < hg, cs[:C], cs[C:2 * C])
            beta = pltpu.roll(jax.nn.sigmoid(t), LANE - nchain, 1)
            col_s[chunk0 + n, 0] = gc
            col_s[chunk0 + n, 1] = beta
            col_s[chunk0 + n, 2] = gt
            row_s[chunk0 + n] = gc.T[0:16, :]
            return carry

        lax.fori_loop(0, n_ch, body, 0, unroll=4)

    gate_phase(abc_ref, ncc, 0)
    gate_phase(abx_ref, ncx, ncc)

    def chunk_of(s):
        cf = s
        cb = jnp.where(s < ncc, ncc - 1 - s, nch + ncc - 1 - s)
        return cf, cb

    def bcast(tile, r):
        return jnp.broadcast_to(tile[:, r:r + 1], (C, LANE))

    def stage1(it, par):
        jobs = []
        for u in range(unroll):
            cf, cb = chunk_of(it * unroll + u)
            for d in range(2):
                c = cf if d == 0 else cb
                gc_t = col_s[c, 0]
                be_t = col_s[c, 1]
                gt_t = col_s[c, 2]
                e1_t = jnp.exp(gc_t)
                tiles = dict(gc=gc_t, be=be_t, e1=e1_t, ca=be_t * e1_t, e2=jnp.exp(gt_t - gc_t),
                             gl=jnp.exp(gt_t), rows=row_s[c])
                for hl in range(hg):
                    jobs.append((par * unroll + u, d * hg + hl, pl.multiple_of(c * C, C), tiles, d, hl))
        tri = {0: ii >= jj, 1: ii <= jj}
        strict = {0: ii > jj, 1: ii < jj}

        def rd(ref, job):
            return ref[pl.ds(job[2], C), job[5] * D:(job[5] + 1) * D]

        qn = [rd(qn_s, j) for j in jobs]
        kn = [rd(kn_s, j) for j in jobs]
        gmat = [_dot_nt(jnp.concatenate([q, k], axis=0), k) for q, k in zip(qn, kn)]
        dec = [jnp.exp(jnp.where(tri[j[4]], bcast(j[3]["gc"], j[1])[:, :C] - j[3]["rows"][j[1]:j[1] + 1, :], -1e30))
               for j in jobs]
        lmat = [jnp.where(strict[j[4]], g[C:] * dc, 0.0) * bcast(j[3]["be"], j[1])[:, :C]
                for j, g, dc in zip(jobs, gmat, dec)]
        for j, g, dc in zip(jobs, gmat, dec):
            pqk_s[j[0], j[1]] = (g[:C] * dc).astype(BF16)
        tinv = _tri_inverse(lmat, [bool(j[4]) for j in jobs], ii, jj)
        knf = [k.astype(F32) for k in kn]
        rhs = [jnp.concatenate([(rd(vv_s, j).astype(F32) * bcast(j[3]["be"], j[1])).astype(BF16),
                                (kf * bcast(j[3]["ca"], j[1])).astype(BF16)], axis=1)
               for j, kf in zip(jobs, knf)]
        uw = [_dot(t.astype(BF16), r) for t, r in zip(tinv, rhs)]
        for j, x, q, kf in zip(jobs, uw, qn, knf):
            slot, r = j[0], j[1]
            qd = (q.astype(F32) * bcast(j[3]["e1"], r)).astype(BF16)
            pu_s[slot, r] = x[:, :D]
            pwq_s[slot, r] = jnp.concatenate([x[:, D:].astype(BF16), qd], axis=0)
            pkt_s[slot, r] = (kf * bcast(j[3]["e2"], r)).T.astype(BF16)
            pgl_s[slot, r] = jnp.broadcast_to(j[3]["gl"][0:8, r:r + 1], (8, LANE))

    def stage2(it, par):
        for u in range(unroll):
            slot = par * unroll + u
            cf, cb = chunk_of(it * unroll + u)
            chains = [(d * hg + hl, pl.multiple_of((cb if d else cf) * C, C), hl)
                      for d in range(2) for hl in range(hg)]
            st = [st_s[r] for r, _, _ in chains]
            ws = [_dot(pwq_s[slot, r], s.astype(BF16)) for (r, _, _), s in zip(chains, st)]
            vn = [(pu_s[slot, r] - w[:C]).astype(BF16) for (r, _, _), w in zip(chains, ws)]
            o = [w[C:] + _dot(pqk_s[slot, r], v) for (r, _, _), w, v in zip(chains, ws, vn)]
            kv = [_dot(pkt_s[slot, r], v) for (r, _, _), v in zip(chains, vn)]
            for (r, row0, hl), s, x, y in zip(chains, st, kv, o):
                st_s[r] = s * jnp.concatenate([pgl_s[slot, r]] * (D // 8), axis=0) + x
                o_s[pl.ds(row0, C), hl * D:(hl + 1) * D] += y

    st_s[...] = jnp.zeros_like(st_s)
    o_s[...] = jnp.zeros_like(o_s)
    stage1(0, 0)

    def loop_body(it, carry):
        stage1(it, it % 2)
        stage2(it - 1, (it - 1) % 2)
        return carry

    n_it = nch // unroll
    lax.fori_loop(1, n_it, loop_body, 0)
    stage2(n_it - 1, (n_it - 1) % 2)

    hn = hng_ref[...]
    for hl in range(hg):
        cols = slice(hl * D, (hl + 1) * D)
        o = o_s[tc:tc + tx, cols]
        zg = gate_ref[0, :, cols].astype(F32)
        out_ref[0, :, cols] = (_rms(o, hn) * _silu(zg)).astype(BF16)


def _dn_call(qkv_x, qkv_c, ab_x, ab_c, gate, conv_w, alog_g, dtb_g, hng, *, hg, unroll):
    b, tx, w3 = qkv_x.shape
    tc = qkv_c.shape[1]
    assert (tc // DN_CHUNK) % unroll == 0 and (tx // DN_CHUNK) % unroll == 0
    nslot = 2 * unroll
    nhg = DN_HEADS // hg
    wd = hg * DN_HEAD_DIM
    nch = (tx + tc) // DN_CHUNK
    nchain = 2 * hg
    kern = functools.partial(_dn_kernel, hg=hg, tx=tx, tc=tc, unroll=unroll)

    def col(t, off):
        return pl.BlockSpec((1, t, wd), lambda bi, gi: (bi, 0, off * nhg + gi))

    def cw(off):
        return pl.BlockSpec((CONV_K, wd), lambda bi, gi: (0, off * nhg + gi))

    def ab(t):
        return pl.BlockSpec((1, t, LANE), lambda bi, gi: (bi, 0, gi))

    small = pl.BlockSpec((1, 1, LANE), lambda bi, gi: (gi, 0, 0))
    return pl.pallas_call(
        kern,
        grid=(b, nhg),
        in_specs=[col(tx, 0), col(tx, 1), col(tx, 2), col(tc, 0), col(tc, 1), col(tc, 2),
                  ab(tx), ab(tc),
                  pl.BlockSpec((1, tx, wd), lambda bi, gi: (bi, 0, gi)),
                  cw(0), cw(1), cw(2), small, small,
                  pl.BlockSpec((1, LANE), lambda bi, gi: (0, 0))],
        out_specs=pl.BlockSpec((1, tx, wd), lambda bi, gi: (bi, 0, gi)),
        out_shape=jax.ShapeDtypeStruct((b, tx, DN_HEADS * DN_HEAD_DIM), BF16),
        scratch_shapes=[pltpu.VMEM((tc + tx, wd), BF16)] * 3 + [
            pltpu.VMEM((nch, 3, DN_CHUNK, LANE), F32),
            pltpu.VMEM((nch, 16, DN_CHUNK), F32),
            pltpu.VMEM((nchain, DN_HEAD_DIM, DN_HEAD_DIM), F32),
            pltpu.VMEM((tc + tx, wd), F32),
            pltpu.VMEM((nslot, nchain, DN_CHUNK, DN_HEAD_DIM), F32),
            pltpu.VMEM((nslot, nchain, 2 * DN_CHUNK, DN_HEAD_DIM), BF16),
            pltpu.VMEM((nslot, nchain, DN_HEAD_DIM, DN_CHUNK), BF16),
            pltpu.VMEM((nslot, nchain, DN_CHUNK, DN_CHUNK), BF16),
            pltpu.VMEM((nslot, nchain, 8, LANE), F32)],
        compiler_params=_cparams(("parallel", "arbitrary")),
        name="deltanet",
    )(qkv_x, qkv_x, qkv_x, qkv_c, qkv_c, qkv_c, ab_x, ab_c, gate, conv_w, conv_w, conv_w, alog_g, dtb_g, hng)


def _chain_rows(p, hg):
    nhg = DN_HEADS // hg
    rows = jnp.moveaxis(p.reshape(2, nhg, hg), 1, 0).reshape(nhg, 1, 2 * hg)
    return jnp.pad(rows, ((0, 0), (0, 0), (0, LANE - 2 * hg)))


FFN_TM = 512
FFN_TF = 512
PROJ_TM = 1024
DN_HG = 4
DN_UNROLL = 4
MOD_ROWS = 16


def kernel(x, c, ctx, c_ctx, w_mod, b_mod, norm_g, ffn1_w_in, ffn1_w_out, w_in, conv_w, a_log, dt_bias, head_norm_g, spatial_w, spatial_b, mlp_norm_g, w_out, ffn2_w_in, ffn2_w_out, final_g):
    bsz, t, d = x.shape
    tc = ctx.shape[1]
    assert w_mod.shape[0] == 1, "single-layer block"
    assert bsz + 1 <= MOD_ROWS

    c_all = jnp.concatenate([c, c_ctx[None, :], jnp.zeros((MOD_ROWS - bsz - 1, d), F32)], axis=0)
    m3 = _mod_call(c_all, w_mod[0], b_mod).reshape(MOD_ROWS, N_MOD, d)

    g0, g1, g2 = norm_g[0, 0:1], norm_g[0, 1:2], norm_g[0, 2:3]
    w_ffn1 = _prep_ffn_weights(ffn1_w_in[0], ffn1_w_out[0], FFN_TF)
    w_ffn2 = _prep_ffn_weights(ffn2_w_in[0], ffn2_w_out[0], FFN_TF)

    x1, h_x = _ffn_call(x.reshape(bsz * t, d), m3, g0, w_ffn1, g1, mode="ffn1", mod_base=0,
                        rows_per_mod=t, mod_offset=0, tm=FFN_TM, tf=FFN_TF)
    (h_c,) = _ffn_call(ctx.reshape(bsz * tc, d), m3, g0, w_ffn1, g1, mode="ffn1ctx", mod_base=0,
                       rows_per_mod=bsz * tc, mod_offset=bsz, tm=FFN_TM, tf=FFN_TF)

    w_main, w_ab = _prep_inproj_weights(w_in[0], DN_HG)
    sb = jnp.broadcast_to(spatial_b[0][:, :, None], (MLP_GROUPS, MLP_CHUNK, LANE))
    qkv_x, ab_x, gate_x, o_b = _inproj_call(h_x, w_main, w_ab, spatial_w[0].astype(BF16), sb, mlp_norm_g,
                                            tm=PROJ_TM, with_mlp=True)
    qkv_c, ab_c = _inproj_call(h_c, w_main, w_ab, None, None, None, tm=PROJ_TM, with_mlp=False)

    wq = DN_HEADS * DN_HEAD_DIM
    o_a = _dn_call(qkv_x.reshape(bsz, t, 3 * wq), qkv_c.reshape(bsz, tc, 3 * wq),
                   ab_x.reshape(bsz, t, -1), ab_c.reshape(bsz, tc, -1), gate_x.reshape(bsz, t, wq),
                   conv_w[0], _chain_rows(a_log[0], DN_HG), _chain_rows(dt_bias[0], DN_HG), head_norm_g,
                   hg=DN_HG, unroll=DN_UNROLL)

    (out,) = _ffn_call(x1, m3, g2, w_ffn2, final_g[None, :], (o_a.reshape(bsz * t, wq), o_b, w_out[0].astype(BF16)),
                       mode="ffn2", mod_base=6, rows_per_mod=t, mod_offset=0, tm=FFN_TM, tf=FFN_TF)
    return out.reshape(bsz, t, d)
```

```python
import functools

import jax
import jax.numpy as jnp
import numpy as np
from jax import lax
from jax.experimental import pallas as pl
from jax.experimental.pallas import tpu as pltpu

F32 = jnp.float32
BF16 = jnp.bfloat16

EPS = 1e-6
LANE = 128
N_MOD = 9
DN_HEADS = 8
DN_HEAD_DIM = 128
DN_CHUNK = 64
CONV_K = 5
MLP_GROUPS = 8
MLP_CHUNK = 128
PROJ_CB = 1024
V7X_VMEM_LIMIT = 60000 * 1024
ROW_CHUNK = 16
ROW_UNROLL = 8


def _cparams(sem):
    return pltpu.CompilerParams(dimension_semantics=sem, vmem_limit_bytes=V7X_VMEM_LIMIT)


def _rms(x, g):
    return x * lax.rsqrt(jnp.mean(x * x, axis=-1, keepdims=True) + EPS) * g


def _silu(x):
    return x * jax.nn.sigmoid(x)


def _dot(a, b):
    return jnp.dot(a, b, preferred_element_type=F32)


def _dot_nt(a, b):
    return lax.dot_general(a, b, (((1,), (1,)), ((), ())), preferred_element_type=F32)


def _mod_kernel(c_ref, w_ref, b_ref, o_ref):
    s = _silu(c_ref[...]).astype(BF16)
    o_ref[...] = _dot(s, w_ref[...].astype(BF16)) + b_ref[...]


def _mod_call(c_all, w_mod, b_mod, tn=1024):
    rows, d = c_all.shape
    n = w_mod.shape[1]
    return pl.pallas_call(
        _mod_kernel,
        grid=(n // tn,),
        in_specs=[pl.BlockSpec((rows, d), lambda j: (0, 0)),
                  pl.BlockSpec((d, tn), lambda j: (0, j)),
                  pl.BlockSpec((1, tn), lambda j: (0, j))],
        out_specs=pl.BlockSpec((rows, tn), lambda j: (0, j)),
        out_shape=jax.ShapeDtypeStruct((rows, n), F32),
        compiler_params=_cparams(("arbitrary",)),
        name="mod",
    )(c_all, w_mod, b_mod)


def _ffn_kernel(*refs, mode, mod_base, nf):
    if mode == "ffn2":
        x_ref, m_ref, g_ref, wa_ref, wb_ref, wo_ref, g2_ref, out_ref, h_s, acc_s, vec_s = refs
    elif mode == "ffn1":
        x_ref, m_ref, g_ref, wa_ref, wb_ref, wo_ref, g2_ref, out_ref, h2_ref, h_s, acc_s, vec_s = refs
    else:
        x_ref, m_ref, g_ref, wa_ref, wb_ref, wo_ref, g2_ref, h2_ref, h_s, acc_s, vec_s = refs
    j = pl.program_id(1)
    tm, d = h_s.shape

    def mvec(k):
        return m_ref[0, k:k + 1, :]

    def tile_rows(v):
        return jnp.concatenate([v] * (ROW_CHUNK // 8), axis=0)

    def norm_scale(x, gain, shift=None):
        y = x * lax.rsqrt(jnp.mean(x * x, axis=-1, keepdims=True) + EPS) * tile_rows(gain)
        return y if shift is None else y + tile_rows(shift)

    def row_loop(fn):
        def body(r, carry):
            fn(pl.ds(pl.multiple_of(r * ROW_CHUNK, ROW_CHUNK), ROW_CHUNK))
            return carry
        lax.fori_loop(0, tm // ROW_CHUNK, body, 0, unroll=ROW_UNROLL)

    @pl.when(j == 0)
    def _():
        def put(k, v):
            vec_s[k] = jnp.broadcast_to(v, (8, d))
        put(0, g_ref[...] * (1.0 + mvec(mod_base + 1)))
        put(1, mvec(mod_base))
        put(2, 0.5 * mvec(mod_base + 2))
        if mode == "ffn2":
            put(3, g2_ref[...])
        else:
            put(3, g2_ref[...] * (1.0 + mvec(mod_base + 4)))
            put(4, mvec(mod_base + 3))

        def prologue(rows):
            h_s[rows, :] = norm_scale(x_ref[rows, :], vec_s[0], vec_s[1]).astype(BF16)
            acc_s[rows, :] = jnp.zeros((ROW_CHUNK, d), F32)

        row_loop(prologue)

    h = h_s[...]
    act = (_silu(_dot(h, wa_ref[...])) * _dot(h, wb_ref[...])).astype(BF16)
    acc_s[...] += _dot(act, wo_ref[...])

    @pl.when(j == nf - 1)
    def _():
        def epilogue(rows):
            xn = x_ref[rows, :] + tile_rows(vec_s[2]) * acc_s[rows, :]
            if mode == "ffn2":
                out_ref[rows, :] = norm_scale(xn, vec_s[3])
            else:
                if mode == "ffn1":
                    out_ref[rows, :] = xn
                h2_ref[rows, :] = norm_scale(xn, vec_s[3], vec_s[4]).astype(BF16)

        row_loop(epilogue)


def _split_cast_kernel(a_ref, b0_ref, b1_ref, wa_ref, wb_ref, *, f, tb):
    j = pl.program_id(0)
    rem = f % tb
    col = j * tb + lax.broadcasted_iota(jnp.int32, a_ref.shape, 1)
    keep = col < f
    b = jnp.concatenate([b0_ref[:, rem:], b1_ref[:, :rem]], axis=1) if rem else b0_ref[...]
    wa_ref[...] = jnp.where(keep, a_ref[...], 0.0).astype(BF16)
    wb_ref[...] = jnp.where(keep, b, 0.0).astype(BF16)


def _split_cast_call(w_in, f_pad, tb=256):
    d, f2 = w_in.shape
    f = f2 // 2
    assert f2 % tb == 0 and f_pad % tb == 0 and (f % tb) % LANE == 0
    last = f2 // tb - 1
    blk = pl.BlockSpec((d, tb), lambda j: (0, j))
    return pl.pallas_call(
        functools.partial(_split_cast_kernel, f=f, tb=tb),
        grid=(f_pad // tb,),
        in_specs=[blk,
                  pl.BlockSpec((d, tb), lambda j: (0, jnp.minimum(f // tb + j, last))),
                  pl.BlockSpec((d, tb), lambda j: (0, jnp.minimum(f // tb + j + 1, last)))],
        out_specs=[blk, blk],
        out_shape=[jax.ShapeDtypeStruct((d, f_pad), BF16)] * 2,
        compiler_params=_cparams(("parallel",)),
        name="ffn_weight_cast",
    )(w_in, w_in, w_in)


def _prep_ffn_weights(w_in, w_out, tf):
    f = w_in.shape[1] // 2
    pad = -f % tf
    wa, wb = _split_cast_call(w_in, f + pad)
    wo = jnp.pad(w_out.astype(BF16), ((0, pad), (0, 0)))
    return wa, wb, wo


def _ffn_call(x2d, m3, g, w3, g2, *, mode, mod_base, rows_per_mod, mod_offset, tm, tf):
    r, d = x2d.shape
    wa, wb, wo = w3
    nf = wo.shape[0] // tf
    tiles_per_mod = rows_per_mod // tm
    kern = functools.partial(_ffn_kernel, mode=mode, mod_base=mod_base, nf=nf)
    x_spec = pl.BlockSpec((tm, d), lambda i, j: (i, 0))
    in_specs = [x_spec,
                pl.BlockSpec((1, N_MOD, d), lambda i, j: (mod_offset + i // tiles_per_mod, 0, 0)),
                pl.BlockSpec((1, d), lambda i, j: (0, 0)),
                pl.BlockSpec((d, tf), lambda i, j: (0, j)),
                pl.BlockSpec((d, tf), lambda i, j: (0, j)),
                pl.BlockSpec((tf, d), lambda i, j: (j, 0)),
                pl.BlockSpec((1, d), lambda i, j: (0, 0))]
    args = [x2d, m3, g, wa, wb, wo, g2]
    o_f32 = (jax.ShapeDtypeStruct((r, d), F32), x_spec)
    o_bf = (jax.ShapeDtypeStruct((r, d), BF16), x_spec)
    outs = {"ffn2": [o_f32], "ffn1": [o_f32, o_bf], "ffn1ctx": [o_bf]}[mode]
    return pl.pallas_call(
        kern,
        grid=(r // tm, nf),
        in_specs=in_specs,
        out_specs=[o[1] for o in outs],
        out_shape=[o[0] for o in outs],
        scratch_shapes=[pltpu.VMEM((tm, d), BF16), pltpu.VMEM((tm, d), F32), pltpu.VMEM((5, 8, d), F32)],
        compiler_params=_cparams(("parallel", "arbitrary")),
        name=mode,
    )(*args)


def _inproj_kernel(*refs, tm, with_mlp):
    if with_mlp:
        (h_ref, w_ref, wab_ref, sw_ref, sb_ref, mng_ref,
         qkv_ref, ab_ref, gate_ref, ob_ref) = refs
    else:
        h_ref, w_ref, wab_ref, qkv_ref, ab_ref = refs
    cb = PROJ_CB
    h = h_ref[...]

    def block(jb):
        return _dot(h, w_ref[:, jb * cb:(jb + 1) * cb])

    ab_ref[...] = _dot(h, wab_ref[...])
    for jb in range(3):
        qkv_ref[:, jb * cb:(jb + 1) * cb] = block(jb).astype(BF16)
    if not with_mlp:
        return
    gate_ref[...] = block(3).astype(BF16)
    u = jax.nn.gelu(block(4))
    v = jax.nn.gelu(block(5))
    for g in range(MLP_GROUPS):
        cols = slice(g * LANE, (g + 1) * LANE)
        vn = _rms(v[:, cols], mng_ref[:, cols]).astype(BF16)
        for ci in range(tm // MLP_CHUNK):
            rows = slice(ci * MLP_CHUNK, (ci + 1) * MLP_CHUNK)
            s = _dot(sw_ref[g], vn[rows, :]) + sb_ref[g]
            ob_ref[rows, cols] = (u[rows, cols] * s).astype(BF16)


def _prep_inproj_weights(w_in, hg):
    wq = DN_HEADS * DN_HEAD_DIM
    n_ab = 4 * DN_HEADS
    w = w_in.astype(BF16)
    w_main = jnp.concatenate([w[:, :4 * wq], w[:, 4 * wq + n_ab:]], axis=1)
    wa = w[:, 4 * wq:4 * wq + 2 * DN_HEADS].reshape(-1, 2, DN_HEADS // hg, hg)
    wb = w[:, 4 * wq + 2 * DN_HEADS:4 * wq + n_ab].reshape(-1, 2, DN_HEADS // hg, hg)
    grp = jnp.concatenate([jnp.moveaxis(wa, 2, 1), jnp.moveaxis(wb, 2, 1)], axis=2)
    grp = grp.reshape(w.shape[0], DN_HEADS // hg, 4 * hg)
    w_ab = jnp.pad(grp, ((0, 0), (0, 0), (0, LANE - 4 * hg))).reshape(w.shape[0], -1)
    return w_main, w_ab


def _inproj_call(h2, w_main, w_ab, sw, sb, mng, *, tm, with_mlp):
    n, d = h2.shape
    cb = PROJ_CB
    nab = w_ab.shape[1]
    ncb = 6 if with_mlp else 3
    kern = functools.partial(_inproj_kernel, tm=tm, with_mlp=with_mlp)
    once = dict(pipeline_mode=pl.Buffered(1))
    in_specs = [pl.BlockSpec((tm, d), lambda i: (i, 0)),
                pl.BlockSpec((d, ncb * cb), lambda i: (0, 0), **once),
                pl.BlockSpec((d, nab), lambda i: (0, 0), **once)]
    out_specs = [pl.BlockSpec((tm, 3 * cb), lambda i: (i, 0)),
                 pl.BlockSpec((tm, nab), lambda i: (i, 0))]
    out_shape = [jax.ShapeDtypeStruct((n, 3 * cb), BF16), jax.ShapeDtypeStruct((n, nab), F32)]
    args = [h2, w_main, w_ab]
    if with_mlp:
        in_specs += [pl.BlockSpec((MLP_GROUPS, MLP_CHUNK, MLP_CHUNK), lambda i: (0, 0, 0), **once),
                     pl.BlockSpec((MLP_GROUPS, MLP_CHUNK, LANE), lambda i: (0, 0, 0), **once),
                     pl.BlockSpec((1, cb), lambda i: (0, 0))]
        out_specs += [pl.BlockSpec((tm, cb), lambda i: (i, 0))] * 2
        out_shape += [jax.ShapeDtypeStruct((n, cb), BF16)] * 2
        args += [sw, sb, mng]
    return pl.pallas_call(
        kern,
        grid=(n // tm,),
        in_specs=in_specs,
        out_specs=out_specs,
        out_shape=out_shape,
        compiler_params=_cparams(("parallel",)),
        name="inproj" if with_mlp else "inproj_ctx",
    )(*args)


def _tri_inverse(lmats, uppers, ii, jj):
    eye = (ii == jj).astype(F32)
    b16 = (ii // 16) == (jj // 16)
    b32 = (ii // 32) == (jj // 32)
    m1 = {False: b32 & ((ii // 16) > (jj // 16)), True: b32 & ((ii // 16) < (jj // 16))}
    m2 = {False: (ii // 32) > (jj // 32), True: (ii // 32) < (jj // 32)}
    a1 = [jnp.where(b16, l, 0.0) for l in lmats]
    a1b = [a.astype(BF16) for a in a1]
    a2b = [_dot(a, a).astype(BF16) for a in a1b]
    yield
    a4b = [_dot(a, a).astype(BF16) for a in a2b]
    yield
    a8b = [_dot(a, a).astype(BF16) for a in a4b]
    yield
    p = [eye - a for a in a1]
    for ab in (a2b, a4b, a8b):
        p = [pi + _dot(pi.astype(BF16), a) for pi, a in zip(p, ab)]
        yield
    for masks in (m1, m2):
        pb = [pi.astype(BF16) for pi in p]
        xm = [_dot(jnp.where(masks[u], l, 0.0).astype(BF16), b).astype(BF16)
              for l, u, b in zip(lmats, uppers, pb)]
        yield
        p = [pi - _dot(b, x) for pi, b, x in zip(p, pb, xm)]
        yield
    return p


def _run_interleaved(*gens):
    live = list(gens)
    while live:
        for g in list(live):
            try:
                next(g)
            except StopIteration:
                live.remove(g)


def _dn_kernel(qx_ref, kx_ref, vx_ref, qc_ref, kc_ref, vc_ref, abx_ref, abc_ref, gate_ref,
               cwq_ref, cwk_ref, cwv_ref, alog_ref, dtb_ref, hng_ref,
               out_ref,
               qn_s, kn_s, vv_s, col_s, row_s, st_s, o_s, pu_s, pwq_s, pkt_s, pqk_s, pgl_s,
               *, hg, tx, tc, unroll):
    C = DN_CHUNK
    D = DN_HEAD_DIM
    ncc, ncx = tc // C, tx // C
    nch = ncc + ncx
    nchain = 2 * hg

    ii = lax.broadcasted_iota(jnp.int32, (C, C), 0)
    jj = lax.broadcasted_iota(jnp.int32, (C, C), 1)
    lane_t = lax.broadcasted_iota(jnp.int32, (C, LANE), 1)

    def conv_phase(srcs, t_len, row0):
        n_ch = t_len // C

        def body(n, carry):
            base = pl.multiple_of(n * C, C)
            pbase = pl.multiple_of(jnp.maximum(base - 16, 0), 16)
            nbase = pl.multiple_of(jnp.minimum(base + C, t_len - 16), 16)
            has_prev = jnp.where(n > 0, 1.0, 0.0)
            has_next = jnp.where(n < n_ch - 1, 1.0, 0.0)
            for src, cw_ref, dst, kind in srcs:
                main = src[0, pl.ds(base, C), :].astype(F32)
                prev = src[0, pl.ds(pbase, 16), :].astype(F32) * has_prev
                nxt = src[0, pl.ds(nbase, 16), :].astype(F32) * has_next
                xcat = jnp.concatenate([prev, main, nxt], axis=0)
                y = cw_ref[0:1, :] * xcat[14:14 + C]
                for j in range(1, CONV_K):
                    y = y + cw_ref[j:j + 1, :] * xcat[14 + j:14 + j + C]
                y = _silu(y)
                if kind != "v":
                    parts = []
                    for hl in range(hg):
                        yh = y[:, hl * D:(hl + 1) * D]
                        yh = yh * lax.rsqrt(jnp.sum(yh * yh, axis=-1, keepdims=True) + EPS)
                        if kind == "q":
                            yh = yh * (D ** -0.5)
                        parts.append(yh)
                    y = jnp.concatenate(parts, axis=1) if hg > 1 else parts[0]
                dst[pl.ds(row0 + base, C), :] = y.astype(BF16)
            return carry

        lax.fori_loop(0, n_ch, body, 0)

    conv_phase(((qc_ref, cwq_ref, qn_s, "q"), (kc_ref, cwk_ref, kn_s, "k"), (vc_ref, cwv_ref, vv_s, "v")), tc, 0)
    conv_phase(((qx_ref, cwq_ref, qn_s, "q"), (kx_ref, cwk_ref, kn_s, "k"), (vx_ref, cwv_ref, vv_s, "v")), tx, tc)

    tril = (ii >= jj).astype(F32)
    triu = (ii <= jj).astype(F32)
    ones = jnp.ones((C, C), F32)
    sum_mats = jnp.concatenate([tril, triu, ones], axis=0).astype(BF16)
    neg_a = -jnp.exp(alog_ref[0])
    dtb = dtb_ref[0]

    def gate_phase(ab_ref, n_ch, chunk0):
        def body(n, carry):
            t = ab_ref[0, pl.ds(pl.multiple_of(n * C, C), C), :]
            g = neg_a * jax.nn.softplus(t + dtb)
            g = jnp.where(lane_t < nchain, g, 0.0)
            hi = g.astype(BF16)
            r1 = g - hi.astype(F32)
            mid = r1.astype(BF16)
            lo = (r1 - mid.astype(F32)).astype(BF16)
            cs = _dot(sum_mats, jnp.concatenate([hi, mid, lo], axis=1))
            cs = cs[:, :LANE] + cs[:, LANE:2 * LANE] + cs[:, 2 * LANE:]
            gt = cs[2 * C:]
            gc = jnp.where(lane_t < hg, cs[:C], cs[C:2 * C])
            beta = pltpu.roll(jax.nn.sigmoid(t), LANE - nchain, 1)
            col_s[chunk0 + n, 0] = gc
            col_s[chunk0 + n, 1] = beta
            col_s[chunk0 + n, 2] = gt
            row_s[chunk0 + n] = gc.T[0:16, :]
            return carry

        lax.fori_loop(0, n_ch, body, 0, unroll=4)

    gate_phase(abc_ref, ncc, 0)
    gate_phase(abx_ref, ncx, ncc)

    def chunk_of(s):
        cf = s
        cb = jnp.where(s < ncc, ncc - 1 - s, nch + ncc - 1 - s)
        return cf, cb

    def bcast(tile, r):
        return jnp.broadcast_to(tile[:, r:r + 1], (C, LANE))

    def stage1(it, par):
        jobs = []
        for u in range(unroll):
            cf, cb = chunk_of(it * unroll + u)
            for d in range(2):
                c = cf if d == 0 else cb
                gc_t = col_s[c, 0]
                be_t = col_s[c, 1]
                gt_t = col_s[c, 2]
                e1_t = jnp.exp(gc_t)
                tiles = dict(gc=gc_t, be=be_t, e1=e1_t, ca=be_t * e1_t, e2=jnp.exp(gt_t - gc_t),
                             gl=jnp.exp(gt_t), rows=row_s[c])
                for hl in range(hg):
                    jobs.append((par * unroll + u, d * hg + hl, pl.multiple_of(c * C, C), tiles, d, hl))
        tri = {0: ii >= jj, 1: ii <= jj}
        strict = {0: ii > jj, 1: ii < jj}

        def rd(ref, job):
            return ref[pl.ds(job[2], C), job[5] * D:(job[5] + 1) * D]

        qn = [rd(qn_s, j) for j in jobs]
        kn = [rd(kn_s, j) for j in jobs]
        gmat = [_dot_nt(jnp.concatenate([q, k], axis=0), k) for q, k in zip(qn, kn)]
        yield
        dec =[jnp.exp(jnp.where(tri[j[4]], bcast(j[3]["gc"], j[1])[:, :C] - j[3]["rows"][j[1]:j[1] + 1, :], -1e30))
               for j in jobs]
        lmat = [jnp.where(strict[j[4]], g[C:] * dc, 0.0) * bcast(j[3]["be"], j[1])[:, :C]
                for j, g, dc in zip(jobs, gmat, dec)]
        for j, g, dc in zip(jobs, gmat, dec):
            pqk_s[j[0], j[1]] = (g[:C] * dc).astype(BF16)
        tinv = yield from _tri_inverse(lmat, [bool(j[4]) for j in jobs], ii, jj)
        knf = [k.astype(F32) for k in kn]
        rhs = [jnp.concatenate([(rd(vv_s, j).astype(F32) * bcast(j[3]["be"], j[1])).astype(BF16),
                                (kf * bcast(j[3]["ca"], j[1])).astype(BF16)], axis=1)
               for j, kf in zip(jobs, knf)]
        uw = [_dot(t.astype(BF16), r) for t, r in zip(tinv, rhs)]
        yield
        for j, x, q, kf in zip(jobs, uw, qn, knf):
            slot, r = j[0], j[1]
            qd = (q.astype(F32) * bcast(j[3]["e1"], r)).astype(BF16)
            pu_s[slot, r] = x[:, :D].astype(BF16)
            pwq_s[slot, r] = jnp.concatenate([x[:, D:].astype(BF16), qd], axis=0)
            pkt_s[slot, r] = (kf * bcast(j[3]["e2"], r)).T.astype(BF16)
            pgl_s[slot, r] = jnp.broadcast_to(j[3]["gl"][0:8, r:r + 1], (8, LANE))

    def stage2(it, par):
        for u in range(unroll):
            slot = par * unroll + u
            cf, cb = chunk_of(it * unroll + u)
            chains = [(d * hg + hl, pl.multiple_of((cb if d else cf) * C, C), hl)
                      for d in range(2) for hl in range(hg)]
            st = [st_s[r] for r, _, _ in chains]
            ws = [_dot(pwq_s[slot, r], s.astype(BF16)) for (r, _, _), s in zip(chains, st)]
            yield
            vn = [(pu_s[slot, r].astype(F32) - w[:C]).astype(BF16) for (r, _, _), w in zip(chains, ws)]
            o = [w[C:] + _dot(pqk_s[slot, r], v) for (r, _, _), w, v in zip(chains, ws, vn)]
            kv = [_dot(pkt_s[slot, r], v) for (r, _, _), v in zip(chains, vn)]
            yield
            for (r, row0, hl), s, x, y in zip(chains, st, kv, o):
                st_s[r] = s * jnp.concatenate([pgl_s[slot, r]] * (D // 8), axis=0) + x
                o_s[pl.ds(row0, C), hl * D:(hl + 1) * D] += y

    st_s[...] = jnp.zeros_like(st_s)
    o_s[...] = jnp.zeros_like(o_s)
    _run_interleaved(stage1(0, 0))

    def loop_body(it, carry):
        _run_interleaved(stage1(it, it % 2), stage2(it - 1, (it - 1) % 2))
        return carry

    n_it = nch // unroll
    lax.fori_loop(1, n_it, loop_body, 0)
    _run_interleaved(stage2(n_it - 1, (n_it - 1) % 2))

    hn = hng_ref[...]
    for hl in range(hg):
        cols = slice(hl * D, (hl + 1) * D)
        o = o_s[tc:tc + tx, cols]
        zg = gate_ref[0, :, cols].astype(F32)
        out_ref[0, :, cols] = (_rms(o, hn) * _silu(zg)).astype(BF16)


def _dn_call(qkv_x, qkv_c, ab_x, ab_c, gate, conv_w, alog_g, dtb_g, hng, *, hg, unroll):
    b, tx, w3 = qkv_x.shape
    tc = qkv_c.shape[1]
    assert (tc // DN_CHUNK) % unroll == 0 and (tx // DN_CHUNK) % unroll == 0
    nslot = 2 * unroll
    nhg = DN_HEADS // hg
    wd = hg * DN_HEAD_DIM
    nch = (tx + tc) // DN_CHUNK
    nchain = 2 * hg
    kern = functools.partial(_dn_kernel, hg=hg, tx=tx, tc=tc, unroll=unroll)

    def col(t, off):
        return pl.BlockSpec((1, t, wd), lambda bi, gi: (bi, 0, off * nhg + gi))

    def cw(off):
        return pl.BlockSpec((CONV_K, wd), lambda bi, gi: (0, off * nhg + gi))

    def ab(t):
        return pl.BlockSpec((1, t, LANE), lambda bi, gi: (bi, 0, gi), pipeline_mode=pl.Buffered(1))

    small = pl.BlockSpec((1, 1, LANE), lambda bi, gi: (gi, 0, 0))
    return pl.pallas_call(
        kern,
        grid=(b, nhg),
        in_specs=[col(tx, 0), col(tx, 1), col(tx, 2), col(tc, 0), col(tc, 1), col(tc, 2),
                  ab(tx), ab(tc),
                  pl.BlockSpec((1, tx, wd), lambda bi, gi: (bi, 0, gi), pipeline_mode=pl.Buffered(1)),
                  cw(0), cw(1), cw(2), small, small,
                  pl.BlockSpec((1, LANE), lambda bi, gi: (0, 0))],
        out_specs=pl.BlockSpec((1, tx, wd), lambda bi, gi: (bi, 0, gi)),
        out_shape=jax.ShapeDtypeStruct((b, tx, DN_HEADS * DN_HEAD_DIM), BF16),
        scratch_shapes=[pltpu.VMEM((tc + tx, wd), BF16)] * 3 + [
            pltpu.VMEM((nch, 3, DN_CHUNK, LANE), F32),
            pltpu.VMEM((nch, 16, DN_CHUNK), F32),
            pltpu.VMEM((nchain, DN_HEAD_DIM, DN_HEAD_DIM), F32),
            pltpu.VMEM((tc + tx, wd), F32),
            pltpu.VMEM((nslot, nchain, DN_CHUNK, DN_HEAD_DIM), BF16),
            pltpu.VMEM((nslot, nchain, 2 * DN_CHUNK, DN_HEAD_DIM), BF16),
            pltpu.VMEM((nslot, nchain, DN_HEAD_DIM, DN_CHUNK), BF16),
            pltpu.VMEM((nslot, nchain, DN_CHUNK, DN_CHUNK), BF16),
            pltpu.VMEM((nslot, nchain, 8, LANE), F32)],
        compiler_params=_cparams(("parallel", "arbitrary")),
        name="deltanet",
    )(qkv_x, qkv_x, qkv_x, qkv_c, qkv_c, qkv_c, ab_x, ab_c, gate, conv_w, conv_w, conv_w, alog_g, dtb_g, hng)


def _outproj_kernel(x_ref, oa_ref, ob_ref, w_ref, m_ref, out_ref):
    half = oa_ref.shape[1]
    y = _dot(oa_ref[...], w_ref[:half, :]) + _dot(ob_ref[...], w_ref[half:, :])
    out_ref[...] = x_ref[...] + m_ref[0, 5:6, :] * y


def _outproj_call(x2d, oa, ob, w, m3, *, rows_per_mod, tm):
    n, d = x2d.shape
    half = oa.shape[1]
    tiles_per_mod = rows_per_mod // tm
    return pl.pallas_call(
        _outproj_kernel,
        grid=(n // tm,),
        in_specs=[pl.BlockSpec((tm, d), lambda i: (i, 0)),
                  pl.BlockSpec((tm, half), lambda i: (i, 0)),
                  pl.BlockSpec((tm, half), lambda i: (i, 0)),
                  pl.BlockSpec((2 * half, d), lambda i: (0, 0), pipeline_mode=pl.Buffered(1)),
                  pl.BlockSpec((1, N_MOD, d), lambda i: (i // tiles_per_mod, 0, 0))],
        out_specs=pl.BlockSpec((tm, d), lambda i: (i, 0)),
        out_shape=jax.ShapeDtypeStruct((n, d), F32),
        compiler_params=_cparams(("parallel",)),
        name="outproj",
    )(x2d, oa, ob, w, m3)


def _chain_rows(p, hg):
    nhg = DN_HEADS // hg
    rows = jnp.moveaxis(p.reshape(2, nhg, hg), 1, 0).reshape(nhg, 1, 2 * hg)
    return jnp.pad(rows, ((0, 0), (0, 0), (0, LANE - 2 * hg)))


FFN_TM = 512
FFN_TF = 512
PROJ_TM = 512
OUT_TM = 512
DN_HG = 4
DN_UNROLL = 4
MOD_ROWS = 16


def kernel(x, c, ctx, c_ctx, w_mod, b_mod, norm_g, ffn1_w_in, ffn1_w_out, w_in, conv_w, a_log, dt_bias, head_norm_g, spatial_w, spatial_b, mlp_norm_g, w_out, ffn2_w_in, ffn2_w_out, final_g):
    bsz, t, d = x.shape
    tc = ctx.shape[1]
    assert w_mod.shape[0] == 1, "single-layer block"
    assert bsz + 1 <= MOD_ROWS

    c_all = jnp.concatenate([c, c_ctx[None, :], jnp.zeros((MOD_ROWS - bsz - 1, d), F32)], axis=0)
    m3 = _mod_call(c_all, w_mod[0], b_mod).reshape(MOD_ROWS, N_MOD, d)

    g0, g1, g2 = norm_g[0, 0:1], norm_g[0, 1:2], norm_g[0, 2:3]
    w_ffn1 = _prep_ffn_weights(ffn1_w_in[0], ffn1_w_out[0], FFN_TF)
    w_ffn2 = _prep_ffn_weights(ffn2_w_in[0], ffn2_w_out[0], FFN_TF)

    x1, h_x = _ffn_call(x.reshape(bsz * t, d), m3, g0, w_ffn1, g1, mode="ffn1", mod_base=0,
                        rows_per_mod=t, mod_offset=0, tm=FFN_TM, tf=FFN_TF)
    (h_c,) = _ffn_call(ctx.reshape(bsz * tc, d), m3, g0, w_ffn1, g1, mode="ffn1ctx", mod_base=0,
                       rows_per_mod=bsz * tc, mod_offset=bsz, tm=FFN_TM, tf=FFN_TF)

    w_main, w_ab = _prep_inproj_weights(w_in[0], DN_HG)
    sb = jnp.broadcast_to(spatial_b[0][:, :, None], (MLP_GROUPS, MLP_CHUNK, LANE))
    qkv_x, ab_x, gate_x, o_b = _inproj_call(h_x, w_main, w_ab, spatial_w[0].astype(BF16), sb, mlp_norm_g,
                                            tm=PROJ_TM, with_mlp=True)
    qkv_c, ab_c = _inproj_call(h_c, w_main, w_ab, None, None, None, tm=PROJ_TM, with_mlp=False)

    wq = DN_HEADS * DN_HEAD_DIM
    o_a = _dn_call(qkv_x.reshape(bsz, t, 3 * wq), qkv_c.reshape(bsz, tc, 3 * wq),
                   ab_x.reshape(bsz, t, -1), ab_c.reshape(bsz, tc, -1), gate_x.reshape(bsz, t, wq),
                   conv_w[0], _chain_rows(a_log[0], DN_HG), _chain_rows(dt_bias[0], DN_HG), head_norm_g,
                   hg=DN_HG, unroll=DN_UNROLL)

    x2 = _outproj_call(x1, o_a.reshape(bsz * t, wq), o_b, w_out[0].astype(BF16), m3, rows_per_mod=t, tm=OUT_TM)
    (out,) = _ffn_call(x2, m3, g2, w_ffn2, final_g[None, :], mode="ffn2", mod_base=6,
                       rows_per_mod=t, mod_offset=0, tm=FFN_TM, tf=FFN_TF)
    return out.reshape(bsz, t, d)
```

```python
import functools

import jax
import jax.numpy as jnp
import numpy as np
from jax import lax
from jax.experimental import pallas as pl
from jax.experimental.pallas import tpu as pltpu

F32 = jnp.float32
BF16 = jnp.bfloat16

EPS = 1e-6
LANE = 128
N_MOD = 9
DN_HEADS = 8
DN_HEAD_DIM = 128
DN_CHUNK = 64
CONV_K = 5
MLP_GROUPS = 8
MLP_CHUNK = 128
PROJ_CB = 1024
HALO = 16
V7X_VMEM_LIMIT = 60000 * 1024
ROW_CHUNK = 16
ROW_UNROLL = 8


def _cparams(sem):
    return pltpu.CompilerParams(dimension_semantics=sem, vmem_limit_bytes=V7X_VMEM_LIMIT)


def _rms(x, g):
    return x * lax.rsqrt(jnp.mean(x * x, axis=-1, keepdims=True) + EPS) * g


def _silu(x):
    return x * jax.nn.sigmoid(x)


def _dot(a, b):
    return jnp.dot(a, b, preferred_element_type=F32)


def _dot_nt(a, b):
    return lax.dot_general(a, b, (((1,), (1,)), ((), ())), preferred_element_type=F32)


def _mod_kernel(c_ref, w_ref, b_ref, o_ref):
    s = _silu(c_ref[...]).astype(BF16)
    o_ref[...] = _dot(s, w_ref[...].astype(BF16)) + b_ref[...]


def _mod_call(c_all, w_mod, b_mod, tn=1024):
    rows, d = c_all.shape
    n = w_mod.shape[1]
    return pl.pallas_call(
        _mod_kernel,
        grid=(n // tn,),
        in_specs=[pl.BlockSpec((rows, d), lambda j: (0, 0)),
                  pl.BlockSpec((d, tn), lambda j: (0, j)),
                  pl.BlockSpec((1, tn), lambda j: (0, j))],
        out_specs=pl.BlockSpec((rows, tn), lambda j: (0, j)),
        out_shape=jax.ShapeDtypeStruct((rows, n), F32),
        compiler_params=_cparams(("arbitrary",)),
        name="mod",
    )(c_all, w_mod, b_mod)


def _ffn_kernel(*refs, mode, mod_base, nf):
    if mode == "ffn2":
        x_ref, m_ref, g_ref, wa_ref, wb_ref, wo_ref, g2_ref, out_ref, h_s, acc_s, vec_s = refs
    elif mode == "ffn1":
        x_ref, m_ref, g_ref, wa_ref, wb_ref, wo_ref, g2_ref, out_ref, h2_ref, h_s, acc_s, vec_s = refs
    else:
        x_ref, m_ref, g_ref, wa_ref, wb_ref, wo_ref, g2_ref, h2_ref, h_s, acc_s, vec_s = refs
    j = pl.program_id(1)
    tm, d = h_s.shape

    def mvec(k):
        return m_ref[0, k:k + 1, :]

    def tile_rows(v):
        return jnp.concatenate([v] * (ROW_CHUNK // 8), axis=0)

    def norm_scale(x, gain, shift=None):
        y = x * lax.rsqrt(jnp.mean(x * x, axis=-1, keepdims=True) + EPS) * tile_rows(gain)
        return y if shift is None else y + tile_rows(shift)

    def row_loop(fn):
        def body(r, carry):
            fn(pl.ds(pl.multiple_of(r * ROW_CHUNK, ROW_CHUNK), ROW_CHUNK))
            return carry
        lax.fori_loop(0, tm // ROW_CHUNK, body, 0, unroll=ROW_UNROLL)

    @pl.when(j == 0)
    def _():
        def put(k, v):
            vec_s[k] = jnp.broadcast_to(v, (8, d))
        put(0, g_ref[...] * (1.0 + mvec(mod_base + 1)))
        put(1, mvec(mod_base))
        put(2, 0.5 * mvec(mod_base + 2))
        if mode == "ffn2":
            put(3, g2_ref[...])
        else:
            put(3, g2_ref[...] * (1.0 + mvec(mod_base + 4)))
            put(4, mvec(mod_base + 3))

        def prologue(rows):
            h_s[rows, :] = norm_scale(x_ref[rows, :], vec_s[0], vec_s[1]).astype(BF16)
            acc_s[rows, :] = jnp.zeros((ROW_CHUNK, d), F32)

        row_loop(prologue)

    h = h_s[...]
    act = (_silu(_dot(h, wa_ref[...])) * _dot(h, wb_ref[...])).astype(BF16)
    acc_s[...] += _dot(act, wo_ref[...])

    @pl.when(j == nf - 1)
    def _():
        def epilogue(rows):
            xn = x_ref[rows, :] + tile_rows(vec_s[2]) * acc_s[rows, :]
            if mode == "ffn2":
                out_ref[rows, :] = norm_scale(xn, vec_s[3])
            else:
                if mode == "ffn1":
                    out_ref[rows, :] = xn
                h2_ref[rows, :] = norm_scale(xn, vec_s[3], vec_s[4]).astype(BF16)

        row_loop(epilogue)


def _split_cast_kernel(a_ref, b0_ref, b1_ref, wa_ref, wb_ref, *, f, tb):
    j = pl.program_id(0)
    rem = f % tb
    col = j * tb + lax.broadcasted_iota(jnp.int32, a_ref.shape, 1)
    keep = col < f
    b = jnp.concatenate([b0_ref[:, rem:], b1_ref[:, :rem]], axis=1) if rem else b0_ref[...]
    wa_ref[...] = jnp.where(keep, a_ref[...], 0.0).astype(BF16)
    wb_ref[...] = jnp.where(keep, b, 0.0).astype(BF16)


def _split_cast_call(w_in, f_pad, tb=256):
    d, f2 = w_in.shape
    f = f2 // 2
    assert f2 % tb == 0 and f_pad % tb == 0 and (f % tb) % LANE == 0
    last = f2 // tb - 1
    blk = pl.BlockSpec((d, tb), lambda j: (0, j))
    return pl.pallas_call(
        functools.partial(_split_cast_kernel, f=f, tb=tb),
        grid=(f_pad // tb,),
        in_specs=[blk,
                  pl.BlockSpec((d, tb), lambda j: (0, jnp.minimum(f // tb + j, last))),
                  pl.BlockSpec((d, tb), lambda j: (0, jnp.minimum(f // tb + j + 1, last)))],
        out_specs=[blk, blk],
        out_shape=[jax.ShapeDtypeStruct((d, f_pad), BF16)] * 2,
        compiler_params=_cparams(("parallel",)),
        name="ffn_weight_cast",
    )(w_in, w_in, w_in)


def _prep_ffn_weights(w_in, w_out, tf):
    f = w_in.shape[1] // 2
    pad = -f % tf
    wa, wb = _split_cast_call(w_in, f + pad)
    wo = jnp.pad(w_out.astype(BF16), ((0, pad), (0, 0)))
    return wa, wb, wo


def _ffn_call(x2d, m3, g, w3, g2, *, mode, mod_base, rows_per_mod, mod_offset, tm, tf):
    r, d = x2d.shape
    wa, wb, wo = w3
    nf = wo.shape[0] // tf
    tiles_per_mod = rows_per_mod // tm
    kern = functools.partial(_ffn_kernel, mode=mode, mod_base=mod_base, nf=nf)
    x_spec = pl.BlockSpec((tm, d), lambda i, j: (i, 0))
    in_specs = [x_spec,
                pl.BlockSpec((1, N_MOD, d), lambda i, j: (mod_offset + i // tiles_per_mod, 0, 0)),
                pl.BlockSpec((1, d), lambda i, j: (0, 0)),
                pl.BlockSpec((d, tf), lambda i, j: (0, j)),
                pl.BlockSpec((d, tf), lambda i, j: (0, j)),
                pl.BlockSpec((tf, d), lambda i, j: (j, 0)),
                pl.BlockSpec((1, d), lambda i, j: (0, 0))]
    args = [x2d, m3, g, wa, wb, wo, g2]
    o_f32 = (jax.ShapeDtypeStruct((r, d), F32), x_spec)
    o_bf = (jax.ShapeDtypeStruct((r, d), BF16), x_spec)
    outs = {"ffn2": [o_f32], "ffn1": [o_f32, o_bf], "ffn1ctx": [o_bf]}[mode]
    return pl.pallas_call(
        kern,
        grid=(r // tm, nf),
        in_specs=in_specs,
        out_specs=[o[1] for o in outs],
        out_shape=[o[0] for o in outs],
        scratch_shapes=[pltpu.VMEM((tm, d), BF16), pltpu.VMEM((tm, d), F32), pltpu.VMEM((5, 8, d), F32)],
        compiler_params=_cparams(("parallel", "arbitrary")),
        name=mode,
    )(*args)


def _inproj_kernel(*refs, tm, seq_tiles, with_mlp):
    if with_mlp:
        (h_ref, hp_ref, hn_ref, w_ref, wab_ref, cw_ref, sw_ref, sb_ref, mng_ref,
         qkv_ref, ab_ref, gate_ref, ob_ref) = refs
    else:
        h_ref, hp_ref, hn_ref, w_ref, wab_ref, cw_ref, qkv_ref, ab_ref = refs
    cb = PROJ_CB
    h = h_ref[...]

    def block(jb):
        return _dot(h, w_ref[:, jb * cb:(jb + 1) * cb])

    ab_ref[...] = _dot(h, wab_ref[...])

    i = pl.program_id(0)
    pos = i % seq_tiles
    keep_prev = jnp.where(pos > 0, 1.0, 0.0).astype(BF16)
    keep_next = jnp.where(pos < seq_tiles - 1, 1.0, 0.0).astype(BF16)
    hext = jnp.concatenate([hp_ref[...] * keep_prev, h, hn_ref[...] * keep_next], axis=0)
    first = HALO - CONV_K // 2
    for jb in range(3):
        zext = _dot(hext, w_ref[:, jb * cb:(jb + 1) * cb])
        for rc in range(tm // DN_CHUNK):
            win = zext[rc * DN_CHUNK:rc * DN_CHUNK + DN_CHUNK + 2 * HALO]
            y = cw_ref[0:1, jb * cb:(jb + 1) * cb] * win[first:first + DN_CHUNK]
            for t in range(1, CONV_K):
                y = y + cw_ref[t:t + 1, jb * cb:(jb + 1) * cb] * win[first + t:first + t + DN_CHUNK]
            y = _silu(y)
            if jb < 2:
                parts = []
                for hd in range(cb // DN_HEAD_DIM):
                    yh = y[:, hd * DN_HEAD_DIM:(hd + 1) * DN_HEAD_DIM]
                    yh = yh * lax.rsqrt(jnp.sum(yh * yh, axis=-1, keepdims=True) + EPS)
                    parts.append(yh * (DN_HEAD_DIM ** -0.5) if jb == 0 else yh)
                y = jnp.concatenate(parts, axis=1)
            qkv_ref[rc * DN_CHUNK:(rc + 1) * DN_CHUNK, jb * cb:(jb + 1) * cb] = y.astype(BF16)
    if not with_mlp:
        return
    gate_ref[...] = block(3).astype(BF16)
    u = jax.nn.gelu(block(4))
    v = jax.nn.gelu(block(5))
    for g in range(MLP_GROUPS):
        cols = slice(g * LANE, (g + 1) * LANE)
        vn = _rms(v[:, cols], mng_ref[:, cols]).astype(BF16)
        for ci in range(tm // MLP_CHUNK):
            rows = slice(ci * MLP_CHUNK, (ci + 1) * MLP_CHUNK)
            s = _dot(sw_ref[g], vn[rows, :]) + sb_ref[g]
            ob_ref[rows, cols] = (u[rows, cols] * s).astype(BF16)


def _prep_inproj_weights(w_in, hg):
    wq = DN_HEADS * DN_HEAD_DIM
    n_ab = 4 * DN_HEADS
    w_main = jnp.concatenate([w_in[:, :4 * wq].astype(BF16), w_in[:, 4 * wq + n_ab:].astype(BF16)], axis=1)
    w = w_in[:, 4 * wq:4 * wq + n_ab].astype(BF16)
    wa = w[:, :2 * DN_HEADS].reshape(-1, 2, DN_HEADS // hg, hg)
    wb = w[:, 2 * DN_HEADS:].reshape(-1, 2, DN_HEADS // hg, hg)
    grp = jnp.concatenate([jnp.moveaxis(wa, 2, 1), jnp.moveaxis(wb, 2, 1)], axis=2)
    grp = grp.reshape(w.shape[0], DN_HEADS // hg, 4 * hg)
    w_ab = jnp.pad(grp, ((0, 0), (0, 0), (0, LANE - 4 * hg))).reshape(w.shape[0], -1)
    return w_main, w_ab


def _inproj_call(h2, w_main, w_ab, conv_w, sw, sb, mng, *, tm, seq_len, with_mlp):
    n, d = h2.shape
    cb = PROJ_CB
    nab = w_ab.shape[1]
    ncb = 6 if with_mlp else 3
    assert seq_len % tm == 0 and tm % HALO == 0
    kern = functools.partial(_inproj_kernel, tm=tm, seq_tiles=seq_len // tm, with_mlp=with_mlp)
    once = dict(pipeline_mode=pl.Buffered(1))
    hb = tm // HALO
    in_specs = [pl.BlockSpec((tm, d), lambda i: (i, 0)),
                pl.BlockSpec((HALO, d), lambda i: (jnp.maximum(i * hb - 1, 0), 0)),
                pl.BlockSpec((HALO, d), lambda i: (jnp.minimum((i + 1) * hb, n // HALO - 1), 0)),
                pl.BlockSpec((d, ncb * cb), lambda i: (0, 0), **once),
                pl.BlockSpec((d, nab), lambda i: (0, 0), **once),
                pl.BlockSpec((CONV_K, 3 * cb), lambda i: (0, 0), **once)]
    out_specs = [pl.BlockSpec((tm, 3 * cb), lambda i: (i, 0)),
                 pl.BlockSpec((tm, nab), lambda i: (i, 0))]
    out_shape = [jax.ShapeDtypeStruct((n, 3 * cb), BF16), jax.ShapeDtypeStruct((n, nab), F32)]
    args = [h2, h2, h2, w_main, w_ab, conv_w]
    if with_mlp:
        in_specs += [pl.BlockSpec((MLP_GROUPS, MLP_CHUNK, MLP_CHUNK), lambda i: (0, 0, 0), **once),
                     pl.BlockSpec((MLP_GROUPS, MLP_CHUNK, LANE), lambda i: (0, 0, 0), **once),
                     pl.BlockSpec((1, cb), lambda i: (0, 0))]
        out_specs += [pl.BlockSpec((tm, cb), lambda i: (i, 0))] * 2
        out_shape += [jax.ShapeDtypeStruct((n, cb), BF16)] * 2
        args += [sw, sb, mng]
    return pl.pallas_call(
        kern,
        grid=(n // tm,),
        in_specs=in_specs,
        out_specs=out_specs,
        out_shape=out_shape,
        compiler_params=_cparams(("parallel",)),
        name="inproj" if with_mlp else "inproj_ctx",
    )(*args)


def _tri_inverse(lmats, uppers, ii, jj):
    eye = (ii == jj).astype(F32)
    b16 = (ii // 16) == (jj // 16)
    b32 = (ii // 32) == (jj // 32)
    m1 = {False: b32 & ((ii // 16) > (jj // 16)), True: b32 & ((ii // 16) < (jj // 16))}
    m2 = {False: (ii // 32) > (jj // 32), True: (ii // 32) < (jj // 32)}
    a1 = [jnp.where(b16, l, 0.0) for l in lmats]
    a1b = [a.astype(BF16) for a in a1]
    a2b = [_dot(a, a).astype(BF16) for a in a1b]
    yield
    a4b = [_dot(a, a).astype(BF16) for a in a2b]
    yield
    a8b = [_dot(a, a).astype(BF16) for a in a4b]
    yield
    p = [eye - a for a in a1]
    for ab in (a2b, a4b, a8b):
        p = [pi + _dot(pi.astype(BF16), a) for pi, a in zip(p, ab)]
        yield
    for masks in (m1, m2):
        pb = [pi.astype(BF16) for pi in p]
        xm = [_dot(jnp.where(masks[u], l, 0.0).astype(BF16), b).astype(BF16)
              for l, u, b in zip(lmats, uppers, pb)]
        yield
        p = [pi - _dot(b, x) for pi, b, x in zip(p, pb, xm)]
        yield
    return p


def _run_interleaved(*gens):
    live = list(gens)
    while live:
        for g in list(live):
            try:
                next(g)
            except StopIteration:
                live.remove(g)


def _dn_kernel(qx_ref, kx_ref, vx_ref, qc_ref, kc_ref, vc_ref, abx_ref, abc_ref, gate_ref,
               alog_ref, dtb_ref, hng_ref,
               out_ref,
               col_s, row_s, st_s, o_s, pu_s, pwq_s, pkt_s, pqk_s, pgl_s,
               *, hg, tx, tc, unroll):
    C = DN_CHUNK
    D = DN_HEAD_DIM
    ncc, ncx = tc // C, tx // C
    nch = ncc + ncx
    nchain = 2 * hg

    ii = lax.broadcasted_iota(jnp.int32, (C, C), 0)
    jj = lax.broadcasted_iota(jnp.int32, (C, C), 1)
    lane_t = lax.broadcasted_iota(jnp.int32, (C, LANE), 1)

    tril = (ii >= jj).astype(F32)
    triu = (ii <= jj).astype(F32)
    ones = jnp.ones((C, C), F32)
    sum_mats = jnp.concatenate([tril, triu, ones], axis=0).astype(BF16)
    neg_a = -jnp.exp(alog_ref[0])
    dtb = dtb_ref[0]

    def gate_phase(ab_ref, n_ch, chunk0):
        def body(n, carry):
            t = ab_ref[0, pl.ds(pl.multiple_of(n * C, C), C), :]
            g = neg_a * jax.nn.softplus(t + dtb)
            g = jnp.where(lane_t < nchain, g, 0.0)
            hi = g.astype(BF16)
            r1 = g - hi.astype(F32)
            mid = r1.astype(BF16)
            lo = (r1 - mid.astype(F32)).astype(BF16)
            cs = _dot(sum_mats, jnp.concatenate([hi, mid, lo], axis=1))
            cs = cs[:, :LANE] + cs[:, LANE:2 * LANE] + cs[:, 2 * LANE:]
            gt = cs[2 * C:]
            gc = jnp.where(lane_t < hg, cs[:C], cs[C:2 * C])
            beta = pltpu.roll(jax.nn.sigmoid(t), LANE - nchain, 1)
            col_s[chunk0 + n, 0] = gc
            col_s[chunk0 + n, 1] = beta
            col_s[chunk0 + n, 2] = gt
            row_s[chunk0 + n] = gc.T[0:16, :]
            return carry

        lax.fori_loop(0, n_ch, body, 0, unroll=4)

    gate_phase(abc_ref, ncc, 0)
    gate_phase(abx_ref, ncx, ncc)

    def chunk_of(s):
        cf = s
        cb = jnp.where(s < ncc, ncc - 1 - s, nch + ncc - 1 - s)
        return cf, cb

    def bcast(tile, r):
        return jnp.broadcast_to(tile[:, r:r + 1], (C, LANE))

    def stage1(it, par, in_context):
        q_ref, k_ref, v_ref, c0 = (qc_ref, kc_ref, vc_ref, 0) if in_context else (qx_ref, kx_ref, vx_ref, ncc)
        jobs = []
        for u in range(unroll):
            cf, cb = chunk_of(it * unroll + u)
            for d in range(2):
                c = cf if d == 0 else cb
                gc_t = col_s[c, 0]
                be_t = col_s[c, 1]
                gt_t = col_s[c, 2]
                e1_t = jnp.exp(gc_t)
                tiles = dict(gc=gc_t, be=be_t, e1=e1_t, ca=be_t * e1_t, e2=jnp.exp(gt_t - gc_t),
                             gl=jnp.exp(gt_t), rows=row_s[c])
                for hl in range(hg):
                    jobs.append((par * unroll + u, d * hg + hl, pl.multiple_of((c - c0) * C, C), tiles, d, hl))
        tri = {0: ii >= jj, 1: ii <= jj}
        strict = {0: ii > jj, 1: ii < jj}

        def rd(ref, job):
            return ref[0, pl.ds(job[2], C), job[5] * D:(job[5] + 1) * D]

        qn = [rd(q_ref, j) for j in jobs]
        kn = [rd(k_ref, j) for j in jobs]
        gmat = [_dot_nt(jnp.concatenate([q, k], axis=0), k) for q, k in zip(qn, kn)]
        yield
        dec =[jnp.exp(jnp.where(tri[j[4]], bcast(j[3]["gc"], j[1])[:, :C] - j[3]["rows"][j[1]:j[1] + 1, :], -1e30))
               for j in jobs]
        lmat = [jnp.where(strict[j[4]], g[C:] * dc, 0.0) * bcast(j[3]["be"], j[1])[:, :C]
                for j, g, dc in zip(jobs, gmat, dec)]
        for j, g, dc in zip(jobs, gmat, dec):
            pqk_s[j[0], j[1]] = (g[:C] * dc).astype(BF16)
        tinv = yield from _tri_inverse(lmat, [bool(j[4]) for j in jobs], ii, jj)
        knf = [k.astype(F32) for k in kn]
        rhs = [jnp.concatenate([(rd(v_ref, j).astype(F32) * bcast(j[3]["be"], j[1])).astype(BF16),
                                (kf * bcast(j[3]["ca"], j[1])).astype(BF16)], axis=1)
               for j, kf in zip(jobs, knf)]
        uw = [_dot(t.astype(BF16), r) for t, r in zip(tinv, rhs)]
        yield
        for j, x, q, kf in zip(jobs, uw, qn, knf):
            slot, r = j[0], j[1]
            qd = (q.astype(F32) * bcast(j[3]["e1"], r)).astype(BF16)
            pu_s[slot, r] = x[:, :D].astype(BF16)
            pwq_s[slot, r] = jnp.concatenate([x[:, D:].astype(BF16), qd], axis=0)
            pkt_s[slot, r] = (kf * bcast(j[3]["e2"], r)).T.astype(BF16)
            pgl_s[slot, r] = jnp.broadcast_to(j[3]["gl"][0:8, r:r + 1], (8, LANE))

    def stage2(it, par):
        for u in range(unroll):
            slot = par * unroll + u
            cf, cb = chunk_of(it * unroll + u)
            chains = [(d * hg + hl, pl.multiple_of((cb if d else cf) * C, C), hl)
                      for d in range(2) for hl in range(hg)]
            st = [st_s[r] for r, _, _ in chains]
            ws = [_dot(pwq_s[slot, r], s.astype(BF16)) for (r, _, _), s in zip(chains, st)]
            yield
            vn = [(pu_s[slot, r].astype(F32) - w[:C]).astype(BF16) for (r, _, _), w in zip(chains, ws)]
            o = [w[C:] + _dot(pqk_s[slot, r], v) for (r, _, _), w, v in zip(chains, ws, vn)]
            kv = [_dot(pkt_s[slot, r], v) for (r, _, _), v in zip(chains, vn)]
            yield
            for (r, row0, hl), s, x, y in zip(chains, st, kv, o):
                st_s[r] = s * jnp.concatenate([pgl_s[slot, r]] * (D // 8), axis=0) + x
                o_s[pl.ds(row0, C), hl * D:(hl + 1) * D] += y

    st_s[...] = jnp.zeros_like(st_s)
    o_s[...] = jnp.zeros_like(o_s)
    _run_interleaved(stage1(0, 0, True))

    def loop_body(it, carry):
        _run_interleaved(stage1(it, it % 2, False), stage2(it - 1, (it - 1) % 2))
        return carry

    n_it = nch // unroll
    lax.fori_loop(1, n_it, loop_body, 0)
    _run_interleaved(stage2(n_it - 1, (n_it - 1) % 2))

    hn = hng_ref[...]
    for hl in range(hg):
        cols = slice(hl * D, (hl + 1) * D)
        o = o_s[tc:tc + tx, cols]
        zg = gate_ref[0, :, cols].astype(F32)
        out_ref[0, :, cols] = (_rms(o, hn) * _silu(zg)).astype(BF16)


def _dn_call(qkv_x, qkv_c, ab_x, ab_c, gate, alog_g, dtb_g, hng, *, hg, unroll):
    b, tx, w3 = qkv_x.shape
    tc = qkv_c.shape[1]
    assert tc // DN_CHUNK == unroll and (tx // DN_CHUNK) % unroll == 0
    nslot = 2 * unroll
    nhg = DN_HEADS // hg
    wd = hg * DN_HEAD_DIM
    nch = (tx + tc) // DN_CHUNK
    nchain = 2 * hg
    kern = functools.partial(_dn_kernel, hg=hg, tx=tx, tc=tc, unroll=unroll)

    def col(t, off):
        return pl.BlockSpec((1, t, wd), lambda bi, gi: (bi, 0, off * nhg + gi))

    def ab(t):
        return pl.BlockSpec((1, t, LANE), lambda bi, gi: (bi, 0, gi), pipeline_mode=pl.Buffered(1))

    small = pl.BlockSpec((1, 1, LANE), lambda bi, gi: (gi, 0, 0))
    return pl.pallas_call(
        kern,
        grid=(b, nhg),
        in_specs=[col(tx, 0), col(tx, 1), col(tx, 2), col(tc, 0), col(tc, 1), col(tc, 2),
                  ab(tx), ab(tc),
                  pl.BlockSpec((1, tx, wd), lambda bi, gi: (bi, 0, gi), pipeline_mode=pl.Buffered(1)),
                  small, small,
                  pl.BlockSpec((1, LANE), lambda bi, gi: (0, 0))],
        out_specs=pl.BlockSpec((1, tx, wd), lambda bi, gi: (bi, 0, gi)),
        out_shape=jax.ShapeDtypeStruct((b, tx, DN_HEADS * DN_HEAD_DIM), BF16),
        scratch_shapes=[
            pltpu.VMEM((nch, 3, DN_CHUNK, LANE), F32),
            pltpu.VMEM((nch, 16, DN_CHUNK), F32),
            pltpu.VMEM((nchain, DN_HEAD_DIM, DN_HEAD_DIM), F32),
            pltpu.VMEM((tc + tx, wd), F32),
            pltpu.VMEM((nslot, nchain, DN_CHUNK, DN_HEAD_DIM), BF16),
            pltpu.VMEM((nslot, nchain, 2 * DN_CHUNK, DN_HEAD_DIM), BF16),
            pltpu.VMEM((nslot, nchain, DN_HEAD_DIM, DN_CHUNK), BF16),
            pltpu.VMEM((nslot, nchain, DN_CHUNK, DN_CHUNK), BF16),
            pltpu.VMEM((nslot, nchain, 8, LANE), F32)],
        compiler_params=_cparams(("parallel", "arbitrary")),
        name="deltanet",
    )(qkv_x, qkv_x, qkv_x, qkv_c, qkv_c, qkv_c, ab_x, ab_c, gate, alog_g, dtb_g, hng)


def _outproj_kernel(x_ref, oa_ref, ob_ref, w_ref, m_ref, out_ref):
    half = oa_ref.shape[1]
    y = _dot(oa_ref[...], w_ref[:half, :]) + _dot(ob_ref[...], w_ref[half:, :])
    out_ref[...] = x_ref[...] + m_ref[0, 5:6, :] * y


def _outproj_call(x2d, oa, ob, w, m3, *, rows_per_mod, tm):
    n, d = x2d.shape
    half = oa.shape[1]
    tiles_per_mod = rows_per_mod // tm
    return pl.pallas_call(
        _outproj_kernel,
        grid=(n // tm,),
        in_specs=[pl.BlockSpec((tm, d), lambda i: (i, 0)),
                  pl.BlockSpec((tm, half), lambda i: (i, 0)),
                  pl.BlockSpec((tm, half), lambda i: (i, 0)),
                  pl.BlockSpec((2 * half, d), lambda i: (0, 0), pipeline_mode=pl.Buffered(1)),
                  pl.BlockSpec((1, N_MOD, d), lambda i: (i // tiles_per_mod, 0, 0))],
        out_specs=pl.BlockSpec((tm, d), lambda i: (i, 0)),
        out_shape=jax.ShapeDtypeStruct((n, d), F32),
        compiler_params=_cparams(("parallel",)),
        name="outproj",
    )(x2d, oa, ob, w, m3)


def _chain_rows(p, hg):
    nhg = DN_HEADS // hg
    rows = jnp.moveaxis(p.reshape(2, nhg, hg), 1, 0).reshape(nhg, 1, 2 * hg)
    return jnp.pad(rows, ((0, 0), (0, 0), (0, LANE - 2 * hg)))


FFN_TM = 512
FFN_TF = 512
PROJ_TM = 512
OUT_TM = 512
DN_HG = 4
DN_UNROLL = 4
MOD_ROWS = 16


def kernel(x, c, ctx, c_ctx, w_mod, b_mod, norm_g, ffn1_w_in, ffn1_w_out, w_in, conv_w, a_log, dt_bias, head_norm_g, spatial_w, spatial_b, mlp_norm_g, w_out, ffn2_w_in, ffn2_w_out, final_g):
    bsz, t, d = x.shape
    tc = ctx.shape[1]
    assert w_mod.shape[0] == 1, "single-layer block"
    assert bsz + 1 <= MOD_ROWS

    c_all = jnp.concatenate([c, c_ctx[None, :], jnp.zeros((MOD_ROWS - bsz - 1, d), F32)], axis=0)
    m3 = _mod_call(c_all, w_mod[0], b_mod).reshape(MOD_ROWS, N_MOD, d)

    g0, g1, g2 = norm_g[0, 0:1], norm_g[0, 1:2], norm_g[0, 2:3]
    w_ffn1 = _prep_ffn_weights(ffn1_w_in[0], ffn1_w_out[0], FFN_TF)
    w_ffn2 = _prep_ffn_weights(ffn2_w_in[0], ffn2_w_out[0], FFN_TF)

    x1, h_x = _ffn_call(x.reshape(bsz * t, d), m3, g0, w_ffn1, g1, mode="ffn1", mod_base=0,
                        rows_per_mod=t, mod_offset=0, tm=FFN_TM, tf=FFN_TF)
    (h_c,) = _ffn_call(ctx.reshape(bsz * tc, d), m3, g0, w_ffn1, g1, mode="ffn1ctx", mod_base=0,
                       rows_per_mod=bsz * tc, mod_offset=bsz, tm=FFN_TM, tf=FFN_TF)

    w_main, w_ab = _prep_inproj_weights(w_in[0], DN_HG)
    sb = jnp.broadcast_to(spatial_b[0][:, :, None], (MLP_GROUPS, MLP_CHUNK, LANE))
    qkv_x, ab_x, gate_x, o_b = _inproj_call(h_x, w_main, w_ab, conv_w[0], spatial_w[0].astype(BF16), sb, mlp_norm_g,
                                            tm=PROJ_TM, seq_len=t, with_mlp=True)
    qkv_c, ab_c = _inproj_call(h_c, w_main, w_ab, conv_w[0], None, None, None,
                               tm=min(PROJ_TM, tc), seq_len=tc, with_mlp=False)

    wq = DN_HEADS * DN_HEAD_DIM
    o_a = _dn_call(qkv_x.reshape(bsz, t, 3 * wq), qkv_c.reshape(bsz, tc, 3 * wq),
                   ab_x.reshape(bsz, t, -1), ab_c.reshape(bsz, tc, -1), gate_x.reshape(bsz, t, wq),
                   _chain_rows(a_log[0], DN_HG), _chain_rows(dt_bias[0], DN_HG), head_norm_g,
                   hg=DN_HG, unroll=DN_UNROLL)

    x2 = _outproj_call(x1, o_a.reshape(bsz * t, wq), o_b, w_out[0].astype(BF16), m3, rows_per_mod=t, tm=OUT_TM)
    (out,) = _ffn_call(x2, m3, g2, w_ffn2, final_g[None, :], mode="ffn2", mod_base=6,
                       rows_per_mod=t, mod_offset=0, tm=FFN_TM, tf=FFN_TF)
    return out.reshape(bsz, t, d)
```

```python
import functools

import jax
import jax.numpy as jnp
import numpy as np
from jax import lax
from jax.experimental import pallas as pl
from jax.experimental.pallas import tpu as pltpu

F32 = jnp.float32
BF16 = jnp.bfloat16

EPS = 1e-6
LANE = 128
N_MOD = 9
DN_HEADS = 8
DN_HEAD_DIM = 128
DN_CHUNK = 64
CONV_K = 5
MLP_GROUPS = 8
MLP_CHUNK = 128
PROJ_CB = 1024
HALO = 16
V7X_VMEM_LIMIT = 60000 * 1024
ROW_CHUNK = 16
ROW_UNROLL = 8


def _cparams(sem):
    return pltpu.CompilerParams(dimension_semantics=sem, vmem_limit_bytes=V7X_VMEM_LIMIT)


def _rms(x, g):
    return x * lax.rsqrt(jnp.mean(x * x, axis=-1, keepdims=True) + EPS) * g


def _silu(x):
    return x * jax.nn.sigmoid(x)


def _dot(a, b):
    return jnp.dot(a, b, preferred_element_type=F32)


def _dot_nt(a, b):
    return lax.dot_general(a, b, (((1,), (1,)), ((), ())), preferred_element_type=F32)


def _mod_kernel(c_ref, w_ref, b_ref, o_ref):
    s = _silu(c_ref[...]).astype(BF16)
    o_ref[...] = _dot(s, w_ref[...].astype(BF16)) + b_ref[...]


def _mod_call(c_all, w_mod, b_mod, tn=1024):
    rows, d = c_all.shape
    n = w_mod.shape[1]
    return pl.pallas_call(
        _mod_kernel,
        grid=(n // tn,),
        in_specs=[pl.BlockSpec((rows, d), lambda j: (0, 0)),
                  pl.BlockSpec((d, tn), lambda j: (0, j)),
                  pl.BlockSpec((1, tn), lambda j: (0, j))],
        out_specs=pl.BlockSpec((rows, tn), lambda j: (0, j)),
        out_shape=jax.ShapeDtypeStruct((rows, n), F32),
        compiler_params=_cparams(("arbitrary",)),
        name="mod",
    )(c_all, w_mod, b_mod)


def _ffn_kernel(*refs, mode, mod_base, nf, n_tiles):
    xn_ref, xe_ref, mn_ref, me_ref, g_ref, wa_ref, wb_ref, wo_ref, g2_ref = refs[:9]
    h_bufs, acc_bufs = refs[-4:-2], refs[-2:]
    if mode == "ffn1":
        out_ref, h2_ref = refs[9:11]
    elif mode == "ffn2":
        out_ref, h2_ref = refs[9], None
    else:
        out_ref, h2_ref = None, refs[9]
    i = pl.program_id(0)
    j = pl.program_id(1)
    tm, d = h_bufs[0].shape
    n_chunks = tm // ROW_CHUNK
    per_step = -(-n_chunks // nf)

    def bc(v):
        return jnp.broadcast_to(v, (ROW_CHUNK, d))

    def norm_scale(x, gain, shift=None):
        y = x * lax.rsqrt(jnp.mean(x * x, axis=-1, keepdims=True) + EPS) * gain
        return y if shift is None else y + shift

    def prologue_vecs(m_ref):
        return bc(g_ref[...] * (1.0 + m_ref[0, mod_base + 1:mod_base + 2, :])), bc(m_ref[0, mod_base:mod_base + 1, :])

    def prologue(x_ref, rows, buf, vecs):
        h_bufs[buf][rows, :] = norm_scale(x_ref[rows, :], *vecs).astype(BF16)
        acc_bufs[buf][rows, :] = jnp.zeros((ROW_CHUNK, d), F32)

    def epilogue_vecs():
        half_gate = bc(0.5 * me_ref[0, mod_base + 2:mod_base + 3, :])
        if mode == "ffn2":
            return half_gate, bc(g2_ref[...]), None
        return (half_gate, bc(g2_ref[...] * (1.0 + me_ref[0, mod_base + 4:mod_base + 5, :])),
                bc(me_ref[0, mod_base + 3:mod_base + 4, :]))

    def epilogue(rows, buf, vecs):
        half_gate, gain2, shift2 = vecs
        xq = xe_ref[rows, :] + half_gate * acc_bufs[buf][rows, :]
        if mode == "ffn2":
            out_ref[rows, :] = norm_scale(xq, gain2)
        else:
            if mode == "ffn1":
                out_ref[rows, :] = xq
            h2_ref[rows, :] = norm_scale(xq, gain2, shift2).astype(BF16)

    def step_rows():
        for k in range(per_step):
            c = jnp.minimum(j * per_step + k, n_chunks - 1)
            yield pl.ds(pl.multiple_of(c * ROW_CHUNK, ROW_CHUNK), ROW_CHUNK)

    @pl.when((i == 0) & (j == 0))
    def _():
        vecs = prologue_vecs(me_ref)

        def body(r, carry):
            rows = pl.ds(pl.multiple_of(r * ROW_CHUNK, ROW_CHUNK), ROW_CHUNK)
            prologue(xe_ref, rows, 0, vecs)
            acc_bufs[1][rows, :] = jnp.zeros((ROW_CHUNK, d), F32)
            return carry

        lax.fori_loop(0, n_chunks, body, 0, unroll=ROW_UNROLL)

    for cur in range(2):
        oth = 1 - cur

        @pl.when((i < n_tiles) & (i % 2 == cur))
        def _(cur=cur, oth=oth):
            h = h_bufs[cur][...]
            act = (_silu(_dot(h, wa_ref[...])) * _dot(h, wb_ref[...])).astype(BF16)
            acc_bufs[cur][...] += _dot(act, wo_ref[...])
            ev = epilogue_vecs()
            for rows in step_rows():
                epilogue(rows, oth, ev)
            pv = prologue_vecs(mn_ref)
            for rows in step_rows():
                prologue(xn_ref, rows, oth, pv)

        @pl.when((i == n_tiles) & (i % 2 == cur))
        def _(oth=oth):
            ev = epilogue_vecs()
            for rows in step_rows():
                epilogue(rows, oth, ev)


def _split_cast_kernel(a_ref, b_ref, wa_ref, wb_ref, tail_s, *, f, tb, nb):
    s = pl.program_id(0)
    rem = f % tb
    lane = lax.broadcasted_iota(jnp.int32, a_ref.shape, 1)
    a_blk = jnp.minimum(s, nb - 1)
    wa_ref[...] = jnp.where(a_blk * tb + lane < f, a_ref[...], 0.0).astype(BF16)

    @pl.when(s == 0)
    def _():
        wb_ref[...] = jnp.zeros_like(wb_ref)

    @pl.when(s > 0)
    def _():
        b = jnp.concatenate([tail_s[...], b_ref[:, :rem]], axis=1) if rem else tail_s[...]
        wb_ref[...] = jnp.where((s - 1) * tb + lane < f, b, 0.0).astype(BF16)

    tail_s[...] = b_ref[:, rem:]


def _split_cast_call(w_in, f_pad, tb=256):
    d, f2 = w_in.shape
    f = f2 // 2
    assert f2 % tb == 0 and f_pad % tb == 0 and (f % tb) % LANE == 0
    nb = f_pad // tb
    last = f2 // tb - 1
    return pl.pallas_call(
        functools.partial(_split_cast_kernel, f=f, tb=tb, nb=nb),
        grid=(nb + 1,),
        in_specs=[pl.BlockSpec((d, tb), lambda s: (0, jnp.minimum(s, nb - 1))),
                  pl.BlockSpec((d, tb), lambda s: (0, jnp.minimum(f // tb + s, last)))],
        out_specs=[pl.BlockSpec((d, tb), lambda s: (0, jnp.minimum(s, nb - 1))),
                   pl.BlockSpec((d, tb), lambda s: (0, jnp.maximum(s - 1, 0)))],
        out_shape=[jax.ShapeDtypeStruct((d, f_pad), BF16)] * 2,
        scratch_shapes=[pltpu.VMEM((d, tb - f % tb), F32)],
        compiler_params=_cparams(("arbitrary",)),
        name="ffn_weight_cast",
    )(w_in, w_in)


def _prep_ffn_weights(w_in, w_out, tf):
    f = w_in.shape[1] // 2
    pad = -f % tf
    wa, wb = _split_cast_call(w_in, f + pad)
    wo = jnp.pad(w_out.astype(BF16), ((0, pad), (0, 0)))
    return wa, wb, wo


def _ffn_call(x2d, m3, g, w3, g2, *, mode, mod_base, rows_per_mod, mod_offset, tm, tf):
    r, d = x2d.shape
    wa, wb, wo = w3
    nf = wo.shape[0] // tf
    n = r // tm
    tiles_per_mod = rows_per_mod // tm
    kern = functools.partial(_ffn_kernel, mode=mode, mod_base=mod_base, nf=nf, n_tiles=n)

    def nxt(i):
        return jnp.minimum(i + 1, n - 1)

    def prv(i):
        return jnp.maximum(i - 1, 0)

    def wblk(i, j):
        return jnp.where(i < n, j, nf - 1)

    def mod_spec(tile):
        return pl.BlockSpec((1, N_MOD, d), lambda i, j: (mod_offset + tile(i) // tiles_per_mod, 0, 0))

    in_specs = [pl.BlockSpec((tm, d), lambda i, j: (nxt(i), 0)),
                pl.BlockSpec((tm, d), lambda i, j: (prv(i), 0)),
                mod_spec(nxt), mod_spec(prv),
                pl.BlockSpec((1, d), lambda i, j: (0, 0)),
                pl.BlockSpec((d, tf), lambda i, j: (0, wblk(i, j))),
                pl.BlockSpec((d, tf), lambda i, j: (0, wblk(i, j))),
                pl.BlockSpec((tf, d), lambda i, j: (wblk(i, j), 0)),
                pl.BlockSpec((1, d), lambda i, j: (0, 0))]
    args = [x2d, x2d, m3, m3, g, wa, wb, wo, g2]
    o_spec = pl.BlockSpec((tm, d), lambda i, j: (prv(i), 0))
    o_f32 = (jax.ShapeDtypeStruct((r, d), F32), o_spec)
    o_bf = (jax.ShapeDtypeStruct((r, d), BF16), o_spec)
    outs = {"ffn2": [o_f32], "ffn1": [o_f32, o_bf], "ffn1ctx": [o_bf]}[mode]
    return pl.pallas_call(
        kern,
        grid=(n + 1, nf),
        in_specs=in_specs,
        out_specs=[o[1] for o in outs],
        out_shape=[o[0] for o in outs],
        scratch_shapes=[pltpu.VMEM((tm, d), BF16)] * 2 + [pltpu.VMEM((tm, d), F32)] * 2,
        compiler_params=_cparams(("arbitrary", "arbitrary")),
        name=mode,
    )(*args)


def _inproj_kernel(*refs, tm, seq_tiles, with_mlp):
    if with_mlp:
        (h_ref, hp_ref, hn_ref, w_ref, wab_ref, cw_ref, sw_ref, sb_ref, mng_ref,
         qkv_ref, ab_ref, gate_ref, ob_ref) = refs
    else:
        h_ref, hp_ref, hn_ref, w_ref, wab_ref, cw_ref, qkv_ref, ab_ref = refs
    cb = PROJ_CB
    h = h_ref[...]

    def block(jb):
        return _dot(h, w_ref[:, jb * cb:(jb + 1) * cb])

    ab_ref[...] = _dot(h, wab_ref[...])

    i = pl.program_id(0)
    pos = i % seq_tiles
    keep_prev = jnp.where(pos > 0, 1.0, 0.0).astype(BF16)
    keep_next = jnp.where(pos < seq_tiles - 1, 1.0, 0.0).astype(BF16)
    hext = jnp.concatenate([hp_ref[...] * keep_prev, h, hn_ref[...] * keep_next], axis=0)
    first = HALO - CONV_K // 2
    for jb in range(3):
        zext = _dot(hext, w_ref[:, jb * cb:(jb + 1) * cb])
        for rc in range(tm // DN_CHUNK):
            win = zext[rc * DN_CHUNK:rc * DN_CHUNK + DN_CHUNK + 2 * HALO]
            y = cw_ref[0:1, jb * cb:(jb + 1) * cb] * win[first:first + DN_CHUNK]
            for t in range(1, CONV_K):
                y = y + cw_ref[t:t + 1, jb * cb:(jb + 1) * cb] * win[first + t:first + t + DN_CHUNK]
            y = _silu(y)
            if jb < 2:
                parts = []
                for hd in range(cb // DN_HEAD_DIM):
                    yh = y[:, hd * DN_HEAD_DIM:(hd + 1) * DN_HEAD_DIM]
                    yh = yh * lax.rsqrt(jnp.sum(yh * yh, axis=-1, keepdims=True) + EPS)
                    parts.append(yh * (DN_HEAD_DIM ** -0.5) if jb == 0 else yh)
                y = jnp.concatenate(parts, axis=1)
            qkv_ref[rc * DN_CHUNK:(rc + 1) * DN_CHUNK, jb * cb:(jb + 1) * cb] = y.astype(BF16)
    if not with_mlp:
        return
    gate_ref[...] = block(3).astype(BF16)
    u = jax.nn.gelu(block(4))
    v = jax.nn.gelu(block(5))
    for g in range(MLP_GROUPS):
        cols = slice(g * LANE, (g + 1) * LANE)
        vn = _rms(v[:, cols], mng_ref[:, cols]).astype(BF16)
        for ci in range(tm // MLP_CHUNK):
            rows = slice(ci * MLP_CHUNK, (ci + 1) * MLP_CHUNK)
            s = _dot(sw_ref[g], vn[rows, :]) + sb_ref[g]
            ob_ref[rows, cols] = (u[rows, cols] * s).astype(BF16)


def _prep_inproj_weights(w_in, hg):
    wq = DN_HEADS * DN_HEAD_DIM
    n_ab = 4 * DN_HEADS
    w_main = jnp.concatenate([w_in[:, :4 * wq].astype(BF16), w_in[:, 4 * wq + n_ab:].astype(BF16)], axis=1)
    w = w_in[:, 4 * wq:4 * wq + n_ab].astype(BF16)
    wa = w[:, :2 * DN_HEADS].reshape(-1, 2, DN_HEADS // hg, hg)
    wb = w[:, 2 * DN_HEADS:].reshape(-1, 2, DN_HEADS // hg, hg)
    grp = jnp.concatenate([jnp.moveaxis(wa, 2, 1), jnp.moveaxis(wb, 2, 1)], axis=2)
    grp = grp.reshape(w.shape[0], DN_HEADS // hg, 4 * hg)
    w_ab = jnp.pad(grp, ((0, 0), (0, 0), (0, LANE - 4 * hg))).reshape(w.shape[0], -1)
    return w_main, w_ab


def _inproj_call(h2, w_main, w_ab, conv_w, sw, sb, mng, *, tm, seq_len, with_mlp):
    n, d = h2.shape
    cb = PROJ_CB
    nab = w_ab.shape[1]
    ncb = 6 if with_mlp else 3
    assert seq_len % tm == 0 and tm % HALO == 0
    kern = functools.partial(_inproj_kernel, tm=tm, seq_tiles=seq_len // tm, with_mlp=with_mlp)
    once = dict(pipeline_mode=pl.Buffered(1))
    hb = tm // HALO
    in_specs = [pl.BlockSpec((tm, d), lambda i: (i, 0)),
                pl.BlockSpec((HALO, d), lambda i: (jnp.maximum(i * hb - 1, 0), 0)),
                pl.BlockSpec((HALO, d), lambda i: (jnp.minimum((i + 1) * hb, n // HALO - 1), 0)),
                pl.BlockSpec((d, ncb * cb), lambda i: (0, 0), **once),
                pl.BlockSpec((d, nab), lambda i: (0, 0), **once),
                pl.BlockSpec((CONV_K, 3 * cb), lambda i: (0, 0), **once)]
    out_specs = [pl.BlockSpec((tm, 3 * cb), lambda i: (i, 0)),
                 pl.BlockSpec((tm, nab), lambda i: (i, 0))]
    out_shape = [jax.ShapeDtypeStruct((n, 3 * cb), BF16), jax.ShapeDtypeStruct((n, nab), F32)]
    args = [h2, h2, h2, w_main, w_ab, conv_w]
    if with_mlp:
        in_specs += [pl.BlockSpec((MLP_GROUPS, MLP_CHUNK, MLP_CHUNK), lambda i: (0, 0, 0), **once),
                     pl.BlockSpec((MLP_GROUPS, MLP_CHUNK, LANE), lambda i: (0, 0, 0), **once),
                     pl.BlockSpec((1, cb), lambda i: (0, 0))]
        out_specs += [pl.BlockSpec((tm, cb), lambda i: (i, 0))] * 2
        out_shape += [jax.ShapeDtypeStruct((n, cb), BF16)] * 2
        args += [sw, sb, mng]
    return pl.pallas_call(
        kern,
        grid=(n // tm,),
        in_specs=in_specs,
        out_specs=out_specs,
        out_shape=out_shape,
        compiler_params=_cparams(("parallel",)),
        name="inproj" if with_mlp else "inproj_ctx",
    )(*args)


def _tri_inverse(lmats, uppers, ii, jj):
    eye = (ii == jj).astype(F32)
    b16 = (ii // 16) == (jj // 16)
    b32 = (ii // 32) == (jj // 32)
    m1 = {False: b32 & ((ii // 16) > (jj // 16)), True: b32 & ((ii // 16) < (jj // 16))}
    m2 = {False: (ii // 32) > (jj // 32), True: (ii // 32) < (jj // 32)}
    a1 = [jnp.where(b16, l, 0.0) for l in lmats]
    a1b = [a.astype(BF16) for a in a1]
    a2b = [_dot(a, a).astype(BF16) for a in a1b]
    yield
    a4b = [_dot(a, a).astype(BF16) for a in a2b]
    yield
    a8b = [_dot(a, a).astype(BF16) for a in a4b]
    yield
    p = [eye - a for a in a1]
    for ab in (a2b, a4b, a8b):
        p = [pi + _dot(pi.astype(BF16), a) for pi, a in zip(p, ab)]
        yield
    for masks in (m1, m2):
        pb = [pi.astype(BF16) for pi in p]
        xm = [_dot(jnp.where(masks[u], l, 0.0).astype(BF16), b).astype(BF16)
              for l, u, b in zip(lmats, uppers, pb)]
        yield
        p = [pi - _dot(b, x) for pi, b, x in zip(p, pb, xm)]
        yield
    return p


def _run_interleaved(*gens):
    live = list(gens)
    while live:
        for g in list(live):
            try:
                next(g)
            except StopIteration:
                live.remove(g)


def _dn_kernel(qx_ref, kx_ref, vx_ref, qc_ref, kc_ref, vc_ref, abx_ref, abc_ref, gate_ref,
               alog_ref, dtb_ref, hng_ref,
               out_ref,
               col_s, row_s, st_s, o_s, pu_s, pwq_s, pkt_s, pqk_s, pgl_s,
               *, hg, tx, tc, unroll):
    C = DN_CHUNK
    D = DN_HEAD_DIM
    ncc, ncx = tc // C, tx // C
    nch = ncc + ncx
    nchain = 2 * hg

    ii = lax.broadcasted_iota(jnp.int32, (C, C), 0)
    jj = lax.broadcasted_iota(jnp.int32, (C, C), 1)
    lane_t = lax.broadcasted_iota(jnp.int32, (C, LANE), 1)

    tril = (ii >= jj).astype(F32)
    triu = (ii <= jj).astype(F32)
    ones = jnp.ones((C, C), F32)
    sum_mats = jnp.concatenate([tril, triu, ones], axis=0).astype(BF16)
    neg_a = -jnp.exp(alog_ref[0])
    dtb = dtb_ref[0]

    def gate_phase(ab_ref, n_ch, chunk0):
        def body(n, carry):
            t = ab_ref[0, pl.ds(pl.multiple_of(n * C, C), C), :]
            g = neg_a * jax.nn.softplus(t + dtb)
            g = jnp.where(lane_t < nchain, g, 0.0)
            hi = g.astype(BF16)
            r1 = g - hi.astype(F32)
            mid = r1.astype(BF16)
            lo = (r1 - mid.astype(F32)).astype(BF16)
            cs = _dot(sum_mats, jnp.concatenate([hi, mid, lo], axis=1))
            cs = cs[:, :LANE] + cs[:, LANE:2 * LANE] + cs[:, 2 * LANE:]
            gt = cs[2 * C:]
            gc = jnp.where(lane_t < hg, cs[:C], cs[C:2 * C])
            beta = pltpu.roll(jax.nn.sigmoid(t), LANE - nchain, 1)
            col_s[chunk0 + n, 0] = gc
            col_s[chunk0 + n, 1] = beta
            col_s[chunk0 + n, 2] = gt
            row_s[chunk0 + n] = gc.T[0:16, :]
            return carry

        lax.fori_loop(0, n_ch, body, 0, unroll=4)

    gate_phase(abc_ref, ncc, 0)
    gate_phase(abx_ref, ncx, ncc)

    def chunk_of(s):
        cf = s
        cb = jnp.where(s < ncc, ncc - 1 - s, nch + ncc - 1 - s)
        return cf, cb

    def bcast(tile, r):
        return jnp.broadcast_to(tile[:, r:r + 1], (C, LANE))

    def stage1(it, par, in_context):
        q_ref, k_ref, v_ref, c0 = (qc_ref, kc_ref, vc_ref, 0) if in_context else (qx_ref, kx_ref, vx_ref, ncc)
        jobs = []
        for u in range(unroll):
            cf, cb = chunk_of(it * unroll + u)
            for d in range(2):
                c = cf if d == 0 else cb
                gc_t = col_s[c, 0]
                be_t = col_s[c, 1]
                gt_t = col_s[c, 2]
                e1_t = jnp.exp(gc_t)
                tiles = dict(gc=gc_t, be=be_t, e1=e1_t, ca=be_t * e1_t, e2=jnp.exp(gt_t - gc_t),
                             gl=jnp.exp(gt_t), rows=row_s[c])
                for hl in range(hg):
                    jobs.append((par * unroll + u, d * hg + hl, pl.multiple_of((c - c0) * C, C), tiles, d, hl))
        tri = {0: ii >= jj, 1: ii <= jj}
        strict = {0: ii > jj, 1: ii < jj}

        def rd(ref, job):
            return ref[0, pl.ds(job[2], C), job[5] * D:(job[5] + 1) * D]

        qn = [rd(q_ref, j) for j in jobs]
        kn = [rd(k_ref, j) for j in jobs]
        gmat = [_dot_nt(jnp.concatenate([q, k], axis=0), k) for q, k in zip(qn, kn)]
        yield
        dec =[jnp.exp(jnp.where(tri[j[4]], bcast(j[3]["gc"], j[1])[:, :C] - j[3]["rows"][j[1]:j[1] + 1, :], -1e30))
               for j in jobs]
        lmat = [jnp.where(strict[j[4]], g[C:] * dc, 0.0) * bcast(j[3]["be"], j[1])[:, :C]
                for j, g, dc in zip(jobs, gmat, dec)]
        for j, g, dc in zip(jobs, gmat, dec):
            pqk_s[j[0], j[1]] = (g[:C] * dc).astype(BF16)
        tinv = yield from _tri_inverse(lmat, [bool(j[4]) for j in jobs], ii, jj)
        knf = [k.astype(F32) for k in kn]
        rhs = [jnp.concatenate([(rd(v_ref, j).astype(F32) * bcast(j[3]["be"], j[1])).astype(BF16),
                                (kf * bcast(j[3]["ca"], j[1])).astype(BF16)], axis=1)
               for j, kf in zip(jobs, knf)]
        uw = [_dot(t.astype(BF16), r) for t, r in zip(tinv, rhs)]
        yield
        for j, x, q, kf in zip(jobs, uw, qn, knf):
            slot, r = j[0], j[1]
            qd = (q.astype(F32) * bcast(j[3]["e1"], r)).astype(BF16)
            pu_s[slot, r] = x[:, :D].astype(BF16)
            pwq_s[slot, r] = jnp.concatenate([x[:, D:].astype(BF16), qd], axis=0)
            pkt_s[slot, r] = (kf * bcast(j[3]["e2"], r)).T.astype(BF16)
            pgl_s[slot, r] = jnp.broadcast_to(j[3]["gl"][0:8, r:r + 1], (8, LANE))

    def stage2(it, par):
        for u in range(unroll):
            slot = par * unroll + u
            cf, cb = chunk_of(it * unroll + u)
            chains = [(d * hg + hl, pl.multiple_of((cb if d else cf) * C, C), hl)
                      for d in range(2) for hl in range(hg)]
            st = [st_s[r] for r, _, _ in chains]
            ws = [_dot(pwq_s[slot, r], s.astype(BF16)) for (r, _, _), s in zip(chains, st)]
            yield
            vn = [(pu_s[slot, r].astype(F32) - w[:C]).astype(BF16) for (r, _, _), w in zip(chains, ws)]
            o = [w[C:] + _dot(pqk_s[slot, r], v) for (r, _, _), w, v in zip(chains, ws, vn)]
            kv = [_dot(pkt_s[slot, r], v) for (r, _, _), v in zip(chains, vn)]
            yield
            for (r, row0, hl), s, x, y in zip(chains, st, kv, o):
                st_s[r] = s * jnp.concatenate([pgl_s[slot, r]] * (D // 8), axis=0) + x
                o_s[pl.ds(row0, C), hl * D:(hl + 1) * D] += y

    st_s[...] = jnp.zeros_like(st_s)
    o_s[...] = jnp.zeros_like(o_s)
    _run_interleaved(stage1(0, 0, True))

    def loop_body(it, carry):
        _run_interleaved(stage1(it, it % 2, False), stage2(it - 1, (it - 1) % 2))
        return carry

    n_it = nch // unroll
    lax.fori_loop(1, n_it, loop_body, 0)
    _run_interleaved(stage2(n_it - 1, (n_it - 1) % 2))

    hn = hng_ref[...]
    for hl in range(hg):
        cols = slice(hl * D, (hl + 1) * D)
        o = o_s[tc:tc + tx, cols]
        zg = gate_ref[0, :, cols].astype(F32)
        out_ref[0, :, cols] = (_rms(o, hn) * _silu(zg)).astype(BF16)


def _dn_call(qkv_x, qkv_c, ab_x, ab_c, gate, alog_g, dtb_g, hng, *, hg, unroll):
    b, tx, w3 = qkv_x.shape
    tc = qkv_c.shape[1]
    assert tc // DN_CHUNK == unroll and (tx // DN_CHUNK) % unroll == 0
    nslot = 2 * unroll
    nhg = DN_HEADS // hg
    wd = hg * DN_HEAD_DIM
    nch = (tx + tc) // DN_CHUNK
    nchain = 2 * hg
    kern = functools.partial(_dn_kernel, hg=hg, tx=tx, tc=tc, unroll=unroll)

    def col(t, off):
        return pl.BlockSpec((1, t, wd), lambda bi, gi: (bi, 0, off * nhg + gi))

    def ab(t):
        return pl.BlockSpec((1, t, LANE), lambda bi, gi: (bi, 0, gi), pipeline_mode=pl.Buffered(1))

    small = pl.BlockSpec((1, 1, LANE), lambda bi, gi: (gi, 0, 0))
    return pl.pallas_call(
        kern,
        grid=(b, nhg),
        in_specs=[col(tx, 0), col(tx, 1), col(tx, 2), col(tc, 0), col(tc, 1), col(tc, 2),
                  ab(tx), ab(tc),
                  pl.BlockSpec((1, tx, wd), lambda bi, gi: (bi, 0, gi), pipeline_mode=pl.Buffered(1)),
                  small, small,
                  pl.BlockSpec((1, LANE), lambda bi, gi: (0, 0))],
        out_specs=pl.BlockSpec((1, tx, wd), lambda bi, gi: (bi, 0, gi)),
        out_shape=jax.ShapeDtypeStruct((b, tx, DN_HEADS * DN_HEAD_DIM), BF16),
        scratch_shapes=[
            pltpu.VMEM((nch, 3, DN_CHUNK, LANE), F32),
            pltpu.VMEM((nch, 16, DN_CHUNK), F32),
            pltpu.VMEM((nchain, DN_HEAD_DIM, DN_HEAD_DIM), F32),
            pltpu.VMEM((tc + tx, wd), F32),
            pltpu.VMEM((nslot, nchain, DN_CHUNK, DN_HEAD_DIM), BF16),
            pltpu.VMEM((nslot, nchain, 2 * DN_CHUNK, DN_HEAD_DIM), BF16),
            pltpu.VMEM((nslot, nchain, DN_HEAD_DIM, DN_CHUNK), BF16),
            pltpu.VMEM((nslot, nchain, DN_CHUNK, DN_CHUNK), BF16),
            pltpu.VMEM((nslot, nchain, 8, LANE), F32)],
        compiler_params=_cparams(("parallel", "arbitrary")),
        name="deltanet",
    )(qkv_x, qkv_x, qkv_x, qkv_c, qkv_c, qkv_c, ab_x, ab_c, gate, alog_g, dtb_g, hng)


def _outproj_kernel(x_ref, oa_ref, ob_ref, w_ref, m_ref, out_ref):
    half = oa_ref.shape[1]
    y = _dot(oa_ref[...], w_ref[:half, :]) + _dot(ob_ref[...], w_ref[half:, :])
    out_ref[...] = x_ref[...] + m_ref[0, 5:6, :] * y


def _outproj_call(x2d, oa, ob, w, m3, *, rows_per_mod, tm):
    n, d = x2d.shape
    half = oa.shape[1]
    tiles_per_mod = rows_per_mod // tm
    return pl.pallas_call(
        _outproj_kernel,
        grid=(n // tm,),
        in_specs=[pl.BlockSpec((tm, d), lambda i: (i, 0)),
                  pl.BlockSpec((tm, half), lambda i: (i, 0)),
                  pl.BlockSpec((tm, half), lambda i: (i, 0)),
                  pl.BlockSpec((2 * half, d), lambda i: (0, 0), pipeline_mode=pl.Buffered(1)),
                  pl.BlockSpec((1, N_MOD, d), lambda i: (i // tiles_per_mod, 0, 0))],
        out_specs=pl.BlockSpec((tm, d), lambda i: (i, 0)),
        out_shape=jax.ShapeDtypeStruct((n, d), F32),
        compiler_params=_cparams(("parallel",)),
        name="outproj",
    )(x2d, oa, ob, w, m3)


def _chain_rows(p, hg):
    nhg = DN_HEADS // hg
    rows = jnp.moveaxis(p.reshape(2, nhg, hg), 1, 0).reshape(nhg, 1, 2 * hg)
    return jnp.pad(rows, ((0, 0), (0, 0), (0, LANE - 2 * hg)))


FFN_TM = 512
FFN_TF = 512
PROJ_TM = 512
OUT_TM = 512
DN_HG = 4
DN_UNROLL = 4
MOD_ROWS = 16


def kernel(x, c, ctx, c_ctx, w_mod, b_mod, norm_g, ffn1_w_in, ffn1_w_out, w_in, conv_w, a_log, dt_bias, head_norm_g, spatial_w, spatial_b, mlp_norm_g, w_out, ffn2_w_in, ffn2_w_out, final_g):
    bsz, t, d = x.shape
    tc = ctx.shape[1]
    assert w_mod.shape[0] == 1, "single-layer block"
    assert bsz + 1 <= MOD_ROWS

    c_all = jnp.concatenate([c, c_ctx[None, :], jnp.zeros((MOD_ROWS - bsz - 1, d), F32)], axis=0)
    m3 = _mod_call(c_all, w_mod[0], b_mod).reshape(MOD_ROWS, N_MOD, d)

    g0, g1, g2 = norm_g[0, 0:1], norm_g[0, 1:2], norm_g[0, 2:3]
    w_ffn1 = _prep_ffn_weights(ffn1_w_in[0], ffn1_w_out[0], FFN_TF)
    w_ffn2 = _prep_ffn_weights(ffn2_w_in[0], ffn2_w_out[0], FFN_TF)

    x1, h_x = _ffn_call(x.reshape(bsz * t, d), m3, g0, w_ffn1, g1, mode="ffn1", mod_base=0,
                        rows_per_mod=t, mod_offset=0, tm=FFN_TM, tf=FFN_TF)
    (h_c,) = _ffn_call(ctx.reshape(bsz * tc, d), m3, g0, w_ffn1, g1, mode="ffn1ctx", mod_base=0,
                       rows_per_mod=bsz * tc, mod_offset=bsz, tm=FFN_TM, tf=FFN_TF)

    w_main, w_ab = _prep_inproj_weights(w_in[0], DN_HG)
    sb = jnp.broadcast_to(spatial_b[0][:, :, None], (MLP_GROUPS, MLP_CHUNK, LANE))
    qkv_x, ab_x, gate_x, o_b = _inproj_call(h_x, w_main, w_ab, conv_w[0], spatial_w[0].astype(BF16), sb, mlp_norm_g,
                                            tm=PROJ_TM, seq_len=t, with_mlp=True)
    qkv_c, ab_c = _inproj_call(h_c, w_main, w_ab, conv_w[0], None, None, None,
                               tm=min(PROJ_TM, tc), seq_len=tc, with_mlp=False)

    wq = DN_HEADS * DN_HEAD_DIM
    o_a = _dn_call(qkv_x.reshape(bsz, t, 3 * wq), qkv_c.reshape(bsz, tc, 3 * wq),
                   ab_x.reshape(bsz, t, -1), ab_c.reshape(bsz, tc, -1), gate_x.reshape(bsz, t, wq),
                   _chain_rows(a_log[0], DN_HG), _chain_rows(dt_bias[0], DN_HG), head_norm_g,
                   hg=DN_HG, unroll=DN_UNROLL)

    x2 = _outproj_call(x1, o_a.reshape(bsz * t, wq), o_b, w_out[0].astype(BF16), m3, rows_per_mod=t, tm=OUT_TM)
    (out,) = _ffn_call(x2, m3, g2, w_ffn2, final_g[None, :], mode="ffn2", mod_base=6,
                       rows_per_mod=t, mod_offset=0, tm=FFN_TM, tf=FFN_TF)
    return out.reshape(bsz, t, d)
```

```python
import functools

import jax
import jax.numpy as jnp
import numpy as np
from jax import lax
from jax.experimental import pallas as pl
from jax.experimental.pallas import tpu as pltpu

F32 = jnp.float32
BF16 = jnp.bfloat16

EPS = 1e-6
LANE = 128
N_MOD = 9
DN_HEADS = 8
DN_HEAD_DIM = 128
DN_CHUNK = 64
CONV_K = 5
MLP_GROUPS = 8
MLP_CHUNK = 128
PROJ_CB = 1024
HALO = 16
V7X_VMEM_LIMIT = 60000 * 1024
ROW_CHUNK = 16
ROW_UNROLL = 8


def _cparams(sem):
    return pltpu.CompilerParams(dimension_semantics=sem, vmem_limit_bytes=V7X_VMEM_LIMIT)


def _rms(x, g):
    return x * lax.rsqrt(jnp.mean(x * x, axis=-1, keepdims=True) + EPS) * g


def _silu(x):
    return x * jax.nn.sigmoid(x)


def _dot(a, b):
    return jnp.dot(a, b, preferred_element_type=F32)


def _dot_nt(a, b):
    return lax.dot_general(a, b, (((1,), (1,)), ((), ())), preferred_element_type=F32)


def _mod_kernel(c_ref, w_ref, b_ref, o_ref):
    s = _silu(c_ref[...]).astype(BF16)
    o_ref[...] = _dot(s, w_ref[...].astype(BF16)) + b_ref[...]


def _mod_call(c_all, w_mod, b_mod, tn=1024):
    rows, d = c_all.shape
    n = w_mod.shape[1]
    return pl.pallas_call(
        _mod_kernel,
        grid=(n // tn,),
        in_specs=[pl.BlockSpec((rows, d), lambda j: (0, 0)),
                  pl.BlockSpec((d, tn), lambda j: (0, j)),
                  pl.BlockSpec((1, tn), lambda j: (0, j))],
        out_specs=pl.BlockSpec((rows, tn), lambda j: (0, j)),
        out_shape=jax.ShapeDtypeStruct((rows, n), F32),
        compiler_params=_cparams(("arbitrary",)),
        name="mod",
    )(c_all, w_mod, b_mod)


def _next_tile_step(nf):
    return nf - max(1, nf // 3)


def _ffn_kernel(*refs, mode, mod_base, nf, n_tiles):
    xn_ref, xe_ref, mn_ref, me_ref, g_ref, wa_ref, wb_ref, wo_ref, g2_ref = refs[:9]
    h_bufs, acc_bufs = refs[-4:-2], refs[-2:]
    if mode == "ffn1":
        out_ref, h2_ref = refs[9:11]
    elif mode == "ffn2":
        out_ref, h2_ref = refs[9], None
    else:
        out_ref, h2_ref = None, refs[9]
    i = pl.program_id(0)
    j = pl.program_id(1)
    tm, d = h_bufs[0].shape
    n_chunks = tm // ROW_CHUNK

    def bc(v):
        return jnp.broadcast_to(v, (ROW_CHUNK, d))

    def norm_scale(x, gain, shift=None):
        y = x * lax.rsqrt(jnp.mean(x * x, axis=-1, keepdims=True) + EPS) * gain
        return y if shift is None else y + shift

    def prologue_vecs(m_ref):
        return bc(g_ref[...] * (1.0 + m_ref[0, mod_base + 1:mod_base + 2, :])), bc(m_ref[0, mod_base:mod_base + 1, :])

    def prologue(x_ref, rows, buf, vecs):
        h_bufs[buf][rows, :] = norm_scale(x_ref[rows, :], *vecs).astype(BF16)
        acc_bufs[buf][rows, :] = jnp.zeros((ROW_CHUNK, d), F32)

    def epilogue_vecs():
        half_gate = bc(0.5 * me_ref[0, mod_base + 2:mod_base + 3, :])
        if mode == "ffn2":
            return half_gate, bc(g2_ref[...]), None
        return (half_gate, bc(g2_ref[...] * (1.0 + me_ref[0, mod_base + 4:mod_base + 5, :])),
                bc(me_ref[0, mod_base + 3:mod_base + 4, :]))

    def epilogue(rows, buf, vecs):
        half_gate, gain2, shift2 = vecs
        xq = xe_ref[rows, :] + half_gate * acc_bufs[buf][rows, :]
        if mode == "ffn2":
            out_ref[rows, :] = norm_scale(xq, gain2)
        else:
            if mode == "ffn1":
                out_ref[rows, :] = xq
            h2_ref[rows, :] = norm_scale(xq, gain2, shift2).astype(BF16)

    def step_rows(first_step=0):
        count = -(-n_chunks // (nf - first_step))
        for k in range(count):
            c = jnp.clip((j - first_step) * count + k, 0, n_chunks - 1)
            yield pl.ds(pl.multiple_of(c * ROW_CHUNK, ROW_CHUNK), ROW_CHUNK)

    @pl.when((i == 0) & (j == 0))
    def _():
        vecs = prologue_vecs(me_ref)

        def body(r, carry):
            rows = pl.ds(pl.multiple_of(r * ROW_CHUNK, ROW_CHUNK), ROW_CHUNK)
            prologue(xe_ref, rows, 0, vecs)
            acc_bufs[1][rows, :] = jnp.zeros((ROW_CHUNK, d), F32)
            return carry

        lax.fori_loop(0, n_chunks, body, 0, unroll=ROW_UNROLL)

    for cur in range(2):
        oth = 1 - cur

        @pl.when((i < n_tiles) & (i % 2 == cur))
        def _(cur=cur, oth=oth):
            h = h_bufs[cur][...]
            act = (_silu(_dot(h, wa_ref[...])) * _dot(h, wb_ref[...])).astype(BF16)
            acc_bufs[cur][...] += _dot(act, wo_ref[...])
            ev = epilogue_vecs()
            for rows in step_rows():
                epilogue(rows, oth, ev)
            pv = prologue_vecs(mn_ref)
            for rows in step_rows(_next_tile_step(nf)):
                prologue(xn_ref, rows, oth, pv)

        @pl.when((i == n_tiles) & (i % 2 == cur))
        def _(oth=oth):
            ev = epilogue_vecs()
            for rows in step_rows():
                epilogue(rows, oth, ev)


def _split_cast_kernel(a_ref, b_ref, wa_ref, wb_ref, tail_s, *, f, tb, nb):
    s = pl.program_id(0)
    rem = f % tb
    lane = lax.broadcasted_iota(jnp.int32, a_ref.shape, 1)
    a_blk = jnp.minimum(s, nb - 1)
    wa_ref[...] = jnp.where(a_blk * tb + lane < f, a_ref[...], 0.0).astype(BF16)

    @pl.when(s == 0)
    def _():
        wb_ref[...] = jnp.zeros_like(wb_ref)

    @pl.when(s > 0)
    def _():
        b = jnp.concatenate([tail_s[...], b_ref[:, :rem]], axis=1) if rem else tail_s[...]
        wb_ref[...] = jnp.where((s - 1) * tb + lane < f, b, 0.0).astype(BF16)

    tail_s[...] = b_ref[:, rem:]


def _split_cast_call(w_in, f_pad, tb=256):
    d, f2 = w_in.shape
    f = f2 // 2
    assert f2 % tb == 0 and f_pad % tb == 0 and (f % tb) % LANE == 0
    nb = f_pad // tb
    last = f2 // tb - 1
    return pl.pallas_call(
        functools.partial(_split_cast_kernel, f=f, tb=tb, nb=nb),
        grid=(nb + 1,),
        in_specs=[pl.BlockSpec((d, tb), lambda s: (0, jnp.minimum(s, nb - 1))),
                  pl.BlockSpec((d, tb), lambda s: (0, jnp.minimum(f // tb + s, last)))],
        out_specs=[pl.BlockSpec((d, tb), lambda s: (0, jnp.minimum(s, nb - 1))),
                   pl.BlockSpec((d, tb), lambda s: (0, jnp.maximum(s - 1, 0)))],
        out_shape=[jax.ShapeDtypeStruct((d, f_pad), BF16)] * 2,
        scratch_shapes=[pltpu.VMEM((d, tb - f % tb), F32)],
        compiler_params=_cparams(("arbitrary",)),
        name="ffn_weight_cast",
    )(w_in, w_in)


def _prep_ffn_weights(w_in, w_out, tf):
    f = w_in.shape[1] // 2
    pad = -f % tf
    wa, wb = _split_cast_call(w_in, f + pad)
    wo = jnp.pad(w_out.astype(BF16), ((0, pad), (0, 0)))
    return wa, wb, wo


def _ffn_call(x2d, m3, g, w3, g2, *, mode, mod_base, rows_per_mod, mod_offset, tm, tf):
    r, d = x2d.shape
    wa, wb, wo = w3
    nf = wo.shape[0] // tf
    n = r // tm
    tiles_per_mod = rows_per_mod // tm
    kern = functools.partial(_ffn_kernel, mode=mode, mod_base=mod_base, nf=nf, n_tiles=n)

    def nxt(i):
        return jnp.minimum(i + 1, n - 1)

    def prv(i):
        return jnp.maximum(i - 1, 0)

    def wblk(i, j):
        return jnp.where(i < n, j, nf - 1)

    def mod_spec(tile):
        return pl.BlockSpec((1, N_MOD, d), lambda i, j: (mod_offset + tile(i) // tiles_per_mod, 0, 0))

    in_specs = [pl.BlockSpec((tm, d), lambda i, j: (jnp.where(j < _next_tile_step(nf), jnp.minimum(i, n - 1), nxt(i)), 0)),
                pl.BlockSpec((tm, d), lambda i, j: (prv(i), 0)),
                mod_spec(nxt), mod_spec(prv),
                pl.BlockSpec((1, d), lambda i, j: (0, 0)),
                pl.BlockSpec((d, tf), lambda i, j: (0, wblk(i, j))),
                pl.BlockSpec((d, tf), lambda i, j: (0, wblk(i, j))),
                pl.BlockSpec((tf, d), lambda i, j: (wblk(i, j), 0)),
                pl.BlockSpec((1, d), lambda i, j: (0, 0))]
    args = [x2d, x2d, m3, m3, g, wa, wb, wo, g2]
    o_spec = pl.BlockSpec((tm, d), lambda i, j: (prv(i), 0))
    o_f32 = (jax.ShapeDtypeStruct((r, d), F32), o_spec)
    o_bf = (jax.ShapeDtypeStruct((r, d), BF16), o_spec)
    outs = {"ffn2": [o_f32], "ffn1": [o_f32, o_bf], "ffn1ctx": [o_bf]}[mode]
    return pl.pallas_call(
        kern,
        grid=(n + 1, nf),
        in_specs=in_specs,
        out_specs=[o[1] for o in outs],
        out_shape=[o[0] for o in outs],
        scratch_shapes=[pltpu.VMEM((tm, d), BF16)] * 2 + [pltpu.VMEM((tm, d), F32)] * 2,
        compiler_params=_cparams(("arbitrary", "arbitrary")),
        name=mode,
    )(*args)


def _inproj_kernel(*refs, tm, seq_tiles, with_mlp):
    if with_mlp:
        (h_ref, hp_ref, hn_ref, w_ref, wab_ref, cw_ref, sw_ref, sb_ref, mng_ref,
         qkv_ref, ab_ref, gate_ref, ob_ref) = refs
    else:
        h_ref, hp_ref, hn_ref, w_ref, wab_ref, cw_ref, qkv_ref, ab_ref = refs
    cb = PROJ_CB
    h = h_ref[...]

    def block(jb):
        return _dot(h, w_ref[:, jb * cb:(jb + 1) * cb])

    ab_ref[...] = _dot(h, wab_ref[...])

    i = pl.program_id(0)
    pos = i % seq_tiles
    keep_prev = jnp.where(pos > 0, 1.0, 0.0).astype(BF16)
    keep_next = jnp.where(pos < seq_tiles - 1, 1.0, 0.0).astype(BF16)
    hext = jnp.concatenate([hp_ref[...] * keep_prev, h, hn_ref[...] * keep_next], axis=0)
    first = HALO - CONV_K // 2
    for jb in range(3):
        zext = _dot(hext, w_ref[:, jb * cb:(jb + 1) * cb])
        for rc in range(tm // DN_CHUNK):
            win = zext[rc * DN_CHUNK:rc * DN_CHUNK + DN_CHUNK + 2 * HALO]
            y = cw_ref[0:1, jb * cb:(jb + 1) * cb] * win[first:first + DN_CHUNK]
            for t in range(1, CONV_K):
                y = y + cw_ref[t:t + 1, jb * cb:(jb + 1) * cb] * win[first + t:first + t + DN_CHUNK]
            y = _silu(y)
            if jb < 2:
                parts = []
                for hd in range(cb // DN_HEAD_DIM):
                    yh = y[:, hd * DN_HEAD_DIM:(hd + 1) * DN_HEAD_DIM]
                    yh = yh * lax.rsqrt(jnp.sum(yh * yh, axis=-1, keepdims=True) + EPS)
                    parts.append(yh * (DN_HEAD_DIM ** -0.5) if jb == 0 else yh)
                y = jnp.concatenate(parts, axis=1)
            qkv_ref[rc * DN_CHUNK:(rc + 1) * DN_CHUNK, jb * cb:(jb + 1) * cb] = y.astype(BF16)
    if not with_mlp:
        return
    gate_ref[...] = block(3).astype(BF16)
    u = jax.nn.gelu(block(4))
    v = jax.nn.gelu(block(5))
    for g in range(MLP_GROUPS):
        cols = slice(g * LANE, (g + 1) * LANE)
        vn = _rms(v[:, cols], mng_ref[:, cols]).astype(BF16)
        for ci in range(tm // MLP_CHUNK):
            rows = slice(ci * MLP_CHUNK, (ci + 1) * MLP_CHUNK)
            s = _dot(sw_ref[g], vn[rows, :]) + sb_ref[g]
            ob_ref[rows, cols] = (u[rows, cols] * s).astype(BF16)


def _prep_inproj_weights(w_in, hg):
    wq = DN_HEADS * DN_HEAD_DIM
    n_ab = 4 * DN_HEADS
    w_main = jnp.concatenate([w_in[:, :4 * wq].astype(BF16), w_in[:, 4 * wq + n_ab:].astype(BF16)], axis=1)
    w = w_in[:, 4 * wq:4 * wq + n_ab].astype(BF16)
    wa = w[:, :2 * DN_HEADS].reshape(-1, 2, DN_HEADS // hg, hg)
    wb = w[:, 2 * DN_HEADS:].reshape(-1, 2, DN_HEADS // hg, hg)
    grp = jnp.concatenate([jnp.moveaxis(wa, 2, 1), jnp.moveaxis(wb, 2, 1)], axis=2)
    grp = grp.reshape(w.shape[0], DN_HEADS // hg, 4 * hg)
    w_ab = jnp.pad(grp, ((0, 0), (0, 0), (0, LANE - 4 * hg))).reshape(w.shape[0], -1)
    return w_main, w_ab


def _inproj_call(h2, w_main, w_ab, conv_w, sw, sb, mng, *, tm, seq_len, with_mlp):
    n, d = h2.shape
    cb = PROJ_CB
    nab = w_ab.shape[1]
    ncb = 6 if with_mlp else 3
    assert seq_len % tm == 0 and tm % HALO == 0
    kern = functools.partial(_inproj_kernel, tm=tm, seq_tiles=seq_len // tm, with_mlp=with_mlp)
    once = dict(pipeline_mode=pl.Buffered(1))
    hb = tm // HALO
    in_specs = [pl.BlockSpec((tm, d), lambda i: (i, 0)),
                pl.BlockSpec((HALO, d), lambda i: (jnp.maximum(i * hb - 1, 0), 0)),
                pl.BlockSpec((HALO, d), lambda i: (jnp.minimum((i + 1) * hb, n // HALO - 1), 0)),
                pl.BlockSpec((d, ncb * cb), lambda i: (0, 0), **once),
                pl.BlockSpec((d, nab), lambda i: (0, 0), **once),
                pl.BlockSpec((CONV_K, 3 * cb), lambda i: (0, 0), **once)]
    out_specs = [pl.BlockSpec((tm, 3 * cb), lambda i: (i, 0)),
                 pl.BlockSpec((tm, nab), lambda i: (i, 0))]
    out_shape = [jax.ShapeDtypeStruct((n, 3 * cb), BF16), jax.ShapeDtypeStruct((n, nab), F32)]
    args = [h2, h2, h2, w_main, w_ab, conv_w]
    if with_mlp:
        in_specs += [pl.BlockSpec((MLP_GROUPS, MLP_CHUNK, MLP_CHUNK), lambda i: (0, 0, 0), **once),
                     pl.BlockSpec((MLP_GROUPS, MLP_CHUNK, LANE), lambda i: (0, 0, 0), **once),
                     pl.BlockSpec((1, cb), lambda i: (0, 0))]
        out_specs += [pl.BlockSpec((tm, cb), lambda i: (i, 0))] * 2
        out_shape += [jax.ShapeDtypeStruct((n, cb), BF16)] * 2
        args += [sw, sb, mng]
    return pl.pallas_call(
        kern,
        grid=(n // tm,),
        in_specs=in_specs,
        out_specs=out_specs,
        out_shape=out_shape,
        compiler_params=_cparams(("parallel",)),
        name="inproj" if with_mlp else "inproj_ctx",
    )(*args)


def _tri_inverse(lmats, uppers, ii, jj):
    eye = (ii == jj).astype(F32)
    b16 = (ii // 16) == (jj // 16)
    b32 = (ii // 32) == (jj // 32)
    m1 = {False: b32 & ((ii // 16) > (jj // 16)), True: b32 & ((ii // 16) < (jj // 16))}
    m2 = {False: (ii // 32) > (jj // 32), True: (ii // 32) < (jj // 32)}
    a1 = [jnp.where(b16, l, 0.0) for l in lmats]
    a1b = [a.astype(BF16) for a in a1]
    a2b = [_dot(a, a).astype(BF16) for a in a1b]
    yield
    a4b = [_dot(a, a).astype(BF16) for a in a2b]
    yield
    a8b = [_dot(a, a).astype(BF16) for a in a4b]
    yield
    p = [eye - a for a in a1]
    for ab in (a2b, a4b, a8b):
        p = [pi + _dot(pi.astype(BF16), a) for pi, a in zip(p, ab)]
        yield
    for masks in (m1, m2):
        pb = [pi.astype(BF16) for pi in p]
        xm = [_dot(jnp.where(masks[u], l, 0.0).astype(BF16), b).astype(BF16)
              for l, u, b in zip(lmats, uppers, pb)]
        yield
        p = [pi - _dot(b, x) for pi, b, x in zip(p, pb, xm)]
        yield
    return p


def _run_interleaved(*gens):
    live = list(gens)
    while live:
        for g in list(live):
            try:
                next(g)
            except StopIteration:
                live.remove(g)


def _dn_kernel(qx_ref, kx_ref, vx_ref, qc_ref, kc_ref, vc_ref, abx_ref, abc_ref, gate_ref,
               alog_ref, dtb_ref, hng_ref,
               out_ref,
               col_s, row_s, st_s, o_s, pu_s, pwq_s, pkt_s, pqk_s, pgl_s,
               *, hg, tx, tc, unroll):
    C = DN_CHUNK
    D = DN_HEAD_DIM
    ncc, ncx = tc // C, tx // C
    nch = ncc + ncx
    nchain = 2 * hg

    ii = lax.broadcasted_iota(jnp.int32, (C, C), 0)
    jj = lax.broadcasted_iota(jnp.int32, (C, C), 1)
    lane_t = lax.broadcasted_iota(jnp.int32, (C, LANE), 1)

    tril = (ii >= jj).astype(F32)
    triu = (ii <= jj).astype(F32)
    ones = jnp.ones((C, C), F32)
    sum_mats = jnp.concatenate([tril, triu, ones], axis=0).astype(BF16)
    neg_a = -jnp.exp(alog_ref[0])
    dtb = dtb_ref[0]

    def gate_phase(ab_ref, n_ch, chunk0):
        def body(n, carry):
            t = ab_ref[0, pl.ds(pl.multiple_of(n * C, C), C), :]
            g = neg_a * jax.nn.softplus(t + dtb)
            g = jnp.where(lane_t < nchain, g, 0.0)
            hi = g.astype(BF16)
            r1 = g - hi.astype(F32)
            mid = r1.astype(BF16)
            lo = (r1 - mid.astype(F32)).astype(BF16)
            cs = _dot(sum_mats, jnp.concatenate([hi, mid, lo], axis=1))
            cs = cs[:, :LANE] + cs[:, LANE:2 * LANE] + cs[:, 2 * LANE:]
            gt = cs[2 * C:]
            gc = jnp.where(lane_t < hg, cs[:C], cs[C:2 * C])
            beta = pltpu.roll(jax.nn.sigmoid(t), LANE - nchain, 1)
            col_s[chunk0 + n, 0] = gc
            col_s[chunk0 + n, 1] = beta
            col_s[chunk0 + n, 2] = gt
            row_s[chunk0 + n] = gc.T[0:16, :]
            return carry

        lax.fori_loop(0, n_ch, body, 0, unroll=4)

    gate_phase(abc_ref, ncc, 0)
    gate_phase(abx_ref, ncx, ncc)

    def chunk_of(s):
        cf = s
        cb = jnp.where(s < ncc, ncc - 1 - s, nch + ncc - 1 - s)
        return cf, cb

    def bcast(tile, r):
        return jnp.broadcast_to(tile[:, r:r + 1], (C, LANE))

    def stage1(it, par, in_context):
        q_ref, k_ref, v_ref, c0 = (qc_ref, kc_ref, vc_ref, 0) if in_context else (qx_ref, kx_ref, vx_ref, ncc)
        jobs = []
        for u in range(unroll):
            cf, cb = chunk_of(it * unroll + u)
            for d in range(2):
                c = cf if d == 0 else cb
                gc_t = col_s[c, 0]
                be_t = col_s[c, 1]
                gt_t = col_s[c, 2]
                e1_t = jnp.exp(gc_t)
                tiles = dict(gc=gc_t, be=be_t, e1=e1_t, ca=be_t * e1_t, e2=jnp.exp(gt_t - gc_t),
                             gl=jnp.exp(gt_t), rows=row_s[c])
                for hl in range(hg):
                    jobs.append((par * unroll + u, d * hg + hl, pl.multiple_of((c - c0) * C, C), tiles, d, hl))
        tri = {0: ii >= jj, 1: ii <= jj}
        strict = {0: ii > jj, 1: ii < jj}

        def rd(ref, job):
            return ref[0, pl.ds(job[2], C), job[5] * D:(job[5] + 1) * D]

        qn = [rd(q_ref, j) for j in jobs]
        kn = [rd(k_ref, j) for j in jobs]
        gmat = [_dot_nt(jnp.concatenate([q, k], axis=0), k) for q, k in zip(qn, kn)]
        yield
        dec =[jnp.exp(jnp.where(tri[j[4]], bcast(j[3]["gc"], j[1])[:, :C] - j[3]["rows"][j[1]:j[1] + 1, :], -1e30))
               for j in jobs]
        lmat = [jnp.where(strict[j[4]], g[C:] * dc, 0.0) * bcast(j[3]["be"], j[1])[:, :C]
                for j, g, dc in zip(jobs, gmat, dec)]
        for j, g, dc in zip(jobs, gmat, dec):
            pqk_s[j[0], j[1]] = (g[:C] * dc).astype(BF16)
        tinv = yield from _tri_inverse(lmat, [bool(j[4]) for j in jobs], ii, jj)
        knf = [k.astype(F32) for k in kn]
        rhs = [jnp.concatenate([(rd(v_ref, j).astype(F32) * bcast(j[3]["be"], j[1])).astype(BF16),
                                (kf * bcast(j[3]["ca"], j[1])).astype(BF16)], axis=1)
               for j, kf in zip(jobs, knf)]
        uw = [_dot(t.astype(BF16), r) for t, r in zip(tinv, rhs)]
        yield
        for j, x, q, kf in zip(jobs, uw, qn, knf):
            slot, r = j[0], j[1]
            qd = (q.astype(F32) * bcast(j[3]["e1"], r)).astype(BF16)
            pu_s[slot, r] = x[:, :D].astype(BF16)
            pwq_s[slot, r] = jnp.concatenate([x[:, D:].astype(BF16), qd], axis=0)
            pkt_s[slot, r] = (kf * bcast(j[3]["e2"], r)).T.astype(BF16)
            pgl_s[slot, r] = jnp.broadcast_to(j[3]["gl"][0:8, r:r + 1], (8, LANE))

    def stage2(it, par):
        for u in range(unroll):
            slot = par * unroll + u
            cf, cb = chunk_of(it * unroll + u)
            chains = [(d * hg + hl, pl.multiple_of((cb if d else cf) * C, C), hl)
                      for d in range(2) for hl in range(hg)]
            st = [st_s[r] for r, _, _ in chains]
            ws = [_dot(pwq_s[slot, r], s.astype(BF16)) for (r, _, _), s in zip(chains, st)]
            yield
            vn = [(pu_s[slot, r].astype(F32) - w[:C]).astype(BF16) for (r, _, _), w in zip(chains, ws)]
            o = [w[C:] + _dot(pqk_s[slot, r], v) for (r, _, _), w, v in zip(chains, ws, vn)]
            kv = [_dot(pkt_s[slot, r], v) for (r, _, _), v in zip(chains, vn)]
            yield
            for (r, row0, hl), s, x, y in zip(chains, st, kv, o):
                st_s[r] = s * jnp.concatenate([pgl_s[slot, r]] * (D // 8), axis=0) + x
                o_s[pl.ds(row0, C), hl * D:(hl + 1) * D] += y

    st_s[...] = jnp.zeros_like(st_s)
    o_s[...] = jnp.zeros_like(o_s)
    _run_interleaved(stage1(0, 0, True))

    def loop_body(it, carry):
        _run_interleaved(stage1(it, it % 2, False), stage2(it - 1, (it - 1) % 2))
        return carry

    n_it = nch // unroll
    lax.fori_loop(1, n_it, loop_body, 0)
    _run_interleaved(stage2(n_it - 1, (n_it - 1) % 2))

    hn = hng_ref[...]
    for hl in range(hg):
        cols = slice(hl * D, (hl + 1) * D)
        o = o_s[tc:tc + tx, cols]
        zg = gate_ref[0, :, cols].astype(F32)
        out_ref[0, :, cols] = (_rms(o, hn) * _silu(zg)).astype(BF16)


def _dn_call(qkv_x, qkv_c, ab_x, ab_c, gate, alog_g, dtb_g, hng, *, hg, unroll):
    b, tx, w3 = qkv_x.shape
    tc = qkv_c.shape[1]
    assert tc // DN_CHUNK == unroll and (tx // DN_CHUNK) % unroll == 0
    nslot = 2 * unroll
    nhg = DN_HEADS // hg
    wd = hg * DN_HEAD_DIM
    nch = (tx + tc) // DN_CHUNK
    nchain = 2 * hg
    kern = functools.partial(_dn_kernel, hg=hg, tx=tx, tc=tc, unroll=unroll)

    def col(t, off):
        return pl.BlockSpec((1, t, wd), lambda bi, gi: (bi, 0, off * nhg + gi))

    def ab(t):
        return pl.BlockSpec((1, t, LANE), lambda bi, gi: (bi, 0, gi), pipeline_mode=pl.Buffered(1))

    small = pl.BlockSpec((1, 1, LANE), lambda bi, gi: (gi, 0, 0))
    return pl.pallas_call(
        kern,
        grid=(b, nhg),
        in_specs=[col(tx, 0), col(tx, 1), col(tx, 2), col(tc, 0), col(tc, 1), col(tc, 2),
                  ab(tx), ab(tc),
                  pl.BlockSpec((1, tx, wd), lambda bi, gi: (bi, 0, gi), pipeline_mode=pl.Buffered(1)),
                  small, small,
                  pl.BlockSpec((1, LANE), lambda bi, gi: (0, 0))],
        out_specs=pl.BlockSpec((1, tx, wd), lambda bi, gi: (bi, 0, gi)),
        out_shape=jax.ShapeDtypeStruct((b, tx, DN_HEADS * DN_HEAD_DIM), BF16),
        scratch_shapes=[
            pltpu.VMEM((nch, 3, DN_CHUNK, LANE), F32),
            pltpu.VMEM((nch, 16, DN_CHUNK), F32),
            pltpu.VMEM((nchain, DN_HEAD_DIM, DN_HEAD_DIM), F32),
            pltpu.VMEM((tc + tx, wd), F32),
            pltpu.VMEM((nslot, nchain, DN_CHUNK, DN_HEAD_DIM), BF16),
            pltpu.VMEM((nslot, nchain, 2 * DN_CHUNK, DN_HEAD_DIM), BF16),
            pltpu.VMEM((nslot, nchain, DN_HEAD_DIM, DN_CHUNK), BF16),
            pltpu.VMEM((nslot, nchain, DN_CHUNK, DN_CHUNK), BF16),
            pltpu.VMEM((nslot, nchain, 8, LANE), F32)],
        compiler_params=_cparams(("parallel", "arbitrary")),
        name="deltanet",
    )(qkv_x, qkv_x, qkv_x, qkv_c, qkv_c, qkv_c, ab_x, ab_c, gate, alog_g, dtb_g, hng)


def _outproj_kernel(x_ref, oa_ref, ob_ref, w_ref, m_ref, out_ref):
    half = oa_ref.shape[1]
    y = _dot(oa_ref[...], w_ref[:half, :]) + _dot(ob_ref[...], w_ref[half:, :])
    out_ref[...] = x_ref[...] + m_ref[0, 5:6, :] * y


def _outproj_call(x2d, oa, ob, w, m3, *, rows_per_mod, tm):
    n, d = x2d.shape
    half = oa.shape[1]
    tiles_per_mod = rows_per_mod // tm
    return pl.pallas_call(
        _outproj_kernel,
        grid=(n // tm,),
        in_specs=[pl.BlockSpec((tm, d), lambda i: (i, 0)),
                  pl.BlockSpec((tm, half), lambda i: (i, 0)),
                  pl.BlockSpec((tm, half), lambda i: (i, 0)),
                  pl.BlockSpec((2 * half, d), lambda i: (0, 0), pipeline_mode=pl.Buffered(1)),
                  pl.BlockSpec((1, N_MOD, d), lambda i: (i // tiles_per_mod, 0, 0))],
        out_specs=pl.BlockSpec((tm, d), lambda i: (i, 0)),
        out_shape=jax.ShapeDtypeStruct((n, d), F32),
        compiler_params=_cparams(("parallel",)),
        name="outproj",
    )(x2d, oa, ob, w, m3)


def _chain_rows(p, hg):
    nhg = DN_HEADS // hg
    rows = jnp.moveaxis(p.reshape(2, nhg, hg), 1, 0).reshape(nhg, 1, 2 * hg)
    return jnp.pad(rows, ((0, 0), (0, 0), (0, LANE - 2 * hg)))


FFN_TM = 512
FFN_TF = 512
PROJ_TM = 512
OUT_TM = 512
DN_HG = 4
DN_UNROLL = 4
MOD_ROWS = 16


def kernel(x, c, ctx, c_ctx, w_mod, b_mod, norm_g, ffn1_w_in, ffn1_w_out, w_in, conv_w, a_log, dt_bias, head_norm_g, spatial_w, spatial_b, mlp_norm_g, w_out, ffn2_w_in, ffn2_w_out, final_g):
    bsz, t, d = x.shape
    tc = ctx.shape[1]
    assert w_mod.shape[0] == 1, "single-layer block"
    assert bsz + 1 <= MOD_ROWS

    c_all = jnp.concatenate([c, c_ctx[None, :], jnp.zeros((MOD_ROWS - bsz - 1, d), F32)], axis=0)
    m3 = _mod_call(c_all, w_mod[0], b_mod).reshape(MOD_ROWS, N_MOD, d)

    g0, g1, g2 = norm_g[0, 0:1], norm_g[0, 1:2], norm_g[0, 2:3]
    w_ffn1 = _prep_ffn_weights(ffn1_w_in[0], ffn1_w_out[0], FFN_TF)
    w_ffn2 = _prep_ffn_weights(ffn2_w_in[0], ffn2_w_out[0], FFN_TF)

    x1, h_x = _ffn_call(x.reshape(bsz * t, d), m3, g0, w_ffn1, g1, mode="ffn1", mod_base=0,
                        rows_per_mod=t, mod_offset=0, tm=FFN_TM, tf=FFN_TF)
    (h_c,) = _ffn_call(ctx.reshape(bsz * tc, d), m3, g0, w_ffn1, g1, mode="ffn1ctx", mod_base=0,
                       rows_per_mod=bsz * tc, mod_offset=bsz, tm=FFN_TM, tf=FFN_TF)

    w_main, w_ab = _prep_inproj_weights(w_in[0], DN_HG)
    sb = jnp.broadcast_to(spatial_b[0][:, :, None], (MLP_GROUPS, MLP_CHUNK, LANE))
    qkv_x, ab_x, gate_x, o_b = _inproj_call(h_x, w_main, w_ab, conv_w[0], spatial_w[0].astype(BF16), sb, mlp_norm_g,
                                            tm=PROJ_TM, seq_len=t, with_mlp=True)
    qkv_c, ab_c = _inproj_call(h_c, w_main, w_ab, conv_w[0], None, None, None,
                               tm=min(PROJ_TM, tc), seq_len=tc, with_mlp=False)

    wq = DN_HEADS * DN_HEAD_DIM
    o_a = _dn_call(qkv_x.reshape(bsz, t, 3 * wq), qkv_c.reshape(bsz, tc, 3 * wq),
                   ab_x.reshape(bsz, t, -1), ab_c.reshape(bsz, tc, -1), gate_x.reshape(bsz, t, wq),
                   _chain_rows(a_log[0], DN_HG), _chain_rows(dt_bias[0], DN_HG), head_norm_g,
                   hg=DN_HG, unroll=DN_UNROLL)

    x2 = _outproj_call(x1, o_a.reshape(bsz * t, wq), o_b, w_out[0].astype(BF16), m3, rows_per_mod=t, tm=OUT_TM)
    (out,) = _ffn_call(x2, m3, g2, w_ffn2, final_g[None, :], mode="ffn2", mod_base=6,
                       rows_per_mod=t, mod_offset=0, tm=FFN_TM, tf=FFN_TF)
    return out.reshape(bsz, t, d)
```

```python
import functools

import jax
import jax.numpy as jnp
import numpy as np
from jax import lax
from jax.experimental import pallas as pl
from jax.experimental.pallas import tpu as pltpu

F32 = jnp.float32
BF16 = jnp.bfloat16

EPS = 1e-6
LANE = 128
N_MOD = 9
DN_HEADS = 8
DN_HEAD_DIM = 128
DN_CHUNK = 64
CONV_K = 5
MLP_GROUPS = 8
MLP_CHUNK = 128
PROJ_CB = 1024
HALO = 16
V7X_VMEM_LIMIT = 60000 * 1024
ROW_CHUNK = 16
ROW_UNROLL = 8


def _cparams(sem):
    return pltpu.CompilerParams(dimension_semantics=sem, vmem_limit_bytes=V7X_VMEM_LIMIT)


def _rms(x, g):
    return x * lax.rsqrt(jnp.mean(x * x, axis=-1, keepdims=True) + EPS) * g


def _silu(x):
    return x * jax.nn.sigmoid(x)


def _dot(a, b):
    return jnp.dot(a, b, preferred_element_type=F32)


def _dot_nt(a, b):
    return lax.dot_general(a, b, (((1,), (1,)), ((), ())), preferred_element_type=F32)


def _mod_kernel(c_ref, w_ref, b_ref, o_ref):
    s = _silu(c_ref[...]).astype(BF16)
    o_ref[...] = _dot(s, w_ref[...].astype(BF16)) + b_ref[...]


def _mod_call(c_all, w_mod, b_mod, tn=1024):
    rows, d = c_all.shape
    n = w_mod.shape[1]
    return pl.pallas_call(
        _mod_kernel,
        grid=(n // tn,),
        in_specs=[pl.BlockSpec((rows, d), lambda j: (0, 0)),
                  pl.BlockSpec((d, tn), lambda j: (0, j)),
                  pl.BlockSpec((1, tn), lambda j: (0, j))],
        out_specs=pl.BlockSpec((rows, tn), lambda j: (0, j)),
        out_shape=jax.ShapeDtypeStruct((rows, n), F32),
        compiler_params=_cparams(("arbitrary",)),
        name="mod",
    )(c_all, w_mod, b_mod)


def _ffn_kernel(*refs, mode, mod_base, nf):
    if mode == "ffn2":
        x_ref, m_ref, g_ref, wa_ref, wb_ref, wo_ref, g2_ref, out_ref, h_s, acc_s, vec_s = refs
    elif mode == "ffn1":
        x_ref, m_ref, g_ref, wa_ref, wb_ref, wo_ref, g2_ref, out_ref, h2_ref, h_s, acc_s, vec_s = refs
    else:
        x_ref, m_ref, g_ref, wa_ref, wb_ref, wo_ref, g2_ref, h2_ref, h_s, acc_s, vec_s = refs
    j = pl.program_id(1)
    tm, d = h_s.shape

    def mvec(k):
        return m_ref[0, k:k + 1, :]

    def tile_rows(v):
        return jnp.concatenate([v] * (ROW_CHUNK // 8), axis=0)

    def norm_scale(x, gain, shift=None):
        y = x * lax.rsqrt(jnp.mean(x * x, axis=-1, keepdims=True) + EPS) * tile_rows(gain)
        return y if shift is None else y + tile_rows(shift)

    def row_loop(fn):
        def body(r, carry):
            fn(pl.ds(pl.multiple_of(r * ROW_CHUNK, ROW_CHUNK), ROW_CHUNK))
            return carry
        lax.fori_loop(0, tm // ROW_CHUNK, body, 0, unroll=ROW_UNROLL)

    @pl.when(j == 0)
    def _():
        def put(k, v):
            vec_s[k] = jnp.broadcast_to(v, (8, d))
        put(0, g_ref[...] * (1.0 + mvec(mod_base + 1)))
        put(1, mvec(mod_base))
        put(2, 0.5 * mvec(mod_base + 2))
        if mode == "ffn2":
            put(3, g2_ref[...])
        else:
            put(3, g2_ref[...] * (1.0 + mvec(mod_base + 4)))
            put(4, mvec(mod_base + 3))

        def prologue(rows):
            h_s[rows, :] = norm_scale(x_ref[rows, :], vec_s[0], vec_s[1]).astype(BF16)
            acc_s[rows, :] = jnp.zeros((ROW_CHUNK, d), F32)

        row_loop(prologue)

    h = h_s[...]
    act = (_silu(_dot(h, wa_ref[...])) * _dot(h, wb_ref[...])).astype(BF16)
    acc_s[...] += _dot(act, wo_ref[...])

    @pl.when(j == nf - 1)
    def _():
        def epilogue(rows):
            xn = x_ref[rows, :] + tile_rows(vec_s[2]) * acc_s[rows, :]
            if mode == "ffn2":
                out_ref[rows, :] = norm_scale(xn, vec_s[3])
            else:
                if mode == "ffn1":
                    out_ref[rows, :] = xn
                h2_ref[rows, :] = norm_scale(xn, vec_s[3], vec_s[4]).astype(BF16)

        row_loop(epilogue)


def _split_cast_kernel(a_ref, b_ref, wa_ref, wb_ref, tail_s, *, f, tb, nb):
    s = pl.program_id(0)
    rem = f % tb
    lane = lax.broadcasted_iota(jnp.int32, a_ref.shape, 1)
    a_blk = jnp.minimum(s, nb - 1)
    wa_ref[...] = jnp.where(a_blk * tb + lane < f, a_ref[...], 0.0).astype(BF16)

    @pl.when(s == 0)
    def _():
        wb_ref[...] = jnp.zeros_like(wb_ref)

    @pl.when(s > 0)
    def _():
        b = jnp.concatenate([tail_s[...], b_ref[:, :rem]], axis=1) if rem else tail_s[...]
        wb_ref[...] = jnp.where((s - 1) * tb + lane < f, b, 0.0).astype(BF16)

    tail_s[...] = b_ref[:, rem:]


def _split_cast_call(w_in, f_pad, tb=256):
    d, f2 = w_in.shape
    f = f2 // 2
    assert f2 % tb == 0 and f_pad % tb == 0 and (f % tb) % LANE == 0
    nb = f_pad // tb
    last = f2 // tb - 1
    return pl.pallas_call(
        functools.partial(_split_cast_kernel, f=f, tb=tb, nb=nb),
        grid=(nb + 1,),
        in_specs=[pl.BlockSpec((d, tb), lambda s: (0, jnp.minimum(s, nb - 1))),
                  pl.BlockSpec((d, tb), lambda s: (0, jnp.minimum(f // tb + s, last)))],
        out_specs=[pl.BlockSpec((d, tb), lambda s: (0, jnp.minimum(s, nb - 1))),
                   pl.BlockSpec((d, tb), lambda s: (0, jnp.maximum(s - 1, 0)))],
        out_shape=[jax.ShapeDtypeStruct((d, f_pad), BF16)] * 2,
        scratch_shapes=[pltpu.VMEM((d, tb - f % tb), F32)],
        compiler_params=_cparams(("arbitrary",)),
        name="ffn_weight_cast",
    )(w_in, w_in)


def _prep_ffn_weights(w_in, w_out, tf):
    f = w_in.shape[1] // 2
    pad = -f % tf
    wa, wb = _split_cast_call(w_in, f + pad)
    wo = jnp.pad(w_out.astype(BF16), ((0, pad), (0, 0)))
    return wa, wb, wo


def _ffn_call(x2d, m3, g, w3, g2, *, mode, mod_base, rows_per_mod, mod_offset, tm, tf):
    r, d = x2d.shape
    wa, wb, wo = w3
    nf = wo.shape[0] // tf
    tiles_per_mod = rows_per_mod // tm
    kern = functools.partial(_ffn_kernel, mode=mode, mod_base=mod_base, nf=nf)
    x_spec = pl.BlockSpec((tm, d), lambda i, j: (i, 0))
    in_specs = [x_spec,
                pl.BlockSpec((1, N_MOD, d), lambda i, j: (mod_offset + i // tiles_per_mod, 0, 0)),
                pl.BlockSpec((1, d), lambda i, j: (0, 0)),
                pl.BlockSpec((d, tf), lambda i, j: (0, j)),
                pl.BlockSpec((d, tf), lambda i, j: (0, j)),
                pl.BlockSpec((tf, d), lambda i, j: (j, 0)),
                pl.BlockSpec((1, d), lambda i, j: (0, 0))]
    args = [x2d, m3, g, wa, wb, wo, g2]
    o_f32 = (jax.ShapeDtypeStruct((r, d), F32), x_spec)
    o_bf = (jax.ShapeDtypeStruct((r, d), BF16), x_spec)
    outs = {"ffn2": [o_f32], "ffn1": [o_f32, o_bf], "ffn1ctx": [o_bf]}[mode]
    return pl.pallas_call(
        kern,
        grid=(r // tm, nf),
        in_specs=in_specs,
        out_specs=[o[1] for o in outs],
        out_shape=[o[0] for o in outs],
        scratch_shapes=[pltpu.VMEM((tm, d), BF16), pltpu.VMEM((tm, d), F32), pltpu.VMEM((5, 8, d), F32)],
        compiler_params=_cparams(("parallel", "arbitrary")),
        name=mode,
    )(*args)


def _inproj_kernel(*refs, tm, seq_tiles, with_mlp):
    if with_mlp:
        (h_ref, hp_ref, hn_ref, w_ref, wab_ref, cw_ref, sw_ref, sb_ref, mng_ref,
         qkv_ref, ab_ref, gate_ref, ob_ref) = refs
    else:
        h_ref, hp_ref, hn_ref, w_ref, wab_ref, cw_ref, qkv_ref, ab_ref = refs
    cb = PROJ_CB
    h = h_ref[...]

    def block(jb):
        return _dot(h, w_ref[:, jb * cb:(jb + 1) * cb])

    ab_ref[...] = _dot(h, wab_ref[...])

    i = pl.program_id(0)
    pos = i % seq_tiles
    keep_prev = jnp.where(pos > 0, 1.0, 0.0).astype(BF16)
    keep_next = jnp.where(pos < seq_tiles - 1, 1.0, 0.0).astype(BF16)
    hext = jnp.concatenate([hp_ref[...] * keep_prev, h, hn_ref[...] * keep_next], axis=0)
    first = HALO - CONV_K // 2
    for jb in range(3):
        zext = _dot(hext, w_ref[:, jb * cb:(jb + 1) * cb])
        for rc in range(tm // DN_CHUNK):
            win = zext[rc * DN_CHUNK:rc * DN_CHUNK + DN_CHUNK + 2 * HALO]
            y = cw_ref[0:1, jb * cb:(jb + 1) * cb] * win[first:first + DN_CHUNK]
            for t in range(1, CONV_K):
                y = y + cw_ref[t:t + 1, jb * cb:(jb + 1) * cb] * win[first + t:first + t + DN_CHUNK]
            y = _silu(y)
            if jb < 2:
                parts = []
                for hd in range(cb // DN_HEAD_DIM):
                    yh = y[:, hd * DN_HEAD_DIM:(hd + 1) * DN_HEAD_DIM]
                    yh = yh * lax.rsqrt(jnp.sum(yh * yh, axis=-1, keepdims=True) + EPS)
                    parts.append(yh * (DN_HEAD_DIM ** -0.5) if jb == 0 else yh)
                y = jnp.concatenate(parts, axis=1)
            qkv_ref[rc * DN_CHUNK:(rc + 1) * DN_CHUNK, jb * cb:(jb + 1) * cb] = y.astype(BF16)
    if not with_mlp:
        return
    gate_ref[...] = block(3).astype(BF16)
    u = jax.nn.gelu(block(4))
    v = jax.nn.gelu(block(5))
    for g in range(MLP_GROUPS):
        cols = slice(g * LANE, (g + 1) * LANE)
        vn = _rms(v[:, cols], mng_ref[:, cols]).astype(BF16)
        for ci in range(tm // MLP_CHUNK):
            rows = slice(ci * MLP_CHUNK, (ci + 1) * MLP_CHUNK)
            s = _dot(sw_ref[g], vn[rows, :]) + sb_ref[g]
            ob_ref[rows, cols] = (u[rows, cols] * s).astype(BF16)


def _prep_inproj_weights(w_in, hg):
    wq = DN_HEADS * DN_HEAD_DIM
    n_ab = 4 * DN_HEADS
    w_main = jnp.concatenate([w_in[:, :4 * wq].astype(BF16), w_in[:, 4 * wq + n_ab:].astype(BF16)], axis=1)
    w = w_in[:, 4 * wq:4 * wq + n_ab].astype(BF16)
    wa = w[:, :2 * DN_HEADS].reshape(-1, 2, DN_HEADS // hg, hg)
    wb = w[:, 2 * DN_HEADS:].reshape(-1, 2, DN_HEADS // hg, hg)
    grp = jnp.concatenate([jnp.moveaxis(wa, 2, 1), jnp.moveaxis(wb, 2, 1)], axis=2)
    grp = grp.reshape(w.shape[0], DN_HEADS // hg, 4 * hg)
    w_ab = jnp.pad(grp, ((0, 0), (0, 0), (0, LANE - 4 * hg))).reshape(w.shape[0], -1)
    return w_main, w_ab


def _inproj_call(h2, w_main, w_ab, conv_w, sw, sb, mng, *, tm, seq_len, with_mlp):
    n, d = h2.shape
    cb = PROJ_CB
    nab = w_ab.shape[1]
    ncb = 6 if with_mlp else 3
    assert seq_len % tm == 0 and tm % HALO == 0
    kern = functools.partial(_inproj_kernel, tm=tm, seq_tiles=seq_len // tm, with_mlp=with_mlp)
    once = dict(pipeline_mode=pl.Buffered(1))
    hb = tm // HALO
    in_specs = [pl.BlockSpec((tm, d), lambda i: (i, 0)),
                pl.BlockSpec((HALO, d), lambda i: (jnp.maximum(i * hb - 1, 0), 0)),
                pl.BlockSpec((HALO, d), lambda i: (jnp.minimum((i + 1) * hb, n // HALO - 1), 0)),
                pl.BlockSpec((d, ncb * cb), lambda i: (0, 0), **once),
                pl.BlockSpec((d, nab), lambda i: (0, 0), **once),
                pl.BlockSpec((CONV_K, 3 * cb), lambda i: (0, 0), **once)]
    out_specs = [pl.BlockSpec((tm, 3 * cb), lambda i: (i, 0)),
                 pl.BlockSpec((tm, nab), lambda i: (i, 0))]
    out_shape = [jax.ShapeDtypeStruct((n, 3 * cb), BF16), jax.ShapeDtypeStruct((n, nab), F32)]
    args = [h2, h2, h2, w_main, w_ab, conv_w]
    if with_mlp:
        in_specs += [pl.BlockSpec((MLP_GROUPS, MLP_CHUNK, MLP_CHUNK), lambda i: (0, 0, 0), **once),
                     pl.BlockSpec((MLP_GROUPS, MLP_CHUNK, LANE), lambda i: (0, 0, 0), **once),
                     pl.BlockSpec((1, cb), lambda i: (0, 0))]
        out_specs += [pl.BlockSpec((tm, cb), lambda i: (i, 0))] * 2
        out_shape += [jax.ShapeDtypeStruct((n, cb), BF16)] * 2
        args += [sw, sb, mng]
    return pl.pallas_call(
        kern,
        grid=(n // tm,),
        in_specs=in_specs,
        out_specs=out_specs,
        out_shape=out_shape,
        compiler_params=_cparams(("parallel",)),
        name="inproj" if with_mlp else "inproj_ctx",
    )(*args)


def _tri_inverse(lmats, uppers, ii, jj):
    eye = (ii == jj).astype(F32)
    b16 = (ii // 16) == (jj // 16)
    b32 = (ii // 32) == (jj // 32)
    m1 = {False: b32 & ((ii // 16) > (jj // 16)), True: b32 & ((ii // 16) < (jj // 16))}
    m2 = {False: (ii // 32) > (jj // 32), True: (ii // 32) < (jj // 32)}
    a1 = [jnp.where(b16, l, 0.0) for l in lmats]
    a1b = [a.astype(BF16) for a in a1]
    a2b = [_dot(a, a).astype(BF16) for a in a1b]
    yield
    a4b = [_dot(a, a).astype(BF16) for a in a2b]
    yield
    a8b = [_dot(a, a).astype(BF16) for a in a4b]
    yield
    p = [eye - a for a in a1]
    for ab in (a2b, a4b, a8b):
        p = [pi + _dot(pi.astype(BF16), a) for pi, a in zip(p, ab)]
        yield
    for masks in (m1, m2):
        pb = [pi.astype(BF16) for pi in p]
        xm = [_dot(jnp.where(masks[u], l, 0.0).astype(BF16), b).astype(BF16)
              for l, u, b in zip(lmats, uppers, pb)]
        yield
        p = [pi - _dot(b, x) for pi, b, x in zip(p, pb, xm)]
        yield
    return p


def _run_interleaved(*gens):
    live = list(gens)
    while live:
        for g in list(live):
            try:
                next(g)
            except StopIteration:
                live.remove(g)


def _dn_kernel(qx_ref, kx_ref, vx_ref, qc_ref, kc_ref, vc_ref, abx_ref, abc_ref, gate_ref,
               alog_ref, dtb_ref, hng_ref,
               out_ref,
               col_s, row_s, st_s, o_s, pu_s, pwq_s, pkt_s, pqk_s, pgl_s,
               *, hg, tx, tc, unroll):
    C = DN_CHUNK
    D = DN_HEAD_DIM
    ncc, ncx = tc // C, tx // C
    nch = ncc + ncx
    nchain = 2 * hg

    ii = lax.broadcasted_iota(jnp.int32, (C, C), 0)
    jj = lax.broadcasted_iota(jnp.int32, (C, C), 1)
    lane_t = lax.broadcasted_iota(jnp.int32, (C, LANE), 1)

    tril = (ii >= jj).astype(F32)
    triu = (ii <= jj).astype(F32)
    ones = jnp.ones((C, C), F32)
    sum_mats = jnp.concatenate([tril, triu, ones], axis=0).astype(BF16)
    neg_a = -jnp.exp(alog_ref[0])
    dtb = dtb_ref[0]

    def gate_phase(ab_ref, n_ch, chunk0):
        def body(n, carry):
            t = ab_ref[0, pl.ds(pl.multiple_of(n * C, C), C), :]
            g = neg_a * jax.nn.softplus(t + dtb)
            g = jnp.where(lane_t < nchain, g, 0.0)
            hi = g.astype(BF16)
            r1 = g - hi.astype(F32)
            mid = r1.astype(BF16)
            lo = (r1 - mid.astype(F32)).astype(BF16)
            cs = _dot(sum_mats, jnp.concatenate([hi, mid, lo], axis=1))
            cs = cs[:, :LANE] + cs[:, LANE:2 * LANE] + cs[:, 2 * LANE:]
            gt = cs[2 * C:]
            gc = jnp.where(lane_t < hg, cs[:C], cs[C:2 * C])
            beta = pltpu.roll(jax.nn.sigmoid(t), LANE - nchain, 1)
            col_s[chunk0 + n, 0] = gc
            col_s[chunk0 + n, 1] = beta
            col_s[chunk0 + n, 2] = gt
            row_s[chunk0 + n] = gc.T[0:16, :]
            return carry

        lax.fori_loop(0, n_ch, body, 0, unroll=4)

    gate_phase(abc_ref, ncc, 0)
    gate_phase(abx_ref, ncx, ncc)

    def chunk_of(s):
        cf = s
        cb = jnp.where(s < ncc, ncc - 1 - s, nch + ncc - 1 - s)
        return cf, cb

    def bcast(tile, r):
        return jnp.broadcast_to(tile[:, r:r + 1], (C, LANE))

    def stage1(it, par, in_context):
        q_ref, k_ref, v_ref, c0 = (qc_ref, kc_ref, vc_ref, 0) if in_context else (qx_ref, kx_ref, vx_ref, ncc)
        jobs = []
        for u in range(unroll):
            cf, cb = chunk_of(it * unroll + u)
            for d in range(2):
                c = cf if d == 0 else cb
                gc_t = col_s[c, 0]
                be_t = col_s[c, 1]
                gt_t = col_s[c, 2]
                e1_t = jnp.exp(gc_t)
                tiles = dict(gc=gc_t, be=be_t, e1=e1_t, ca=be_t * e1_t, e2=jnp.exp(gt_t - gc_t),
                             gl=jnp.exp(gt_t), rows=row_s[c])
                for hl in range(hg):
                    jobs.append((par * unroll + u, d * hg + hl, pl.multiple_of((c - c0) * C, C), tiles, d, hl))
        tri = {0: ii >= jj, 1: ii <= jj}
        strict = {0: ii > jj, 1: ii < jj}

        def rd(ref, job):
            return ref[0, pl.ds(job[2], C), job[5] * D:(job[5] + 1) * D]

        qn = [rd(q_ref, j) for j in jobs]
        kn = [rd(k_ref, j) for j in jobs]
        gmat = [_dot_nt(jnp.concatenate([q, k], axis=0), k) for q, k in zip(qn, kn)]
        yield
        dec =[jnp.exp(jnp.where(tri[j[4]], bcast(j[3]["gc"], j[1])[:, :C] - j[3]["rows"][j[1]:j[1] + 1, :], -1e30))
               for j in jobs]
        lmat = [jnp.where(strict[j[4]], g[C:] * dc, 0.0) * bcast(j[3]["be"], j[1])[:, :C]
                for j, g, dc in zip(jobs, gmat, dec)]
        for j, g, dc in zip(jobs, gmat, dec):
            pqk_s[j[0], j[1]] = (g[:C] * dc).astype(BF16)
        tinv = yield from _tri_inverse(lmat, [bool(j[4]) for j in jobs], ii, jj)
        knf = [k.astype(F32) for k in kn]
        rhs = [jnp.concatenate([(rd(v_ref, j).astype(F32) * bcast(j[3]["be"], j[1])).astype(BF16),
                                (kf * bcast(j[3]["ca"], j[1])).astype(BF16)], axis=1)
               for j, kf in zip(jobs, knf)]
        uw = [_dot(t.astype(BF16), r) for t, r in zip(tinv, rhs)]
        yield
        for j, x, q, kf in zip(jobs, uw, qn, knf):
            slot, r = j[0], j[1]
            qd = (q.astype(F32) * bcast(j[3]["e1"], r)).astype(BF16)
            pu_s[slot, r] = x[:, :D].astype(BF16)
            pwq_s[slot, r] = jnp.concatenate([x[:, D:].astype(BF16), qd], axis=0)
            pkt_s[slot, r] = (kf * bcast(j[3]["e2"], r)).T.astype(BF16)
            pgl_s[slot, r] = jnp.broadcast_to(j[3]["gl"][0:8, r:r + 1], (8, LANE))

    def stage2(it, par):
        for u in range(unroll):
            slot = par * unroll + u
            cf, cb = chunk_of(it * unroll + u)
            chains = [(d * hg + hl, pl.multiple_of((cb if d else cf) * C, C), hl)
                      for d in range(2) for hl in range(hg)]
            st = [st_s[r] for r, _, _ in chains]
            ws = [_dot(pwq_s[slot, r], s.astype(BF16)) for (r, _, _), s in zip(chains, st)]
            yield
            vn = [(pu_s[slot, r].astype(F32) - w[:C]).astype(BF16) for (r, _, _), w in zip(chains, ws)]
            o = [w[C:] + _dot(pqk_s[slot, r], v) for (r, _, _), w, v in zip(chains, ws, vn)]
            kv = [_dot(pkt_s[slot, r], v) for (r, _, _), v in zip(chains, vn)]
            yield
            for (r, row0, hl), s, x, y in zip(chains, st, kv, o):
                st_s[r] = s * jnp.concatenate([pgl_s[slot, r]] * (D // 8), axis=0) + x
                o_s[pl.ds(row0, C), hl * D:(hl + 1) * D] += y

    st_s[...] = jnp.zeros_like(st_s)
    o_s[...] = jnp.zeros_like(o_s)
    _run_interleaved(stage1(0, 0, True))

    def loop_body(it, carry):
        _run_interleaved(stage1(it, it % 2, False), stage2(it - 1, (it - 1) % 2))
        return carry

    n_it = nch // unroll
    lax.fori_loop(1, n_it, loop_body, 0)
    _run_interleaved(stage2(n_it - 1, (n_it - 1) % 2))

    hn = hng_ref[...]
    for hl in range(hg):
        cols = slice(hl * D, (hl + 1) * D)
        o = o_s[tc:tc + tx, cols]
        zg = gate_ref[0, :, cols].astype(F32)
        out_ref[0, :, cols] = (_rms(o, hn) * _silu(zg)).astype(BF16)


def _dn_call(qkv_x, qkv_c, ab_x, ab_c, gate, alog_g, dtb_g, hng, *, hg, unroll):
    b, tx, w3 = qkv_x.shape
    tc = qkv_c.shape[1]
    assert tc // DN_CHUNK == unroll and (tx // DN_CHUNK) % unroll == 0
    nslot = 2 * unroll
    nhg = DN_HEADS // hg
    wd = hg * DN_HEAD_DIM
    nch = (tx + tc) // DN_CHUNK
    nchain = 2 * hg
    kern = functools.partial(_dn_kernel, hg=hg, tx=tx, tc=tc, unroll=unroll)

    def col(t, off):
        return pl.BlockSpec((1, t, wd), lambda bi, gi: (bi, 0, off * nhg + gi))

    def ab(t):
        return pl.BlockSpec((1, t, LANE), lambda bi, gi: (bi, 0, gi), pipeline_mode=pl.Buffered(1))

    small = pl.BlockSpec((1, 1, LANE), lambda bi, gi: (gi, 0, 0))
    return pl.pallas_call(
        kern,
        grid=(b, nhg),
        in_specs=[col(tx, 0), col(tx, 1), col(tx, 2), col(tc, 0), col(tc, 1), col(tc, 2),
                  ab(tx), ab(tc),
                  pl.BlockSpec((1, tx, wd), lambda bi, gi: (bi, 0, gi), pipeline_mode=pl.Buffered(1)),
                  small, small,
                  pl.BlockSpec((1, LANE), lambda bi, gi: (0, 0))],
        out_specs=pl.BlockSpec((1, tx, wd), lambda bi, gi: (bi, 0, gi)),
        out_shape=jax.ShapeDtypeStruct((b, tx, DN_HEADS * DN_HEAD_DIM), BF16),
        scratch_shapes=[
            pltpu.VMEM((nch, 3, DN_CHUNK, LANE), F32),
            pltpu.VMEM((nch, 16, DN_CHUNK), F32),
            pltpu.VMEM((nchain, DN_HEAD_DIM, DN_HEAD_DIM), F32),
            pltpu.VMEM((tc + tx, wd), F32),
            pltpu.VMEM((nslot, nchain, DN_CHUNK, DN_HEAD_DIM), BF16),
            pltpu.VMEM((nslot, nchain, 2 * DN_CHUNK, DN_HEAD_DIM), BF16),
            pltpu.VMEM((nslot, nchain, DN_HEAD_DIM, DN_CHUNK), BF16),
            pltpu.VMEM((nslot, nchain, DN_CHUNK, DN_CHUNK), BF16),
            pltpu.VMEM((nslot, nchain, 8, LANE), F32)],
        compiler_params=_cparams(("parallel", "arbitrary")),
        name="deltanet",
    )(qkv_x, qkv_x, qkv_x, qkv_c, qkv_c, qkv_c, ab_x, ab_c, gate, alog_g, dtb_g, hng)


def _outproj_kernel(x_ref, oa_ref, ob_ref, w_ref, m_ref, out_ref):
    half = oa_ref.shape[1]
    y = _dot(oa_ref[...], w_ref[:half, :]) + _dot(ob_ref[...], w_ref[half:, :])
    out_ref[...] = x_ref[...] + m_ref[0, 5:6, :] * y


def _outproj_call(x2d, oa, ob, w, m3, *, rows_per_mod, tm):
    n, d = x2d.shape
    half = oa.shape[1]
    tiles_per_mod = rows_per_mod // tm
    return pl.pallas_call(
        _outproj_kernel,
        grid=(n // tm,),
        in_specs=[pl.BlockSpec((tm, d), lambda i: (i, 0)),
                  pl.BlockSpec((tm, half), lambda i: (i, 0)),
                  pl.BlockSpec((tm, half), lambda i: (i, 0)),
                  pl.BlockSpec((2 * half, d), lambda i: (0, 0), pipeline_mode=pl.Buffered(1)),
                  pl.BlockSpec((1, N_MOD, d), lambda i: (i // tiles_per_mod, 0, 0))],
        out_specs=pl.BlockSpec((tm, d), lambda i: (i, 0)),
        out_shape=jax.ShapeDtypeStruct((n, d), F32),
        compiler_params=_cparams(("parallel",)),
        name="outproj",
    )(x2d, oa, ob, w, m3)


def _chain_rows(p, hg):
    nhg = DN_HEADS // hg
    rows = jnp.moveaxis(p.reshape(2, nhg, hg), 1, 0).reshape(nhg, 1, 2 * hg)
    return jnp.pad(rows, ((0, 0), (0, 0), (0, LANE - 2 * hg)))


FFN_TM = 512
FFN_TF = 512
PROJ_TM = 512
OUT_TM = 512
DN_HG = 4
DN_UNROLL = 4
MOD_ROWS = 16


def kernel(x, c, ctx, c_ctx, w_mod, b_mod, norm_g, ffn1_w_in, ffn1_w_out, w_in, conv_w, a_log, dt_bias, head_norm_g, spatial_w, spatial_b, mlp_norm_g, w_out, ffn2_w_in, ffn2_w_out, final_g):
    bsz, t, d = x.shape
    tc = ctx.shape[1]
    assert w_mod.shape[0] == 1, "single-layer block"
    assert bsz + 1 <= MOD_ROWS

    c_all = jnp.concatenate([c, c_ctx[None, :], jnp.zeros((MOD_ROWS - bsz - 1, d), F32)], axis=0)
    m3 = _mod_call(c_all, w_mod[0], b_mod).reshape(MOD_ROWS, N_MOD, d)

    g0, g1, g2 = norm_g[0, 0:1], norm_g[0, 1:2], norm_g[0, 2:3]
    w_ffn1 = _prep_ffn_weights(ffn1_w_in[0], ffn1_w_out[0], FFN_TF)
    w_ffn2 = _prep_ffn_weights(ffn2_w_in[0], ffn2_w_out[0], FFN_TF)

    x1, h_x = _ffn_call(x.reshape(bsz * t, d), m3, g0, w_ffn1, g1, mode="ffn1", mod_base=0,
                        rows_per_mod=t, mod_offset=0, tm=FFN_TM, tf=FFN_TF)
    (h_c,) = _ffn_call(ctx.reshape(bsz * tc, d), m3, g0, w_ffn1, g1, mode="ffn1ctx", mod_base=0,
                       rows_per_mod=bsz * tc, mod_offset=bsz, tm=FFN_TM, tf=FFN_TF)

    w_main, w_ab = _prep_inproj_weights(w_in[0], DN_HG)
    sb = jnp.broadcast_to(spatial_b[0][:, :, None], (MLP_GROUPS, MLP_CHUNK, LANE))
    qkv_x, ab_x, gate_x, o_b = _inproj_call(h_x, w_main, w_ab, conv_w[0], spatial_w[0].astype(BF16), sb, mlp_norm_g,
                                            tm=PROJ_TM, seq_len=t, with_mlp=True)
    qkv_c, ab_c = _inproj_call(h_c, w_main, w_ab, conv_w[0], None, None, None,
                               tm=min(PROJ_TM, tc), seq_len=tc, with_mlp=False)

    wq = DN_HEADS * DN_HEAD_DIM
    o_a = _dn_call(qkv_x.reshape(bsz, t, 3 * wq), qkv_c.reshape(bsz, tc, 3 * wq),
                   ab_x.reshape(bsz, t, -1), ab_c.reshape(bsz, tc, -1), gate_x.reshape(bsz, t, wq),
                   _chain_rows(a_log[0], DN_HG), _chain_rows(dt_bias[0], DN_HG), head_norm_g,
                   hg=DN_HG, unroll=DN_UNROLL)

    x2 = _outproj_call(x1, o_a.reshape(bsz * t, wq), o_b, w_out[0].astype(BF16), m3, rows_per_mod=t, tm=OUT_TM)
    (out,) = _ffn_call(x2, m3, g2, w_ffn2, final_g[None, :], mode="ffn2", mod_base=6,
                       rows_per_mod=t, mod_offset=0, tm=FFN_TM, tf=FFN_TF)
    return out.reshape(bsz, t, d)
```

```python
import functools

import jax
import jax.numpy as jnp
import numpy as np
from jax import lax
from jax.experimental import pallas as pl
from jax.experimental.pallas import tpu as pltpu

F32 = jnp.float32
BF16 = jnp.bfloat16

EPS = 1e-6
LANE = 128
N_MOD = 9
DN_HEADS = 8
DN_HEAD_DIM = 128
DN_CHUNK = 64
CONV_K = 5
MLP_GROUPS = 8
MLP_CHUNK = 128
PROJ_CB = 1024
HALO = 16
V7X_VMEM_LIMIT = 60000 * 1024
ROW_CHUNK = 16
ROW_UNROLL = 8


def _cparams(sem):
    return pltpu.CompilerParams(dimension_semantics=sem, vmem_limit_bytes=V7X_VMEM_LIMIT)


def _rms(x, g):
    return x * lax.rsqrt(jnp.mean(x * x, axis=-1, keepdims=True) + EPS) * g


def _silu(x):
    return x * jax.nn.sigmoid(x)


def _dot(a, b):
    return jnp.dot(a, b, preferred_element_type=F32)


def _dot_nt(a, b):
    return lax.dot_general(a, b, (((1,), (1,)), ((), ())), preferred_element_type=F32)


def _mod_kernel(c_ref, w_ref, b_ref, o_ref):
    s = _silu(c_ref[...]).astype(BF16)
    o_ref[...] = _dot(s, w_ref[...].astype(BF16)) + b_ref[...]


def _mod_call(c_all, w_mod, b_mod, tn=1024):
    rows, d = c_all.shape
    n = w_mod.shape[1]
    return pl.pallas_call(
        _mod_kernel,
        grid=(n // tn,),
        in_specs=[pl.BlockSpec((rows, d), lambda j: (0, 0)),
                  pl.BlockSpec((d, tn), lambda j: (0, j)),
                  pl.BlockSpec((1, tn), lambda j: (0, j))],
        out_specs=pl.BlockSpec((rows, tn), lambda j: (0, j)),
        out_shape=jax.ShapeDtypeStruct((rows, n), F32),
        compiler_params=_cparams(("arbitrary",)),
        name="mod",
    )(c_all, w_mod, b_mod)


def _ffn_kernel(*refs, mode, mod_base, nf):
    if mode == "ffn2":
        x_ref, m_ref, g_ref, wa_ref, wb_ref, wo_ref, g2_ref, out_ref, h_s, acc_s, vec_s = refs
    elif mode == "ffn1":
        x_ref, m_ref, g_ref, wa_ref, wb_ref, wo_ref, g2_ref, out_ref, h2_ref, h_s, acc_s, vec_s = refs
    else:
        x_ref, m_ref, g_ref, wa_ref, wb_ref, wo_ref, g2_ref, h2_ref, h_s, acc_s, vec_s = refs
    j = pl.program_id(1)
    tm, d = h_s.shape

    def mvec(k):
        return m_ref[0, k:k + 1, :]

    def tile_rows(v):
        return jnp.concatenate([v] * (ROW_CHUNK // 8), axis=0)

    def norm_scale(x, gain, shift=None):
        y = x * lax.rsqrt(jnp.mean(x * x, axis=-1, keepdims=True) + EPS) * tile_rows(gain)
        return y if shift is None else y + tile_rows(shift)

    def row_loop(fn):
        def body(r, carry):
            fn(pl.ds(pl.multiple_of(r * ROW_CHUNK, ROW_CHUNK), ROW_CHUNK))
            return carry
        lax.fori_loop(0, tm // ROW_CHUNK, body, 0, unroll=ROW_UNROLL)

    @pl.when(j == 0)
    def _():
        def put(k, v):
            vec_s[k] = jnp.broadcast_to(v, (8, d))
        put(0, g_ref[...] * (1.0 + mvec(mod_base + 1)))
        put(1, mvec(mod_base))
        put(2, 0.5 * mvec(mod_base + 2))
        if mode == "ffn2":
            put(3, g2_ref[...])
        else:
            put(3, g2_ref[...] * (1.0 + mvec(mod_base + 4)))
            put(4, mvec(mod_base + 3))

        def prologue(rows):
            h_s[rows, :] = norm_scale(x_ref[rows, :], vec_s[0], vec_s[1]).astype(BF16)
            acc_s[rows, :] = jnp.zeros((ROW_CHUNK, d), F32)

        row_loop(prologue)

    h = h_s[...]
    act = (_silu(_dot(h, wa_ref[...])) * _dot(h, wb_ref[...])).astype(BF16)
    acc_s[...] += _dot(act, wo_ref[...])

    @pl.when(j == nf - 1)
    def _():
        def epilogue(rows):
            xn = x_ref[rows, :] + tile_rows(vec_s[2]) * acc_s[rows, :]
            if mode == "ffn2":
                out_ref[rows, :] = norm_scale(xn, vec_s[3])
            else:
                if mode == "ffn1":
                    out_ref[rows, :] = xn
                h2_ref[rows, :] = norm_scale(xn, vec_s[3], vec_s[4]).astype(BF16)

        row_loop(epilogue)


def _split_cast_kernel(a_ref, b_ref, wa_ref, wb_ref, tail_s, *, f, tb, nb):
    s = pl.program_id(0)
    rem = f % tb
    lane = lax.broadcasted_iota(jnp.int32, a_ref.shape, 1)
    a_blk = jnp.minimum(s, nb - 1)
    wa_ref[...] = jnp.where(a_blk * tb + lane < f, a_ref[...], 0.0).astype(BF16)

    @pl.when(s == 0)
    def _():
        wb_ref[...] = jnp.zeros_like(wb_ref)

    @pl.when(s > 0)
    def _():
        b = jnp.concatenate([tail_s[...], b_ref[:, :rem]], axis=1) if rem else tail_s[...]
        wb_ref[...] = jnp.where((s - 1) * tb + lane < f, b, 0.0).astype(BF16)

    tail_s[...] = b_ref[:, rem:]


def _split_cast_call(w_in, f_pad, tb=256):
    d, f2 = w_in.shape
    f = f2 // 2
    assert f2 % tb == 0 and f_pad % tb == 0 and (f % tb) % LANE == 0
    nb = f_pad // tb
    last = f2 // tb - 1
    return pl.pallas_call(
        functools.partial(_split_cast_kernel, f=f, tb=tb, nb=nb),
        grid=(nb + 1,),
        in_specs=[pl.BlockSpec((d, tb), lambda s: (0, jnp.minimum(s, nb - 1))),
                  pl.BlockSpec((d, tb), lambda s: (0, jnp.minimum(f // tb + s, last)))],
        out_specs=[pl.BlockSpec((d, tb), lambda s: (0, jnp.minimum(s, nb - 1))),
                   pl.BlockSpec((d, tb), lambda s: (0, jnp.maximum(s - 1, 0)))],
        out_shape=[jax.ShapeDtypeStruct((d, f_pad), BF16)] * 2,
        scratch_shapes=[pltpu.VMEM((d, tb - f % tb), F32)],
        compiler_params=_cparams(("arbitrary",)),
        name="ffn_weight_cast",
    )(w_in, w_in)


def _prep_ffn_weights(w_in, w_out, tf):
    f = w_in.shape[1] // 2
    pad = -f % tf
    wa, wb = _split_cast_call(w_in, f + pad)
    wo = jnp.pad(w_out.astype(BF16), ((0, pad), (0, 0)))
    return wa, wb, wo


def _ffn_call(x2d, m3, g, w3, g2, *, mode, mod_base, rows_per_mod, mod_offset, tm, tf):
    r, d = x2d.shape
    wa, wb, wo = w3
    nf = wo.shape[0] // tf
    tiles_per_mod = rows_per_mod // tm
    kern = functools.partial(_ffn_kernel, mode=mode, mod_base=mod_base, nf=nf)
    x_spec = pl.BlockSpec((tm, d), lambda i, j: (i, 0))
    in_specs = [x_spec,
                pl.BlockSpec((1, N_MOD, d), lambda i, j: (mod_offset + i // tiles_per_mod, 0, 0)),
                pl.BlockSpec((1, d), lambda i, j: (0, 0)),
                pl.BlockSpec((d, tf), lambda i, j: (0, j)),
                pl.BlockSpec((d, tf), lambda i, j: (0, j)),
                pl.BlockSpec((tf, d), lambda i, j: (j, 0)),
                pl.BlockSpec((1, d), lambda i, j: (0, 0))]
    args = [x2d, m3, g, wa, wb, wo, g2]
    o_f32 = (jax.ShapeDtypeStruct((r, d), F32), x_spec)
    o_bf = (jax.ShapeDtypeStruct((r, d), BF16), x_spec)
    outs = {"ffn2": [o_f32], "ffn1": [o_f32, o_bf], "ffn1ctx": [o_bf]}[mode]
    return pl.pallas_call(
        kern,
        grid=(r // tm, nf),
        in_specs=in_specs,
        out_specs=[o[1] for o in outs],
        out_shape=[o[0] for o in outs],
        scratch_shapes=[pltpu.VMEM((tm, d), BF16), pltpu.VMEM((tm, d), F32), pltpu.VMEM((5, 8, d), F32)],
        compiler_params=_cparams(("parallel", "arbitrary")),
        name=mode,
    )(*args)


def _inproj_kernel(*refs, tm, seq_tiles, with_mlp):
    if with_mlp:
        (h_ref, hp_ref, hn_ref, w_ref, wab_ref, cw_ref, sw_ref, sb_ref, mng_ref,
         qkv_ref, ab_ref, gate_ref, ob_ref) = refs
    else:
        h_ref, hp_ref, hn_ref, w_ref, wab_ref, cw_ref, qkv_ref, ab_ref = refs
    cb = PROJ_CB
    h = h_ref[...]

    def block(jb):
        return _dot(h, w_ref[:, jb * cb:(jb + 1) * cb])

    ab_ref[...] = _dot(h, wab_ref[...])

    i = pl.program_id(0)
    pos = i % seq_tiles
    keep_prev = jnp.where(pos > 0, 1.0, 0.0).astype(BF16)
    keep_next = jnp.where(pos < seq_tiles - 1, 1.0, 0.0).astype(BF16)
    hext = jnp.concatenate([hp_ref[...] * keep_prev, h, hn_ref[...] * keep_next], axis=0)
    first = HALO - CONV_K // 2

    def project_ext(jb):
        return _dot(hext, w_ref[:, jb * cb:(jb + 1) * cb])

    def conv_tail(jb, zext):
        for rc in range(tm // DN_CHUNK):
            win = zext[rc * DN_CHUNK:rc * DN_CHUNK + DN_CHUNK + 2 * HALO]
            y = cw_ref[0:1, jb * cb:(jb + 1) * cb] * win[first:first + DN_CHUNK]
            for t in range(1, CONV_K):
                y = y + cw_ref[t:t + 1, jb * cb:(jb + 1) * cb] * win[first + t:first + t + DN_CHUNK]
            y = _silu(y)
            if jb < 2:
                parts = []
                for hd in range(cb // DN_HEAD_DIM):
                    yh = y[:, hd * DN_HEAD_DIM:(hd + 1) * DN_HEAD_DIM]
                    yh = yh * lax.rsqrt(jnp.sum(yh * yh, axis=-1, keepdims=True) + EPS)
                    parts.append(yh * (DN_HEAD_DIM ** -0.5) if jb == 0 else yh)
                y = jnp.concatenate(parts, axis=1)
            qkv_ref[rc * DN_CHUNK:(rc + 1) * DN_CHUNK, jb * cb:(jb + 1) * cb] = y.astype(BF16)

    zq = project_ext(0)
    if not with_mlp:
        zk = project_ext(1)
        conv_tail(0, zq)
        zv = project_ext(2)
        conv_tail(1, zk)
        conv_tail(2, zv)
        return
    gate_ref[...] = block(3).astype(BF16)
    conv_tail(0, zq)
    zk = project_ext(1)
    u = jax.nn.gelu(block(4))
    conv_tail(1, zk)
    zv = project_ext(2)
    v = jax.nn.gelu(block(5))
    conv_tail(2, zv)
    for g in range(MLP_GROUPS):
        cols = slice(g * LANE, (g + 1) * LANE)
        vn = _rms(v[:, cols], mng_ref[:, cols]).astype(BF16)
        for ci in range(tm // MLP_CHUNK):
            rows = slice(ci * MLP_CHUNK, (ci + 1) * MLP_CHUNK)
            s = _dot(sw_ref[g], vn[rows, :]) + sb_ref[g]
            ob_ref[rows, cols] = (u[rows, cols] * s).astype(BF16)


def _prep_inproj_weights(w_in, hg):
    wq = DN_HEADS * DN_HEAD_DIM
    n_ab = 4 * DN_HEADS
    w_main = jnp.concatenate([w_in[:, :4 * wq].astype(BF16), w_in[:, 4 * wq + n_ab:].astype(BF16)], axis=1)
    w = w_in[:, 4 * wq:4 * wq + n_ab].astype(BF16)
    wa = w[:, :2 * DN_HEADS].reshape(-1, 2, DN_HEADS // hg, hg)
    wb = w[:, 2 * DN_HEADS:].reshape(-1, 2, DN_HEADS // hg, hg)
    grp = jnp.concatenate([jnp.moveaxis(wa, 2, 1), jnp.moveaxis(wb, 2, 1)], axis=2)
    grp = grp.reshape(w.shape[0], DN_HEADS // hg, 4 * hg)
    w_ab = jnp.pad(grp, ((0, 0), (0, 0), (0, LANE - 4 * hg))).reshape(w.shape[0], -1)
    return w_main, w_ab


def _inproj_call(h2, w_main, w_ab, conv_w, sw, sb, mng, *, tm, seq_len, with_mlp):
    n, d = h2.shape
    cb = PROJ_CB
    nab = w_ab.shape[1]
    ncb = 6 if with_mlp else 3
    assert seq_len % tm == 0 and tm % HALO == 0
    kern = functools.partial(_inproj_kernel, tm=tm, seq_tiles=seq_len // tm, with_mlp=with_mlp)
    once = dict(pipeline_mode=pl.Buffered(1))
    hb = tm // HALO
    in_specs = [pl.BlockSpec((tm, d), lambda i: (i, 0)),
                pl.BlockSpec((HALO, d), lambda i: (jnp.maximum(i * hb - 1, 0), 0)),
                pl.BlockSpec((HALO, d), lambda i: (jnp.minimum((i + 1) * hb, n // HALO - 1), 0)),
                pl.BlockSpec((d, ncb * cb), lambda i: (0, 0), **once),
                pl.BlockSpec((d, nab), lambda i: (0, 0), **once),
                pl.BlockSpec((CONV_K, 3 * cb), lambda i: (0, 0), **once)]
    out_specs = [pl.BlockSpec((tm, 3 * cb), lambda i: (i, 0)),
                 pl.BlockSpec((tm, nab), lambda i: (i, 0))]
    out_shape = [jax.ShapeDtypeStruct((n, 3 * cb), BF16), jax.ShapeDtypeStruct((n, nab), F32)]
    args = [h2, h2, h2, w_main, w_ab, conv_w]
    if with_mlp:
        in_specs += [pl.BlockSpec((MLP_GROUPS, MLP_CHUNK, MLP_CHUNK), lambda i: (0, 0, 0), **once),
                     pl.BlockSpec((MLP_GROUPS, MLP_CHUNK, LANE), lambda i: (0, 0, 0), **once),
                     pl.BlockSpec((1, cb), lambda i: (0, 0))]
        out_specs += [pl.BlockSpec((tm, cb), lambda i: (i, 0))] * 2
        out_shape += [jax.ShapeDtypeStruct((n, cb), BF16)] * 2
        args += [sw, sb, mng]
    return pl.pallas_call(
        kern,
        grid=(n // tm,),
        in_specs=in_specs,
        out_specs=out_specs,
        out_shape=out_shape,
        compiler_params=_cparams(("parallel",)),
        name="inproj" if with_mlp else "inproj_ctx",
    )(*args)


def _tri_inverse(lmats, uppers, ii, jj):
    eye = (ii == jj).astype(F32)
    b16 = (ii // 16) == (jj // 16)
    b32 = (ii // 32) == (jj // 32)
    m1 = {False: b32 & ((ii // 16) > (jj // 16)), True: b32 & ((ii // 16) < (jj // 16))}
    m2 = {False: (ii // 32) > (jj // 32), True: (ii // 32) < (jj // 32)}
    a1 = [jnp.where(b16, l, 0.0) for l in lmats]
    a1b = [a.astype(BF16) for a in a1]
    a2b = [_dot(a, a).astype(BF16) for a in a1b]
    yield
    a4b = [_dot(a, a).astype(BF16) for a in a2b]
    yield
    a8b = [_dot(a, a).astype(BF16) for a in a4b]
    yield
    p = [eye - a for a in a1]
    for ab in (a2b, a4b, a8b):
        p = [pi + _dot(pi.astype(BF16), a) for pi, a in zip(p, ab)]
        yield
    for masks in (m1, m2):
        pb = [pi.astype(BF16) for pi in p]
        xm = [_dot(jnp.where(masks[u], l, 0.0).astype(BF16), b).astype(BF16)
              for l, u, b in zip(lmats, uppers, pb)]
        yield
        p = [pi - _dot(b, x) for pi, b, x in zip(p, pb, xm)]
        yield
    return p


def _run_interleaved(*gens):
    live = list(gens)
    while live:
        for g in list(live):
            try:
                next(g)
            except StopIteration:
                live.remove(g)


def _dn_kernel(qx_ref, kx_ref, vx_ref, qc_ref, kc_ref, vc_ref, abx_ref, abc_ref, gate_ref,
               alog_ref, dtb_ref, hng_ref,
               out_ref,
               col_s, row_s, st_s, o_s, pu_s, pwq_s, pkt_s, pqk_s, pgl_s,
               *, hg, tx, tc, unroll):
    C = DN_CHUNK
    D = DN_HEAD_DIM
    ncc, ncx = tc // C, tx // C
    nch = ncc + ncx
    nchain = 2 * hg

    ii = lax.broadcasted_iota(jnp.int32, (C, C), 0)
    jj = lax.broadcasted_iota(jnp.int32, (C, C), 1)
    lane_t = lax.broadcasted_iota(jnp.int32, (C, LANE), 1)

    tril = (ii >= jj).astype(F32)
    triu = (ii <= jj).astype(F32)
    ones = jnp.ones((C, C), F32)
    sum_mats = jnp.concatenate([tril, triu, ones], axis=0).astype(BF16)
    neg_a = -jnp.exp(alog_ref[0])
    dtb = dtb_ref[0]

    def gate_chunk(ab_ref, n, chunk0):
        t = ab_ref[0, n * C:(n + 1) * C, :]
        g = neg_a * jax.nn.softplus(t + dtb)
        g = jnp.where(lane_t < nchain, g, 0.0)
        hi = g.astype(BF16)
        r1 = g - hi.astype(F32)
        mid = r1.astype(BF16)
        lo = (r1 - mid.astype(F32)).astype(BF16)
        cs = _dot(sum_mats, jnp.concatenate([hi, mid, lo], axis=1))
        cs = cs[:, :LANE] + cs[:, LANE:2 * LANE] + cs[:, 2 * LANE:]
        gt = cs[2 * C:]
        gc = jnp.where(lane_t < hg, cs[:C], cs[C:2 * C])
        beta = pltpu.roll(jax.nn.sigmoid(t), LANE - nchain, 1)
        col_s[chunk0 + n, 0] = gc
        col_s[chunk0 + n, 1] = beta
        col_s[chunk0 + n, 2] = gt
        row_s[chunk0 + n] = gc.T[0:16, :]

    def gate_gen(ab_ref, n_ch, chunk0, per_yield=4):
        for n in range(n_ch):
            gate_chunk(ab_ref, n, chunk0)
            if n % per_yield == per_yield - 1:
                yield

    _run_interleaved(gate_gen(abc_ref, ncc, 0))

    def chunk_of(s):
        cf = s
        cb = jnp.where(s < ncc, ncc - 1 - s, nch + ncc - 1 - s)
        return cf, cb

    def bcast(tile, r):
        return jnp.broadcast_to(tile[:, r:r + 1], (C, LANE))

    def stage1(it, par, in_context):
        q_ref, k_ref, v_ref, c0 = (qc_ref, kc_ref, vc_ref, 0) if in_context else (qx_ref, kx_ref, vx_ref, ncc)
        jobs = []
        for u in range(unroll):
            cf, cb = chunk_of(it * unroll + u)
            for d in range(2):
                c = cf if d == 0 else cb
                gc_t = col_s[c, 0]
                be_t = col_s[c, 1]
                gt_t = col_s[c, 2]
                e1_t = jnp.exp(gc_t)
                tiles = dict(gc=gc_t, be=be_t, e1=e1_t, ca=be_t * e1_t, e2=jnp.exp(gt_t - gc_t),
                             gl=jnp.exp(gt_t), rows=row_s[c])
                for hl in range(hg):
                    jobs.append((par * unroll + u, d * hg + hl, pl.multiple_of((c - c0) * C, C), tiles, d, hl))
        tri = {0: ii >= jj, 1: ii <= jj}
        strict = {0: ii > jj, 1: ii < jj}

        def rd(ref, job):
            return ref[0, pl.ds(job[2], C), job[5] * D:(job[5] + 1) * D]

        qn = [rd(q_ref, j) for j in jobs]
        kn = [rd(k_ref, j) for j in jobs]
        gmat = [_dot_nt(jnp.concatenate([q, k], axis=0), k) for q, k in zip(qn, kn)]
        yield
        dec =[jnp.exp(jnp.where(tri[j[4]], bcast(j[3]["gc"], j[1])[:, :C] - j[3]["rows"][j[1]:j[1] + 1, :], -1e30))
               for j in jobs]
        lmat = [jnp.where(strict[j[4]], g[C:] * dc, 0.0) * bcast(j[3]["be"], j[1])[:, :C]
                for j, g, dc in zip(jobs, gmat, dec)]
        for j, g, dc in zip(jobs, gmat, dec):
            pqk_s[j[0], j[1]] = (g[:C] * dc).astype(BF16)
        tinv = yield from _tri_inverse(lmat, [bool(j[4]) for j in jobs], ii, jj)
        knf = [k.astype(F32) for k in kn]
        rhs = [jnp.concatenate([(rd(v_ref, j).astype(F32) * bcast(j[3]["be"], j[1])).astype(BF16),
                                (kf * bcast(j[3]["ca"], j[1])).astype(BF16)], axis=1)
               for j, kf in zip(jobs, knf)]
        uw = [_dot(t.astype(BF16), r) for t, r in zip(tinv, rhs)]
        yield
        for j, x, q, kf in zip(jobs, uw, qn, knf):
            slot, r = j[0], j[1]
            qd = (q.astype(F32) * bcast(j[3]["e1"], r)).astype(BF16)
            pu_s[slot, r] = x[:, :D].astype(BF16)
            pwq_s[slot, r] = jnp.concatenate([x[:, D:].astype(BF16), qd], axis=0)
            pkt_s[slot, r] = (kf * bcast(j[3]["e2"], r)).T.astype(BF16)
            pgl_s[slot, r] = jnp.broadcast_to(j[3]["gl"][0:8, r:r + 1], (8, LANE))

    def stage2(it, par):
        for u in range(unroll):
            slot = par * unroll + u
            cf, cb = chunk_of(it * unroll + u)
            chains = [(d * hg + hl, pl.multiple_of((cb if d else cf) * C, C), hl)
                      for d in range(2) for hl in range(hg)]
            st = [st_s[r] for r, _, _ in chains]
            ws = [_dot(pwq_s[slot, r], s.astype(BF16)) for (r, _, _), s in zip(chains, st)]
            yield
            vn = [(pu_s[slot, r].astype(F32) - w[:C]).astype(BF16) for (r, _, _), w in zip(chains, ws)]
            o = [w[C:] + _dot(pqk_s[slot, r], v) for (r, _, _), w, v in zip(chains, ws, vn)]
            kv = [_dot(pkt_s[slot, r], v) for (r, _, _), v in zip(chains, vn)]
            yield
            for (r, row0, hl), s, x, y in zip(chains, st, kv, o):
                st_s[r] = s * jnp.concatenate([pgl_s[slot, r]] * (D // 8), axis=0) + x
                o_s[pl.ds(row0, C), hl * D:(hl + 1) * D] += y

    st_s[...] = jnp.zeros_like(st_s)
    o_s[...] = jnp.zeros_like(o_s)
    _run_interleaved(stage1(0, 0, True), gate_gen(abx_ref, ncx, ncc))

    def loop_body(it, carry):
        _run_interleaved(stage1(it, it % 2, False), stage2(it - 1, (it - 1) % 2))
        return carry

    n_it = nch // unroll
    lax.fori_loop(1, n_it, loop_body, 0)
    _run_interleaved(stage2(n_it - 1, (n_it - 1) % 2))

    hn = hng_ref[...]
    for hl in range(hg):
        cols = slice(hl * D, (hl + 1) * D)
        o = o_s[tc:tc + tx, cols]
        zg = gate_ref[0, :, cols].astype(F32)
        out_ref[0, :, cols] = (_rms(o, hn) * _silu(zg)).astype(BF16)


def _dn_call(qkv_x, qkv_c, ab_x, ab_c, gate, alog_g, dtb_g, hng, *, hg, unroll):
    b, tx, w3 = qkv_x.shape
    tc = qkv_c.shape[1]
    assert tc // DN_CHUNK == unroll and (tx // DN_CHUNK) % unroll == 0
    nslot = 2 * unroll
    nhg = DN_HEADS // hg
    wd = hg * DN_HEAD_DIM
    nch = (tx + tc) // DN_CHUNK
    nchain = 2 * hg
    kern = functools.partial(_dn_kernel, hg=hg, tx=tx, tc=tc, unroll=unroll)

    def col(t, off):
        return pl.BlockSpec((1, t, wd), lambda bi, gi: (bi, 0, off * nhg + gi))

    def ab(t):
        return pl.BlockSpec((1, t, LANE), lambda bi, gi: (bi, 0, gi), pipeline_mode=pl.Buffered(1))

    small = pl.BlockSpec((1, 1, LANE), lambda bi, gi: (gi, 0, 0))
    return pl.pallas_call(
        kern,
        grid=(b, nhg),
        in_specs=[col(tx, 0), col(tx, 1), col(tx, 2), col(tc, 0), col(tc, 1), col(tc, 2),
                  ab(tx), ab(tc),
                  pl.BlockSpec((1, tx, wd), lambda bi, gi: (bi, 0, gi), pipeline_mode=pl.Buffered(1)),
                  small, small,
                  pl.BlockSpec((1, LANE), lambda bi, gi: (0, 0))],
        out_specs=pl.BlockSpec((1, tx, wd), lambda bi, gi: (bi, 0, gi)),
        out_shape=jax.ShapeDtypeStruct((b, tx, DN_HEADS * DN_HEAD_DIM), BF16),
        scratch_shapes=[
            pltpu.VMEM((nch, 3, DN_CHUNK, LANE), F32),
            pltpu.VMEM((nch, 16, DN_CHUNK), F32),
            pltpu.VMEM((nchain, DN_HEAD_DIM, DN_HEAD_DIM), F32),
            pltpu.VMEM((tc + tx, wd), F32),
            pltpu.VMEM((nslot, nchain, DN_CHUNK, DN_HEAD_DIM), BF16),
            pltpu.VMEM((nslot, nchain, 2 * DN_CHUNK, DN_HEAD_DIM), BF16),
            pltpu.VMEM((nslot, nchain, DN_HEAD_DIM, DN_CHUNK), BF16),
            pltpu.VMEM((nslot, nchain, DN_CHUNK, DN_CHUNK), BF16),
            pltpu.VMEM((nslot, nchain, 8, LANE), F32)],
        compiler_params=_cparams(("parallel", "arbitrary")),
        name="deltanet",
    )(qkv_x, qkv_x, qkv_x, qkv_c, qkv_c, qkv_c, ab_x, ab_c, gate, alog_g, dtb_g, hng)


def _outproj_kernel(x_ref, oa_ref, ob_ref, w_ref, m_ref, out_ref):
    half = oa_ref.shape[1]
    y = _dot(oa_ref[...], w_ref[:half, :]) + _dot(ob_ref[...], w_ref[half:, :])
    out_ref[...] = x_ref[...] + m_ref[0, 5:6, :] * y


def _outproj_call(x2d, oa, ob, w, m3, *, rows_per_mod, tm):
    n, d = x2d.shape
    half = oa.shape[1]
    tiles_per_mod = rows_per_mod // tm
    return pl.pallas_call(
        _outproj_kernel,
        grid=(n // tm,),
        in_specs=[pl.BlockSpec((tm, d), lambda i: (i, 0)),
                  pl.BlockSpec((tm, half), lambda i: (i, 0)),
                  pl.BlockSpec((tm, half), lambda i: (i, 0)),
                  pl.BlockSpec((2 * half, d), lambda i: (0, 0), pipeline_mode=pl.Buffered(1)),
                  pl.BlockSpec((1, N_MOD, d), lambda i: (i // tiles_per_mod, 0, 0))],
        out_specs=pl.BlockSpec((tm, d), lambda i: (i, 0)),
        out_shape=jax.ShapeDtypeStruct((n, d), F32),
        compiler_params=_cparams(("parallel",)),
        name="outproj",
    )(x2d, oa, ob, w, m3)


def _chain_rows(p, hg):
    nhg = DN_HEADS // hg
    rows = jnp.moveaxis(p.reshape(2, nhg, hg), 1, 0).reshape(nhg, 1, 2 * hg)
    return jnp.pad(rows, ((0, 0), (0, 0), (0, LANE - 2 * hg)))


FFN_TM = 512
FFN_TF = 512
PROJ_TM = 512
OUT_TM = 512
DN_HG = 4
DN_UNROLL = 4
MOD_ROWS = 16


def kernel(x, c, ctx, c_ctx, w_mod, b_mod, norm_g, ffn1_w_in, ffn1_w_out, w_in, conv_w, a_log, dt_bias, head_norm_g, spatial_w, spatial_b, mlp_norm_g, w_out, ffn2_w_in, ffn2_w_out, final_g):
    bsz, t, d = x.shape
    tc = ctx.shape[1]
    assert w_mod.shape[0] == 1, "single-layer block"
    assert bsz + 1 <= MOD_ROWS

    c_all = jnp.concatenate([c, c_ctx[None, :], jnp.zeros((MOD_ROWS - bsz - 1, d), F32)], axis=0)
    m3 = _mod_call(c_all, w_mod[0], b_mod).reshape(MOD_ROWS, N_MOD, d)

    g0, g1, g2 = norm_g[0, 0:1], norm_g[0, 1:2], norm_g[0, 2:3]
    w_ffn1 = _prep_ffn_weights(ffn1_w_in[0], ffn1_w_out[0], FFN_TF)
    w_ffn2 = _prep_ffn_weights(ffn2_w_in[0], ffn2_w_out[0], FFN_TF)

    x1, h_x = _ffn_call(x.reshape(bsz * t, d), m3, g0, w_ffn1, g1, mode="ffn1", mod_base=0,
                        rows_per_mod=t, mod_offset=0, tm=FFN_TM, tf=FFN_TF)
    (h_c,) = _ffn_call(ctx.reshape(bsz * tc, d), m3, g0, w_ffn1, g1, mode="ffn1ctx", mod_base=0,
                       rows_per_mod=bsz * tc, mod_offset=bsz, tm=FFN_TM, tf=FFN_TF)

    w_main, w_ab = _prep_inproj_weights(w_in[0], DN_HG)
    sb = jnp.broadcast_to(spatial_b[0][:, :, None], (MLP_GROUPS, MLP_CHUNK, LANE))
    qkv_x, ab_x, gate_x, o_b = _inproj_call(h_x, w_main, w_ab, conv_w[0], spatial_w[0].astype(BF16), sb, mlp_norm_g,
                                            tm=PROJ_TM, seq_len=t, with_mlp=True)
    qkv_c, ab_c = _inproj_call(h_c, w_main, w_ab, conv_w[0], None, None, None,
                               tm=min(PROJ_TM, tc), seq_len=tc, with_mlp=False)

    wq = DN_HEADS * DN_HEAD_DIM
    o_a = _dn_call(qkv_x.reshape(bsz, t, 3 * wq), qkv_c.reshape(bsz, tc, 3 * wq),
                   ab_x.reshape(bsz, t, -1), ab_c.reshape(bsz, tc, -1), gate_x.reshape(bsz, t, wq),
                   _chain_rows(a_log[0], DN_HG), _chain_rows(dt_bias[0], DN_HG), head_norm_g,
                   hg=DN_HG, unroll=DN_UNROLL)

    x2 = _outproj_call(x1, o_a.reshape(bsz * t, wq), o_b, w_out[0].astype(BF16), m3, rows_per_mod=t, tm=OUT_TM)
    (out,) = _ffn_call(x2, m3, g2, w_ffn2, final_g[None, :], mode="ffn2", mod_base=6,
                       rows_per_mod=t, mod_offset=0, tm=FFN_TM, tf=FFN_TF)
    return out.reshape(bsz, t, d)
```

```python
import functools

import jax
import jax.numpy as jnp
from jax import lax
from jax.experimental import pallas as pl
from jax.experimental.pallas import tpu as pltpu

F32 = jnp.float32
BF16 = jnp.bfloat16

EPS = 1e-6
LANE = 128
SUBLANE = 8
N_MOD = 9
DN_HEADS = 8
DN_HEAD_DIM = 128
DN_CHUNK = 64
CONV_K = 5
MLP_GROUPS = 8
MLP_CHUNK = 128
PROJ_CB = 1024
HALO = 16
V7X_VMEM_LIMIT = 60000 * 1024
CAST_TB = 256
SKIP_CAST_TB = 512
ROW_CHUNK = 16
ROW_UNROLL = 8


def _cparams(sem):
    return pltpu.CompilerParams(dimension_semantics=sem, vmem_limit_bytes=V7X_VMEM_LIMIT)


def _rms(x, g):
    return x * lax.rsqrt(jnp.mean(x * x, axis=-1, keepdims=True) + EPS) * g


def _silu(x):
    return x * jax.nn.sigmoid(x)


def _dot(a, b):
    return jnp.dot(a, b, preferred_element_type=F32)


def _dot_nt(a, b):
    return lax.dot_general(a, b, (((1,), (1,)), ((), ())), preferred_element_type=F32)


def _mod_kernel(c_ref, w_ref, b_ref, o_ref):
    s = _silu(c_ref[...]).astype(BF16)
    o_ref[...] = _dot(s, w_ref[...].astype(BF16)) + b_ref[...]


def _mod_call(c_all, w_mod, b_mod, tn):
    rows, d = c_all.shape
    n = w_mod.shape[1]
    return pl.pallas_call(
        _mod_kernel,
        grid=(n // tn,),
        in_specs=[pl.BlockSpec((rows, d), lambda j: (0, 0)),
                  pl.BlockSpec((d, tn), lambda j: (0, j)),
                  pl.BlockSpec((1, tn), lambda j: (0, j))],
        out_specs=pl.BlockSpec((rows, tn), lambda j: (0, j)),
        out_shape=jax.ShapeDtypeStruct((rows, n), F32),
        compiler_params=_cparams(("arbitrary",)),
        name="mod",
    )(c_all, w_mod, b_mod)


def _ffn_kernel(*refs, mode, mod_base, nf):
    if mode == "ffn2":
        x_ref, m_ref, g_ref, wa_ref, wb_ref, wo_ref, g2_ref, out_ref, h_s, acc_s, vec_s = refs
    elif mode == "ffn1":
        x_ref, m_ref, g_ref, wa_ref, wb_ref, wo_ref, g2_ref, out_ref, h2_ref, h_s, acc_s, vec_s = refs
    else:
        x_ref, m_ref, g_ref, wa_ref, wb_ref, wo_ref, g2_ref, h2_ref, h_s, acc_s, vec_s = refs
    j = pl.program_id(1)
    tm, d = h_s.shape

    def mvec(k):
        return m_ref[0, k:k + 1, :]

    def tile_rows(v):
        return jnp.concatenate([v] * (ROW_CHUNK // SUBLANE), axis=0)

    def norm_scale(x, gain, shift=None):
        y = x * lax.rsqrt(jnp.mean(x * x, axis=-1, keepdims=True) + EPS) * tile_rows(gain)
        return y if shift is None else y + tile_rows(shift)

    def row_loop(fn):
        def body(r, carry):
            fn(pl.ds(pl.multiple_of(r * ROW_CHUNK, ROW_CHUNK), ROW_CHUNK))
            return carry
        lax.fori_loop(0, tm // ROW_CHUNK, body, 0, unroll=ROW_UNROLL)

    @pl.when(j == 0)
    def _():
        def put(k, v):
            vec_s[k] = jnp.broadcast_to(v, (SUBLANE, d))
        put(0, g_ref[...] * (1.0 + mvec(mod_base + 1)))
        put(1, mvec(mod_base))
        put(2, 0.5 * mvec(mod_base + 2))
        if mode == "ffn2":
            put(3, g2_ref[...])
        else:
            put(3, g2_ref[...] * (1.0 + mvec(mod_base + 4)))
            put(4, mvec(mod_base + 3))

        def prologue(rows):
            h_s[rows, :] = norm_scale(x_ref[rows, :], vec_s[0], vec_s[1]).astype(BF16)
            acc_s[rows, :] = jnp.zeros((ROW_CHUNK, d), F32)

        row_loop(prologue)

    h = h_s[...]
    act = (_silu(_dot(h, wa_ref[...])) * _dot(h, wb_ref[...])).astype(BF16)
    acc_s[...] += _dot(act, wo_ref[...])

    @pl.when(j == nf - 1)
    def _():
        def epilogue(rows):
            xn = x_ref[rows, :] + tile_rows(vec_s[2]) * acc_s[rows, :]
            if mode == "ffn2":
                out_ref[rows, :] = norm_scale(xn, vec_s[3])
            else:
                if mode == "ffn1":
                    out_ref[rows, :] = xn
                h2_ref[rows, :] = norm_scale(xn, vec_s[3], vec_s[4]).astype(BF16)

        row_loop(epilogue)


def _split_cast_kernel(a_ref, b_ref, wa_ref, wb_ref, tail_s, *, f, tb, nb):
    s = pl.program_id(0)
    rem = f % tb
    lane = lax.broadcasted_iota(jnp.int32, a_ref.shape, 1)
    a_blk = jnp.minimum(s, nb - 1)
    wa_ref[...] = jnp.where(a_blk * tb + lane < f, a_ref[...], 0.0).astype(BF16)

    @pl.when(s == 0)
    def _():
        wb_ref[...] = jnp.zeros_like(wb_ref)

    @pl.when(s > 0)
    def _():
        b = jnp.concatenate([tail_s[...], b_ref[:, :rem]], axis=1) if rem else tail_s[...]
        wb_ref[...] = jnp.where((s - 1) * tb + lane < f, b, 0.0).astype(BF16)

    tail_s[...] = b_ref[:, rem:]


def _split_cast_call(w_in, f_pad, tb):
    d, f2 = w_in.shape
    f = f2 // 2
    assert f2 % tb == 0 and f_pad % tb == 0 and (f % tb) % LANE == 0
    nb = f_pad // tb
    last = f2 // tb - 1
    return pl.pallas_call(
        functools.partial(_split_cast_kernel, f=f, tb=tb, nb=nb),
        grid=(nb + 1,),
        in_specs=[pl.BlockSpec((d, tb), lambda s: (0, jnp.minimum(s, nb - 1))),
                  pl.BlockSpec((d, tb), lambda s: (0, jnp.minimum(f // tb + s, last)))],
        out_specs=[pl.BlockSpec((d, tb), lambda s: (0, jnp.minimum(s, nb - 1))),
                   pl.BlockSpec((d, tb), lambda s: (0, jnp.maximum(s - 1, 0)))],
        out_shape=[jax.ShapeDtypeStruct((d, f_pad), BF16)] * 2,
        scratch_shapes=[pltpu.VMEM((d, tb - f % tb), F32)],
        compiler_params=_cparams(("arbitrary",)),
        name="ffn_weight_cast",
    )(w_in, w_in)


def _prep_ffn_weights(w_in, w_out, tf):
    f = w_in.shape[1] // 2
    pad = -f % tf
    wa, wb = _split_cast_call(w_in, f + pad, CAST_TB)
    wo = jnp.pad(w_out.astype(BF16), ((0, pad), (0, 0)))
    return wa, wb, wo


def _ffn_call(x2d, m3, g, w3, g2, *, mode, mod_base, rows_per_mod, mod_offset, tm, tf):
    r, d = x2d.shape
    wa, wb, wo = w3
    nf = wo.shape[0] // tf
    tiles_per_mod = rows_per_mod // tm
    kern = functools.partial(_ffn_kernel, mode=mode, mod_base=mod_base, nf=nf)
    x_spec = pl.BlockSpec((tm, d), lambda i, j: (i, 0))
    in_specs = [x_spec,
                pl.BlockSpec((1, N_MOD, d), lambda i, j: (mod_offset + i // tiles_per_mod, 0, 0)),
                pl.BlockSpec((1, d), lambda i, j: (0, 0)),
                pl.BlockSpec((d, tf), lambda i, j: (0, j)),
                pl.BlockSpec((d, tf), lambda i, j: (0, j)),
                pl.BlockSpec((tf, d), lambda i, j: (j, 0)),
                pl.BlockSpec((1, d), lambda i, j: (0, 0))]
    args = [x2d, m3, g, wa, wb, wo, g2]
    o_f32 = (jax.ShapeDtypeStruct((r, d), F32), x_spec)
    o_bf = (jax.ShapeDtypeStruct((r, d), BF16), x_spec)
    outs = {"ffn2": [o_f32], "ffn1": [o_f32, o_bf], "ffn1ctx": [o_bf]}[mode]
    return pl.pallas_call(
        kern,
        grid=(r // tm, nf),
        in_specs=in_specs,
        out_specs=[o[1] for o in outs],
        out_shape=[o[0] for o in outs],
        scratch_shapes=[pltpu.VMEM((tm, d), BF16), pltpu.VMEM((tm, d), F32), pltpu.VMEM((5, SUBLANE, d), F32)],
        compiler_params=_cparams(("parallel", "arbitrary")),
        name=mode,
    )(*args)


def _inproj_kernel(*refs, tm, seq_tiles, with_mlp):
    if with_mlp:
        (h_ref, hp_ref, hn_ref, w_ref, wab_ref, cw_ref, sw_ref, sb_ref, mng_ref,
         qkv_ref, ab_ref, gate_ref, ob_ref) = refs
    else:
        h_ref, hp_ref, hn_ref, w_ref, wab_ref, cw_ref, qkv_ref, ab_ref = refs
    cb = PROJ_CB
    h = h_ref[...]

    def block(jb):
        return _dot(h, w_ref[:, jb * cb:(jb + 1) * cb])

    ab_ref[...] = _dot(h, wab_ref[...])

    i = pl.program_id(0)
    pos = i % seq_tiles
    keep_prev = jnp.where(pos > 0, 1.0, 0.0).astype(BF16)
    keep_next = jnp.where(pos < seq_tiles - 1, 1.0, 0.0).astype(BF16)
    hext = jnp.concatenate([hp_ref[...] * keep_prev, h, hn_ref[...] * keep_next], axis=0)
    first = HALO - CONV_K // 2
    for jb in range(3):
        zext = _dot(hext, w_ref[:, jb * cb:(jb + 1) * cb])
        for rc in range(tm // DN_CHUNK):
            win = zext[rc * DN_CHUNK:rc * DN_CHUNK + DN_CHUNK + 2 * HALO]
            y = cw_ref[0:1, jb * cb:(jb + 1) * cb] * win[first:first + DN_CHUNK]
            for t in range(1, CONV_K):
                y = y + cw_ref[t:t + 1, jb * cb:(jb + 1) * cb] * win[first + t:first + t + DN_CHUNK]
            y = _silu(y)
            if jb < 2:
                parts = []
                for hd in range(cb // DN_HEAD_DIM):
                    yh = y[:, hd * DN_HEAD_DIM:(hd + 1) * DN_HEAD_DIM]
                    yh = yh * lax.rsqrt(jnp.sum(yh * yh, axis=-1, keepdims=True) + EPS)
                    parts.append(yh * (DN_HEAD_DIM ** -0.5) if jb == 0 else yh)
                y = jnp.concatenate(parts, axis=1)
            qkv_ref[rc * DN_CHUNK:(rc + 1) * DN_CHUNK, jb * cb:(jb + 1) * cb] = y.astype(BF16)
    if not with_mlp:
        return
    gate_ref[...] = block(3).astype(BF16)
    u = jax.nn.gelu(block(4))
    v = jax.nn.gelu(block(5))
    for g in range(MLP_GROUPS):
        cols = slice(g * LANE, (g + 1) * LANE)
        vn = _rms(v[:, cols], mng_ref[:, cols]).astype(BF16)
        for ci in range(tm // MLP_CHUNK):
            rows = slice(ci * MLP_CHUNK, (ci + 1) * MLP_CHUNK)
            s = _dot(sw_ref[g], vn[rows, :]) + sb_ref[g]
            ob_ref[rows, cols] = (u[rows, cols] * s).astype(BF16)


def _skip_cast_kernel(a_ref, b_ref, o_ref, *, lo_blocks, skip):
    j = pl.program_id(0)

    @pl.when(j < lo_blocks)
    def _():
        o_ref[...] = a_ref[...].astype(BF16)

    @pl.when(j >= lo_blocks)
    def _():
        tb = a_ref.shape[1]
        both = jnp.concatenate([a_ref[...], b_ref[...]], axis=1)
        o_ref[...] = pltpu.roll(both, 2 * tb - skip, 1)[:, :tb].astype(BF16)


def _skip_cast_call(w, lo, skip, tb):
    d, n = w.shape
    n_out = n - skip
    assert lo % tb == 0 and n_out % tb == 0 and 0 < skip < tb and tb & (tb - 1) == 0
    last = -(-n // tb) - 1
    return pl.pallas_call(
        functools.partial(_skip_cast_kernel, lo_blocks=lo // tb, skip=skip),
        grid=(n_out // tb,),
        in_specs=[pl.BlockSpec((d, tb), lambda j: (0, j)),
                  pl.BlockSpec((d, tb), lambda j: (0, jnp.minimum(j + 1, last)))],
        out_specs=pl.BlockSpec((d, tb), lambda j: (0, j)),
        out_shape=jax.ShapeDtypeStruct((d, n_out), BF16),
        compiler_params=_cparams(("parallel",)),
        name="inproj_weight_cast",
    )(w, w)


def _prep_inproj_weights(w_in, hg):
    wq = DN_HEADS * DN_HEAD_DIM
    n_ab = 4 * DN_HEADS
    w_main = _skip_cast_call(w_in, 4 * wq, n_ab, SKIP_CAST_TB)
    w = w_in[:, 4 * wq:4 * wq + n_ab].astype(BF16)
    wa = w[:, :2 * DN_HEADS].reshape(-1, 2, DN_HEADS // hg, hg)
    wb = w[:, 2 * DN_HEADS:].reshape(-1, 2, DN_HEADS // hg, hg)
    grp = jnp.concatenate([jnp.moveaxis(wa, 2, 1), jnp.moveaxis(wb, 2, 1)], axis=2)
    grp = grp.reshape(w.shape[0], DN_HEADS // hg, 4 * hg)
    w_ab = jnp.pad(grp, ((0, 0), (0, 0), (0, LANE - 4 * hg))).reshape(w.shape[0], -1)
    return w_main, w_ab


def _inproj_call(h2, w_main, w_ab, conv_w, sw, sb, mng, *, tm, seq_len, with_mlp):
    n, d = h2.shape
    cb = PROJ_CB
    nab = w_ab.shape[1]
    ncb = 6 if with_mlp else 3
    assert seq_len % tm == 0 and tm % HALO == 0
    kern = functools.partial(_inproj_kernel, tm=tm, seq_tiles=seq_len // tm, with_mlp=with_mlp)
    once = dict(pipeline_mode=pl.Buffered(1))
    hb = tm // HALO
    in_specs = [pl.BlockSpec((tm, d), lambda i: (i, 0)),
                pl.BlockSpec((HALO, d), lambda i: (jnp.maximum(i * hb - 1, 0), 0)),
                pl.BlockSpec((HALO, d), lambda i: (jnp.minimum((i + 1) * hb, n // HALO - 1), 0)),
                pl.BlockSpec((d, ncb * cb), lambda i: (0, 0), **once),
                pl.BlockSpec((d, nab), lambda i: (0, 0), **once),
                pl.BlockSpec((CONV_K, 3 * cb), lambda i: (0, 0), **once)]
    out_specs = [pl.BlockSpec((tm, 3 * cb), lambda i: (i, 0)),
                 pl.BlockSpec((tm, nab), lambda i: (i, 0))]
    out_shape = [jax.ShapeDtypeStruct((n, 3 * cb), BF16), jax.ShapeDtypeStruct((n, nab), F32)]
    args = [h2, h2, h2, w_main, w_ab, conv_w]
    if with_mlp:
        in_specs += [pl.BlockSpec((MLP_GROUPS, MLP_CHUNK, MLP_CHUNK), lambda i: (0, 0, 0), **once),
                     pl.BlockSpec((MLP_GROUPS, MLP_CHUNK, LANE), lambda i: (0, 0, 0), **once),
                     pl.BlockSpec((1, cb), lambda i: (0, 0))]
        out_specs += [pl.BlockSpec((tm, cb), lambda i: (i, 0))] * 2
        out_shape += [jax.ShapeDtypeStruct((n, cb), BF16)] * 2
        args += [sw, sb, mng]
    return pl.pallas_call(
        kern,
        grid=(n // tm,),
        in_specs=in_specs,
        out_specs=out_specs,
        out_shape=out_shape,
        compiler_params=_cparams(("parallel",)),
        name="inproj" if with_mlp else "inproj_ctx",
    )(*args)


def _tri_inverse(lmats, uppers, ii, jj):
    eye = (ii == jj).astype(F32)
    b16 = (ii // 16) == (jj // 16)
    b32 = (ii // 32) == (jj // 32)
    m1 = {False: b32 & ((ii // 16) > (jj // 16)), True: b32 & ((ii // 16) < (jj // 16))}
    m2 = {False: (ii // 32) > (jj // 32), True: (ii // 32) < (jj // 32)}
    a1 = [jnp.where(b16, l, 0.0) for l in lmats]
    a1b = [a.astype(BF16) for a in a1]
    a2b = [_dot(a, a).astype(BF16) for a in a1b]
    yield
    a4b = [_dot(a, a).astype(BF16) for a in a2b]
    yield
    a8b = [_dot(a, a).astype(BF16) for a in a4b]
    yield
    p = [eye - a for a in a1]
    for ab in (a2b, a4b, a8b):
        p = [pi + _dot(pi.astype(BF16), a) for pi, a in zip(p, ab)]
        yield
    for masks in (m1, m2):
        pb = [pi.astype(BF16) for pi in p]
        xm = [_dot(jnp.where(masks[u], l, 0.0).astype(BF16), b).astype(BF16)
              for l, u, b in zip(lmats, uppers, pb)]
        yield
        p = [pi - _dot(b, x) for pi, b, x in zip(p, pb, xm)]
        yield
    return p


def _run_interleaved(*gens):
    live = list(gens)
    while live:
        for g in list(live):
            try:
                next(g)
            except StopIteration:
                live.remove(g)


def _dn_kernel(qx_ref, kx_ref, vx_ref, qc_ref, kc_ref, vc_ref, abx_ref, abc_ref, gate_ref,
               alog_ref, dtb_ref, hng_ref,
               out_ref,
               col_s, row_s, st_s, o_s, pu_s, pwq_s, pkt_s, pqk_s, pgl_s,
               *, hg, tx, tc, unroll):
    C = DN_CHUNK
    D = DN_HEAD_DIM
    ncc, ncx = tc // C, tx // C
    nch = ncc + ncx
    nchain = 2 * hg

    ii = lax.broadcasted_iota(jnp.int32, (C, C), 0)
    jj = lax.broadcasted_iota(jnp.int32, (C, C), 1)
    lane_t = lax.broadcasted_iota(jnp.int32, (C, LANE), 1)

    tril = (ii >= jj).astype(F32)
    triu = (ii <= jj).astype(F32)
    ones = jnp.ones((C, C), F32)
    sum_mats = jnp.concatenate([tril, triu, ones], axis=0).astype(BF16)
    neg_a = -jnp.exp(alog_ref[0])
    dtb = dtb_ref[0]

    def gate_chunk(ab_ref, n, chunk0):
        t = ab_ref[0, n * C:(n + 1) * C, :]
        g = neg_a * jax.nn.softplus(t + dtb)
        g = jnp.where(lane_t < nchain, g, 0.0)
        hi = g.astype(BF16)
        r1 = g - hi.astype(F32)
        mid = r1.astype(BF16)
        lo = (r1 - mid.astype(F32)).astype(BF16)
        cs = _dot(sum_mats, jnp.concatenate([hi, mid, lo], axis=1))
        cs = cs[:, :LANE] + cs[:, LANE:2 * LANE] + cs[:, 2 * LANE:]
        gt = cs[2 * C:]
        gc = jnp.where(lane_t < hg, cs[:C], cs[C:2 * C])
        beta = pltpu.roll(jax.nn.sigmoid(t), LANE - nchain, 1)
        col_s[chunk0 + n, 0] = gc
        col_s[chunk0 + n, 1] = beta
        col_s[chunk0 + n, 2] = gt
        row_s[chunk0 + n] = gc.T[0:row_s.shape[1], :]

    def gate_gen(ab_ref, n_ch, chunk0, per_yield=4):
        for n in range(n_ch):
            gate_chunk(ab_ref, n, chunk0)
            if n % per_yield == per_yield - 1:
                yield

    _run_interleaved(gate_gen(abc_ref, ncc, 0))

    def chunk_of(s):
        cf = s
        cb = jnp.where(s < ncc, ncc - 1 - s, nch + ncc - 1 - s)
        return cf, cb

    def bcast(tile, r):
        return jnp.broadcast_to(tile[:, r:r + 1], (C, LANE))

    def stage1(it, par, in_context):
        q_ref, k_ref, v_ref, c0 = (qc_ref, kc_ref, vc_ref, 0) if in_context else (qx_ref, kx_ref, vx_ref, ncc)
        jobs = []
        for u in range(unroll):
            cf, cb = chunk_of(it * unroll + u)
            for d in range(2):
                c = cf if d == 0 else cb
                gc_t = col_s[c, 0]
                be_t = col_s[c, 1]
                gt_t = col_s[c, 2]
                e1_t = jnp.exp(gc_t)
                tiles = dict(gc=gc_t, be=be_t, e1=e1_t, ca=be_t * e1_t, e2=jnp.exp(gt_t - gc_t),
                             gl=jnp.exp(gt_t), rows=row_s[c])
                for hl in range(hg):
                    jobs.append((par * unroll + u, d * hg + hl, pl.multiple_of((c - c0) * C, C), tiles, d, hl))
        tri = {0: ii >= jj, 1: ii <= jj}
        strict = {0: ii > jj, 1: ii < jj}

        def rd(ref, job):
            return ref[0, pl.ds(job[2], C), job[5] * D:(job[5] + 1) * D]

        qn = [rd(q_ref, j) for j in jobs]
        kn = [rd(k_ref, j) for j in jobs]
        gmat = [_dot_nt(jnp.concatenate([q, k], axis=0), k) for q, k in zip(qn, kn)]
        yield
        dec = [jnp.exp(jnp.where(tri[j[4]], bcast(j[3]["gc"], j[1])[:, :C] - j[3]["rows"][j[1]:j[1] + 1, :], -1e30))
               for j in jobs]
        lmat = [jnp.where(strict[j[4]], g[C:] * dc, 0.0) * bcast(j[3]["be"], j[1])[:, :C]
                for j, g, dc in zip(jobs, gmat, dec)]
        for j, g, dc in zip(jobs, gmat, dec):
            pqk_s[j[0], j[1]] = (g[:C] * dc).astype(BF16)
        tinv = yield from _tri_inverse(lmat, [bool(j[4]) for j in jobs], ii, jj)
        knf = [k.astype(F32) for k in kn]
        rhs = [jnp.concatenate([(rd(v_ref, j).astype(F32) * bcast(j[3]["be"], j[1])).astype(BF16),
                                (kf * bcast(j[3]["ca"], j[1])).astype(BF16)], axis=1)
               for j, kf in zip(jobs, knf)]
        uw = [_dot(t.astype(BF16), r) for t, r in zip(tinv, rhs)]
        yield
        for j, x, q, kf in zip(jobs, uw, qn, knf):
            slot, r = j[0], j[1]
            qd = (q.astype(F32) * bcast(j[3]["e1"], r)).astype(BF16)
            pu_s[slot, r] = x[:, :D].astype(BF16)
            pwq_s[slot, r] = jnp.concatenate([x[:, D:].astype(BF16), qd], axis=0)
            pkt_s[slot, r] = (kf * bcast(j[3]["e2"], r)).T.astype(BF16)
            pgl_s[slot, r] = jnp.broadcast_to(j[3]["gl"][0:SUBLANE, r:r + 1], (SUBLANE, LANE))

    def stage2(it, par):
        for u in range(unroll):
            slot = par * unroll + u
            cf, cb = chunk_of(it * unroll + u)
            chains = [(d * hg + hl, pl.multiple_of((cb if d else cf) * C, C), hl)
                      for d in range(2) for hl in range(hg)]
            st = [st_s[r] for r, _, _ in chains]
            ws = [_dot(pwq_s[slot, r], s.astype(BF16)) for (r, _, _), s in zip(chains, st)]
            yield
            vn = [(pu_s[slot, r].astype(F32) - w[:C]).astype(BF16) for (r, _, _), w in zip(chains, ws)]
            o = [w[C:] + _dot(pqk_s[slot, r], v) for (r, _, _), w, v in zip(chains, ws, vn)]
            kv = [_dot(pkt_s[slot, r], v) for (r, _, _), v in zip(chains, vn)]
            yield
            for (r, row0, hl), s, x, y in zip(chains, st, kv, o):
                st_s[r] = s * jnp.concatenate([pgl_s[slot, r]] * (D // SUBLANE), axis=0) + x
                o_s[pl.ds(row0, C), hl * D:(hl + 1) * D] += y

    st_s[...] = jnp.zeros_like(st_s)
    o_s[...] = jnp.zeros_like(o_s)
    _run_interleaved(stage1(0, 0, True), gate_gen(abx_ref, ncx, ncc))

    def loop_body(it, carry):
        _run_interleaved(stage1(it, it % 2, False), stage2(it - 1, (it - 1) % 2))
        return carry

    n_it = nch // unroll
    lax.fori_loop(1, n_it, loop_body, 0)
    _run_interleaved(stage2(n_it - 1, (n_it - 1) % 2))

    hn = hng_ref[...]
    for hl in range(hg):
        cols = slice(hl * D, (hl + 1) * D)
        o = o_s[tc:tc + tx, cols]
        zg = gate_ref[0, :, cols].astype(F32)
        out_ref[0, :, cols] = (_rms(o, hn) * _silu(zg)).astype(BF16)


def _dn_call(qkv_x, qkv_c, ab_x, ab_c, gate, alog_g, dtb_g, hng, *, hg, unroll):
    b, tx, w3 = qkv_x.shape
    tc = qkv_c.shape[1]
    assert tc // DN_CHUNK == unroll and (tx // DN_CHUNK) % unroll == 0
    nslot = 2 * unroll
    nhg = DN_HEADS // hg
    wd = hg * DN_HEAD_DIM
    nch = (tx + tc) // DN_CHUNK
    nchain = 2 * hg
    kern = functools.partial(_dn_kernel, hg=hg, tx=tx, tc=tc, unroll=unroll)

    def col(t, off):
        return pl.BlockSpec((1, t, wd), lambda bi, gi: (bi, 0, off * nhg + gi))

    def ab(t):
        return pl.BlockSpec((1, t, LANE), lambda bi, gi: (bi, 0, gi), pipeline_mode=pl.Buffered(1))

    small = pl.BlockSpec((1, 1, LANE), lambda bi, gi: (gi, 0, 0))
    return pl.pallas_call(
        kern,
        grid=(b, nhg),
        in_specs=[col(tx, 0), col(tx, 1), col(tx, 2), col(tc, 0), col(tc, 1), col(tc, 2),
                  ab(tx), ab(tc),
                  pl.BlockSpec((1, tx, wd), lambda bi, gi: (bi, 0, gi), pipeline_mode=pl.Buffered(1)),
                  small, small,
                  pl.BlockSpec((1, LANE), lambda bi, gi: (0, 0))],
        out_specs=pl.BlockSpec((1, tx, wd), lambda bi, gi: (bi, 0, gi)),
        out_shape=jax.ShapeDtypeStruct((b, tx, DN_HEADS * DN_HEAD_DIM), BF16),
        scratch_shapes=[
            pltpu.VMEM((nch, 3, DN_CHUNK, LANE), F32),
            pltpu.VMEM((nch, -(-nchain // SUBLANE) * SUBLANE, DN_CHUNK), F32),
            pltpu.VMEM((nchain, DN_HEAD_DIM, DN_HEAD_DIM), F32),
            pltpu.VMEM((tc + tx, wd), F32),
            pltpu.VMEM((nslot, nchain, DN_CHUNK, DN_HEAD_DIM), BF16),
            pltpu.VMEM((nslot, nchain, 2 * DN_CHUNK, DN_HEAD_DIM), BF16),
            pltpu.VMEM((nslot, nchain, DN_HEAD_DIM, DN_CHUNK), BF16),
            pltpu.VMEM((nslot, nchain, DN_CHUNK, DN_CHUNK), BF16),
            pltpu.VMEM((nslot, nchain, SUBLANE, LANE), F32)],
        compiler_params=_cparams(("parallel", "arbitrary")),
        name="deltanet",
    )(qkv_x, qkv_x, qkv_x, qkv_c, qkv_c, qkv_c, ab_x, ab_c, gate, alog_g, dtb_g, hng)


def _outproj_kernel(x_ref, oa_ref, ob_ref, w_ref, m_ref, out_ref):
    half = oa_ref.shape[1]
    y = _dot(oa_ref[...], w_ref[:half, :]) + _dot(ob_ref[...], w_ref[half:, :])
    out_ref[...] = x_ref[...] + m_ref[0, 5:6, :] * y


def _outproj_call(x2d, oa, ob, w, m3, *, rows_per_mod, tm):
    n, d = x2d.shape
    half = oa.shape[1]
    tiles_per_mod = rows_per_mod // tm
    return pl.pallas_call(
        _outproj_kernel,
        grid=(n // tm,),
        in_specs=[pl.BlockSpec((tm, d), lambda i: (i, 0)),
                  pl.BlockSpec((tm, half), lambda i: (i, 0)),
                  pl.BlockSpec((tm, half), lambda i: (i, 0)),
                  pl.BlockSpec((2 * half, d), lambda i: (0, 0), pipeline_mode=pl.Buffered(1)),
                  pl.BlockSpec((1, N_MOD, d), lambda i: (i // tiles_per_mod, 0, 0))],
        out_specs=pl.BlockSpec((tm, d), lambda i: (i, 0)),
        out_shape=jax.ShapeDtypeStruct((n, d), F32),
        compiler_params=_cparams(("parallel",)),
        name="outproj",
    )(x2d, oa, ob, w, m3)


def _chain_rows(p, hg):
    nhg = DN_HEADS // hg
    rows = jnp.moveaxis(p.reshape(2, nhg, hg), 1, 0).reshape(nhg, 1, 2 * hg)
    return jnp.pad(rows, ((0, 0), (0, 0), (0, LANE - 2 * hg)))


FFN_TM = 512
FFN_TF = 512
PROJ_TM = 512
OUT_TM = 1024
DN_HG = 4
DN_UNROLL = 4
MOD_ROWS = 16
MOD_TN = 2048


def kernel(x, c, ctx, c_ctx, w_mod, b_mod, norm_g, ffn1_w_in, ffn1_w_out, w_in, conv_w, a_log, dt_bias, head_norm_g, spatial_w, spatial_b, mlp_norm_g, w_out, ffn2_w_in, ffn2_w_out, final_g):
    bsz, t, d = x.shape
    tc = ctx.shape[1]
    assert w_mod.shape[0] == 1, "single-layer block"
    assert bsz + 1 <= MOD_ROWS

    c_all = jnp.concatenate([c, c_ctx[None, :], jnp.zeros((MOD_ROWS - bsz - 1, d), F32)], axis=0)
    m3 = _mod_call(c_all, w_mod[0], b_mod, MOD_TN).reshape(MOD_ROWS, N_MOD, d)

    g0, g1, g2 = norm_g[0, 0:1], norm_g[0, 1:2], norm_g[0, 2:3]
    w_ffn1 = _prep_ffn_weights(ffn1_w_in[0], ffn1_w_out[0], FFN_TF)
    w_ffn2 = _prep_ffn_weights(ffn2_w_in[0], ffn2_w_out[0], FFN_TF)

    x1, h_x = _ffn_call(x.reshape(bsz * t, d), m3, g0, w_ffn1, g1, mode="ffn1", mod_base=0,
                        rows_per_mod=t, mod_offset=0, tm=FFN_TM, tf=FFN_TF)
    (h_c,) = _ffn_call(ctx.reshape(bsz * tc, d), m3, g0, w_ffn1, g1, mode="ffn1ctx", mod_base=0,
                       rows_per_mod=bsz * tc, mod_offset=bsz, tm=FFN_TM, tf=FFN_TF)

    w_main, w_ab = _prep_inproj_weights(w_in[0], DN_HG)
    sb = jnp.broadcast_to(spatial_b[0][:, :, None], (MLP_GROUPS, MLP_CHUNK, LANE))
    qkv_x, ab_x, gate_x, o_b = _inproj_call(h_x, w_main, w_ab, conv_w[0], spatial_w[0].astype(BF16), sb, mlp_norm_g,
                                            tm=PROJ_TM, seq_len=t, with_mlp=True)
    qkv_c, ab_c = _inproj_call(h_c, w_main, w_ab, conv_w[0], None, None, None,
                               tm=min(PROJ_TM, tc), seq_len=tc, with_mlp=False)

    wq = DN_HEADS * DN_HEAD_DIM
    o_a = _dn_call(qkv_x.reshape(bsz, t, 3 * wq), qkv_c.reshape(bsz, tc, 3 * wq),
                   ab_x.reshape(bsz, t, -1), ab_c.reshape(bsz, tc, -1), gate_x.reshape(bsz, t, wq),
                   _chain_rows(a_log[0], DN_HG), _chain_rows(dt_bias[0], DN_HG), head_norm_g,
                   hg=DN_HG, unroll=DN_UNROLL)

    x2 = _outproj_call(x1, o_a.reshape(bsz * t, wq), o_b, w_out[0].astype(BF16), m3, rows_per_mod=t, tm=OUT_TM)
    (out,) = _ffn_call(x2, m3, g2, w_ffn2, final_g[None, :], mode="ffn2", mod_base=6,
                       rows_per_mod=t, mod_offset=0, tm=FFN_TM, tf=FFN_TF)
    return out.reshape(bsz, t, d)
```

```python
import functools

import jax
import jax.numpy as jnp
from jax import lax
from jax.experimental import pallas as pl
from jax.experimental.pallas import tpu as pltpu

F32 = jnp.float32
BF16 = jnp.bfloat16

EPS = 1e-6
LANE = 128
SUBLANE = 8
N_MOD = 9
DN_HEADS = 8
DN_HEAD_DIM = 128
DN_CHUNK = 64
CONV_K = 5
MLP_GROUPS = 8
MLP_CHUNK = 128
PROJ_CB = 1024
HALO = 16
V7X_VMEM_LIMIT = 60000 * 1024
CAST_TB = 256
ROW_CHUNK = 16
ROW_UNROLL = 8


def _cparams(sem):
    return pltpu.CompilerParams(dimension_semantics=sem, vmem_limit_bytes=V7X_VMEM_LIMIT)


def _rms(x, g):
    return x * lax.rsqrt(jnp.mean(x * x, axis=-1, keepdims=True) + EPS) * g


def _silu(x):
    return x * jax.nn.sigmoid(x)


def _dot(a, b):
    return jnp.dot(a, b, preferred_element_type=F32)


def _dot_nt(a, b):
    return lax.dot_general(a, b, (((1,), (1,)), ((), ())), preferred_element_type=F32)


def _mod_kernel(c_ref, w_ref, b_ref, o_ref):
    s = _silu(c_ref[...]).astype(BF16)
    o_ref[...] = _dot(s, w_ref[...].astype(BF16)) + b_ref[...]


def _mod_call(c_all, w_mod, b_mod, tn):
    rows, d = c_all.shape
    n = w_mod.shape[1]
    return pl.pallas_call(
        _mod_kernel,
        grid=(n // tn,),
        in_specs=[pl.BlockSpec((rows, d), lambda j: (0, 0)),
                  pl.BlockSpec((d, tn), lambda j: (0, j)),
                  pl.BlockSpec((1, tn), lambda j: (0, j))],
        out_specs=pl.BlockSpec((rows, tn), lambda j: (0, j)),
        out_shape=jax.ShapeDtypeStruct((rows, n), F32),
        compiler_params=_cparams(("arbitrary",)),
        name="mod",
    )(c_all, w_mod, b_mod)


def _ffn_kernel(*refs, mode, mod_base, nf):
    if mode == "ffn2":
        x_ref, m_ref, g_ref, wa_ref, wb_ref, wo_ref, g2_ref, out_ref, h_s, acc_s, vec_s = refs
    elif mode == "ffn1":
        x_ref, m_ref, g_ref, wa_ref, wb_ref, wo_ref, g2_ref, out_ref, h2_ref, h_s, acc_s, vec_s = refs
    else:
        x_ref, m_ref, g_ref, wa_ref, wb_ref, wo_ref, g2_ref, h2_ref, h_s, acc_s, vec_s = refs
    j = pl.program_id(1)
    tm, d = h_s.shape

    def mvec(k):
        return m_ref[0, k:k + 1, :]

    def tile_rows(v):
        return jnp.concatenate([v] * (ROW_CHUNK // SUBLANE), axis=0)

    def norm_scale(x, gain, shift=None):
        y = x * lax.rsqrt(jnp.mean(x * x, axis=-1, keepdims=True) + EPS) * tile_rows(gain)
        return y if shift is None else y + tile_rows(shift)

    def row_loop(fn):
        def body(r, carry):
            fn(pl.ds(pl.multiple_of(r * ROW_CHUNK, ROW_CHUNK), ROW_CHUNK))
            return carry
        lax.fori_loop(0, tm // ROW_CHUNK, body, 0, unroll=ROW_UNROLL)

    @pl.when(j == 0)
    def _():
        def put(k, v):
            vec_s[k] = jnp.broadcast_to(v, (SUBLANE, d))
        put(0, g_ref[...] * (1.0 + mvec(mod_base + 1)))
        put(1, mvec(mod_base))
        put(2, 0.5 * mvec(mod_base + 2))
        if mode == "ffn2":
            put(3, g2_ref[...])
        else:
            put(3, g2_ref[...] * (1.0 + mvec(mod_base + 4)))
            put(4, mvec(mod_base + 3))

        def prologue(rows):
            h_s[rows, :] = norm_scale(x_ref[rows, :], vec_s[0], vec_s[1]).astype(BF16)
            acc_s[rows, :] = jnp.zeros((ROW_CHUNK, d), F32)

        row_loop(prologue)

    h = h_s[...]
    act = (_silu(_dot(h, wa_ref[...])) * _dot(h, wb_ref[...])).astype(BF16)
    acc_s[...] += _dot(act, wo_ref[...])

    @pl.when(j == nf - 1)
    def _():
        def epilogue(rows):
            xn = x_ref[rows, :] + tile_rows(vec_s[2]) * acc_s[rows, :]
            if mode == "ffn2":
                out_ref[rows, :] = norm_scale(xn, vec_s[3])
            else:
                if mode == "ffn1":
                    out_ref[rows, :] = xn
                h2_ref[rows, :] = norm_scale(xn, vec_s[3], vec_s[4]).astype(BF16)

        row_loop(epilogue)


def _split_cast_kernel(a_ref, b_ref, wa_ref, wb_ref, tail_s, *, f, tb, nb):
    s = pl.program_id(0)
    rem = f % tb
    lane = lax.broadcasted_iota(jnp.int32, a_ref.shape, 1)
    a_blk = jnp.minimum(s, nb - 1)
    wa_ref[...] = jnp.where(a_blk * tb + lane < f, a_ref[...], 0.0).astype(BF16)

    @pl.when(s == 0)
    def _():
        wb_ref[...] = jnp.zeros_like(wb_ref)

    @pl.when(s > 0)
    def _():
        b = jnp.concatenate([tail_s[...], b_ref[:, :rem]], axis=1) if rem else tail_s[...]
        wb_ref[...] = jnp.where((s - 1) * tb + lane < f, b, 0.0).astype(BF16)

    tail_s[...] = b_ref[:, rem:]


def _split_cast_call(w_in, f_pad, tb):
    d, f2 = w_in.shape
    f = f2 // 2
    assert f2 % tb == 0 and f_pad % tb == 0 and (f % tb) % LANE == 0
    nb = f_pad // tb
    last = f2 // tb - 1
    return pl.pallas_call(
        functools.partial(_split_cast_kernel, f=f, tb=tb, nb=nb),
        grid=(nb + 1,),
        in_specs=[pl.BlockSpec((d, tb), lambda s: (0, jnp.minimum(s, nb - 1))),
                  pl.BlockSpec((d, tb), lambda s: (0, jnp.minimum(f // tb + s, last)))],
        out_specs=[pl.BlockSpec((d, tb), lambda s: (0, jnp.minimum(s, nb - 1))),
                   pl.BlockSpec((d, tb), lambda s: (0, jnp.maximum(s - 1, 0)))],
        out_shape=[jax.ShapeDtypeStruct((d, f_pad), BF16)] * 2,
        scratch_shapes=[pltpu.VMEM((d, tb - f % tb), F32)],
        compiler_params=_cparams(("arbitrary",)),
        name="ffn_weight_cast",
    )(w_in, w_in)


def _prep_ffn_weights(w_in, w_out, tf):
    f = w_in.shape[1] // 2
    pad = -f % tf
    wa, wb = _split_cast_call(w_in, f + pad, CAST_TB)
    wo = jnp.pad(w_out.astype(BF16), ((0, pad), (0, 0)))
    return wa, wb, wo


def _ffn_call(x2d, m3, g, w3, g2, *, mode, mod_base, rows_per_mod, mod_offset, tm, tf):
    r, d = x2d.shape
    wa, wb, wo = w3
    nf = wo.shape[0] // tf
    tiles_per_mod = rows_per_mod // tm
    kern = functools.partial(_ffn_kernel, mode=mode, mod_base=mod_base, nf=nf)
    x_spec = pl.BlockSpec((tm, d), lambda i, j: (i, 0))
    in_specs = [x_spec,
                pl.BlockSpec((1, N_MOD, d), lambda i, j: (mod_offset + i // tiles_per_mod, 0, 0)),
                pl.BlockSpec((1, d), lambda i, j: (0, 0)),
                pl.BlockSpec((d, tf), lambda i, j: (0, j)),
                pl.BlockSpec((d, tf), lambda i, j: (0, j)),
                pl.BlockSpec((tf, d), lambda i, j: (j, 0)),
                pl.BlockSpec((1, d), lambda i, j: (0, 0))]
    args = [x2d, m3, g, wa, wb, wo, g2]
    o_f32 = (jax.ShapeDtypeStruct((r, d), F32), x_spec)
    o_bf = (jax.ShapeDtypeStruct((r, d), BF16), x_spec)
    outs = {"ffn2": [o_f32], "ffn1": [o_f32, o_bf], "ffn1ctx": [o_bf]}[mode]
    return pl.pallas_call(
        kern,
        grid=(r // tm, nf),
        in_specs=in_specs,
        out_specs=[o[1] for o in outs],
        out_shape=[o[0] for o in outs],
        scratch_shapes=[pltpu.VMEM((tm, d), BF16), pltpu.VMEM((tm, d), F32), pltpu.VMEM((5, SUBLANE, d), F32)],
        compiler_params=_cparams(("parallel", "arbitrary")),
        name=mode,
    )(*args)


def _inproj_kernel(*refs, tm, seq_tiles, with_mlp):
    if with_mlp:
        (h_ref, hp_ref, hn_ref, w_ref, wab_ref, cw_ref, sw_ref, sb_ref, mng_ref,
         qkv_ref, ab_ref, gate_ref, ob_ref) = refs
    else:
        h_ref, hp_ref, hn_ref, w_ref, wab_ref, cw_ref, qkv_ref, ab_ref = refs
    cb = PROJ_CB
    h = h_ref[...]

    def block(jb):
        return _dot(h, w_ref[:, jb * cb:(jb + 1) * cb])

    ab_ref[...] = _dot(h, wab_ref[...])

    i = pl.program_id(0)
    pos = i % seq_tiles
    keep_prev = jnp.where(pos > 0, 1.0, 0.0).astype(BF16)
    keep_next = jnp.where(pos < seq_tiles - 1, 1.0, 0.0).astype(BF16)
    hext = jnp.concatenate([hp_ref[...] * keep_prev, h, hn_ref[...] * keep_next], axis=0)
    first = HALO - CONV_K // 2
    for jb in range(3):
        zext = _dot(hext, w_ref[:, jb * cb:(jb + 1) * cb])
        for rc in range(tm // DN_CHUNK):
            win = zext[rc * DN_CHUNK:rc * DN_CHUNK + DN_CHUNK + 2 * HALO]
            y = cw_ref[0:1, jb * cb:(jb + 1) * cb] * win[first:first + DN_CHUNK]
            for t in range(1, CONV_K):
                y = y + cw_ref[t:t + 1, jb * cb:(jb + 1) * cb] * win[first + t:first + t + DN_CHUNK]
            y = _silu(y)
            if jb < 2:
                parts = []
                for hd in range(cb // DN_HEAD_DIM):
                    yh = y[:, hd * DN_HEAD_DIM:(hd + 1) * DN_HEAD_DIM]
                    yh = yh * lax.rsqrt(jnp.sum(yh * yh, axis=-1, keepdims=True) + EPS)
                    parts.append(yh * (DN_HEAD_DIM ** -0.5) if jb == 0 else yh)
                y = jnp.concatenate(parts, axis=1)
            qkv_ref[rc * DN_CHUNK:(rc + 1) * DN_CHUNK, jb * cb:(jb + 1) * cb] = y.astype(BF16)
    if not with_mlp:
        return
    gate_ref[...] = block(3).astype(BF16)
    u = jax.nn.gelu(block(4))
    v = jax.nn.gelu(block(5))
    for g in range(MLP_GROUPS):
        cols = slice(g * LANE, (g + 1) * LANE)
        vn = _rms(v[:, cols], mng_ref[:, cols]).astype(BF16)
        for ci in range(tm // MLP_CHUNK):
            rows = slice(ci * MLP_CHUNK, (ci + 1) * MLP_CHUNK)
            s = _dot(sw_ref[g], vn[rows, :]) + sb_ref[g]
            ob_ref[rows, cols] = (u[rows, cols] * s).astype(BF16)


def _prep_inproj_weights(w_in, hg):
    wq = DN_HEADS * DN_HEAD_DIM
    n_ab = 4 * DN_HEADS
    w_main = jnp.concatenate([w_in[:, :4 * wq].astype(BF16), w_in[:, 4 * wq + n_ab:].astype(BF16)], axis=1)
    w = w_in[:, 4 * wq:4 * wq + n_ab].astype(BF16)
    wa = w[:, :2 * DN_HEADS].reshape(-1, 2, DN_HEADS // hg, hg)
    wb = w[:, 2 * DN_HEADS:].reshape(-1, 2, DN_HEADS // hg, hg)
    grp = jnp.concatenate([jnp.moveaxis(wa, 2, 1), jnp.moveaxis(wb, 2, 1)], axis=2)
    grp = grp.reshape(w.shape[0], DN_HEADS // hg, 4 * hg)
    w_ab = jnp.pad(grp, ((0, 0), (0, 0), (0, LANE - 4 * hg))).reshape(w.shape[0], -1)
    return w_main, w_ab


def _inproj_call(h2, w_main, w_ab, conv_w, sw, sb, mng, *, tm, seq_len, with_mlp):
    n, d = h2.shape
    cb = PROJ_CB
    nab = w_ab.shape[1]
    ncb = 6 if with_mlp else 3
    assert seq_len % tm == 0 and tm % HALO == 0
    kern = functools.partial(_inproj_kernel, tm=tm, seq_tiles=seq_len // tm, with_mlp=with_mlp)
    once = dict(pipeline_mode=pl.Buffered(1))
    hb = tm // HALO
    in_specs = [pl.BlockSpec((tm, d), lambda i: (i, 0)),
                pl.BlockSpec((HALO, d), lambda i: (jnp.maximum(i * hb - 1, 0), 0)),
                pl.BlockSpec((HALO, d), lambda i: (jnp.minimum((i + 1) * hb, n // HALO - 1), 0)),
                pl.BlockSpec((d, ncb * cb), lambda i: (0, 0), **once),
                pl.BlockSpec((d, nab), lambda i: (0, 0), **once),
                pl.BlockSpec((CONV_K, 3 * cb), lambda i: (0, 0), **once)]
    out_specs = [pl.BlockSpec((tm, 3 * cb), lambda i: (i, 0)),
                 pl.BlockSpec((tm, nab), lambda i: (i, 0))]
    out_shape = [jax.ShapeDtypeStruct((n, 3 * cb), BF16), jax.ShapeDtypeStruct((n, nab), F32)]
    args = [h2, h2, h2, w_main, w_ab, conv_w]
    if with_mlp:
        in_specs += [pl.BlockSpec((MLP_GROUPS, MLP_CHUNK, MLP_CHUNK), lambda i: (0, 0, 0), **once),
                     pl.BlockSpec((MLP_GROUPS, MLP_CHUNK, LANE), lambda i: (0, 0, 0), **once),
                     pl.BlockSpec((1, cb), lambda i: (0, 0))]
        out_specs += [pl.BlockSpec((tm, cb), lambda i: (i, 0))] * 2
        out_shape += [jax.ShapeDtypeStruct((n, cb), BF16)] * 2
        args += [sw, sb, mng]
    return pl.pallas_call(
        kern,
        grid=(n // tm,),
        in_specs=in_specs,
        out_specs=out_specs,
        out_shape=out_shape,
        compiler_params=_cparams(("parallel",)),
        name="inproj" if with_mlp else "inproj_ctx",
    )(*args)


def _tri_inverse(lmats, uppers, ii, jj):
    eye = (ii == jj).astype(F32)
    b16 = (ii // 16) == (jj // 16)
    b32 = (ii // 32) == (jj // 32)
    m1 = {False: b32 & ((ii // 16) > (jj // 16)), True: b32 & ((ii // 16) < (jj // 16))}
    m2 = {False: (ii // 32) > (jj // 32), True: (ii // 32) < (jj // 32)}
    a1 = [jnp.where(b16, l, 0.0) for l in lmats]
    a1b = [a.astype(BF16) for a in a1]
    a2b = [_dot(a, a).astype(BF16) for a in a1b]
    yield
    a4b = [_dot(a, a).astype(BF16) for a in a2b]
    yield
    a8b = [_dot(a, a).astype(BF16) for a in a4b]
    yield
    p = [eye - a for a in a1]
    for ab in (a2b, a4b, a8b):
        p = [pi + _dot(pi.astype(BF16), a) for pi, a in zip(p, ab)]
        yield
    for masks in (m1, m2):
        pb = [pi.astype(BF16) for pi in p]
        xm = [_dot(jnp.where(masks[u], l, 0.0).astype(BF16), b).astype(BF16)
              for l, u, b in zip(lmats, uppers, pb)]
        yield
        p = [pi - _dot(b, x) for pi, b, x in zip(p, pb, xm)]
        yield
    return p


def _run_interleaved(*gens):
    live = list(gens)
    while live:
        for g in list(live):
            try:
                next(g)
            except StopIteration:
                live.remove(g)


def _dn_kernel(qx_ref, kx_ref, vx_ref, qc_ref, kc_ref, vc_ref, abx_ref, abc_ref, gate_ref,
               alog_ref, dtb_ref, hng_ref,
               out_ref,
               col_s, row_s, st_s, o_s, pu_s, pwq_s, pkt_s, pqk_s, pgl_s,
               *, hg, tx, tc, unroll):
    C = DN_CHUNK
    D = DN_HEAD_DIM
    ncc, ncx = tc // C, tx // C
    nch = ncc + ncx
    nchain = 2 * hg

    ii = lax.broadcasted_iota(jnp.int32, (C, C), 0)
    jj = lax.broadcasted_iota(jnp.int32, (C, C), 1)
    lane_t = lax.broadcasted_iota(jnp.int32, (C, LANE), 1)

    tril = (ii >= jj).astype(F32)
    triu = (ii <= jj).astype(F32)
    ones = jnp.ones((C, C), F32)
    sum_mats = jnp.concatenate([tril, triu, ones], axis=0).astype(BF16)
    neg_a = -jnp.exp(alog_ref[0])
    dtb = dtb_ref[0]

    def gate_chunk(ab_ref, n, chunk0):
        t = ab_ref[0, n * C:(n + 1) * C, :]
        g = neg_a * jax.nn.softplus(t + dtb)
        g = jnp.where(lane_t < nchain, g, 0.0)
        hi = g.astype(BF16)
        r1 = g - hi.astype(F32)
        mid = r1.astype(BF16)
        lo = (r1 - mid.astype(F32)).astype(BF16)
        cs = _dot(sum_mats, jnp.concatenate([hi, mid, lo], axis=1))
        cs = cs[:, :LANE] + cs[:, LANE:2 * LANE] + cs[:, 2 * LANE:]
        gt = cs[2 * C:]
        gc = jnp.where(lane_t < hg, cs[:C], cs[C:2 * C])
        beta = pltpu.roll(jax.nn.sigmoid(t), LANE - nchain, 1)
        col_s[chunk0 + n, 0] = gc
        col_s[chunk0 + n, 1] = beta
        col_s[chunk0 + n, 2] = gt
        row_s[chunk0 + n] = gc.T[0:row_s.shape[1], :]

    def gate_gen(ab_ref, n_ch, chunk0, per_yield=4):
        for n in range(n_ch):
            gate_chunk(ab_ref, n, chunk0)
            if n % per_yield == per_yield - 1:
                yield

    _run_interleaved(gate_gen(abc_ref, ncc, 0))

    def chunk_of(s):
        cf = s
        cb = jnp.where(s < ncc, ncc - 1 - s, nch + ncc - 1 - s)
        return cf, cb

    def bcast(tile, r):
        return jnp.broadcast_to(tile[:, r:r + 1], (C, LANE))

    def stage1(it, par, in_context):
        q_ref, k_ref, v_ref, c0 = (qc_ref, kc_ref, vc_ref, 0) if in_context else (qx_ref, kx_ref, vx_ref, ncc)
        jobs = []
        for u in range(unroll):
            cf, cb = chunk_of(it * unroll + u)
            for d in range(2):
                c = cf if d == 0 else cb
                gc_t = col_s[c, 0]
                be_t = col_s[c, 1]
                gt_t = col_s[c, 2]
                e1_t = jnp.exp(gc_t)
                tiles = dict(gc=gc_t, be=be_t, e1=e1_t, ca=be_t * e1_t, e2=jnp.exp(gt_t - gc_t),
                             gl=jnp.exp(gt_t), rows=row_s[c])
                for hl in range(hg):
                    jobs.append((par * unroll + u, d * hg + hl, pl.multiple_of((c - c0) * C, C), tiles, d, hl))
        tri = {0: ii >= jj, 1: ii <= jj}
        strict = {0: ii > jj, 1: ii < jj}

        def rd(ref, job):
            return ref[0, pl.ds(job[2], C), job[5] * D:(job[5] + 1) * D]

        qn = [rd(q_ref, j) for j in jobs]
        kn = [rd(k_ref, j) for j in jobs]
        gmat = [_dot_nt(jnp.concatenate([q, k], axis=0), k) for q, k in zip(qn, kn)]
        yield
        dec = [jnp.exp(jnp.where(tri[j[4]], bcast(j[3]["gc"], j[1])[:, :C] - j[3]["rows"][j[1]:j[1] + 1, :], -1e30))
               for j in jobs]
        lmat = [jnp.where(strict[j[4]], g[C:] * dc, 0.0) * bcast(j[3]["be"], j[1])[:, :C]
                for j, g, dc in zip(jobs, gmat, dec)]
        for j, g, dc in zip(jobs, gmat, dec):
            pqk_s[j[0], j[1]] = (g[:C] * dc).astype(BF16)
        tinv = yield from _tri_inverse(lmat, [bool(j[4]) for j in jobs], ii, jj)
        knf = [k.astype(F32) for k in kn]
        rhs = [jnp.concatenate([(rd(v_ref, j).astype(F32) * bcast(j[3]["be"], j[1])).astype(BF16),
                                (kf * bcast(j[3]["ca"], j[1])).astype(BF16)], axis=1)
               for j, kf in zip(jobs, knf)]
        uw = [_dot(t.astype(BF16), r) for t, r in zip(tinv, rhs)]
        yield
        for j, x, q, kf in zip(jobs, uw, qn, knf):
            slot, r = j[0], j[1]
            qd = (q.astype(F32) * bcast(j[3]["e1"], r)).astype(BF16)
            pu_s[slot, r] = x[:, :D].astype(BF16)
            pwq_s[slot, r] = jnp.concatenate([x[:, D:].astype(BF16), qd], axis=0)
            pkt_s[slot, r] = (kf * bcast(j[3]["e2"], r)).T.astype(BF16)
            pgl_s[slot, r] = jnp.broadcast_to(j[3]["gl"][0:SUBLANE, r:r + 1], (SUBLANE, LANE))

    def stage2(it, par):
        for u in range(unroll):
            slot = par * unroll + u
            cf, cb = chunk_of(it * unroll + u)
            chains = [(d * hg + hl, pl.multiple_of((cb if d else cf) * C, C), hl)
                      for d in range(2) for hl in range(hg)]
            st = [st_s[r] for r, _, _ in chains]
            ws = [_dot(pwq_s[slot, r], s.astype(BF16)) for (r, _, _), s in zip(chains, st)]
            yield
            vn = [(pu_s[slot, r].astype(F32) - w[:C]).astype(BF16) for (r, _, _), w in zip(chains, ws)]
            o = [w[C:] + _dot(pqk_s[slot, r], v) for (r, _, _), w, v in zip(chains, ws, vn)]
            kv = [_dot(pkt_s[slot, r], v) for (r, _, _), v in zip(chains, vn)]
            yield
            for (r, row0, hl), s, x, y in zip(chains, st, kv, o):
                st_s[r] = s * jnp.concatenate([pgl_s[slot, r]] * (D // SUBLANE), axis=0) + x
                o_s[pl.ds(row0, C), hl * D:(hl + 1) * D] += y

    st_s[...] = jnp.zeros_like(st_s)
    o_s[...] = jnp.zeros_like(o_s)
    _run_interleaved(stage1(0, 0, True), gate_gen(abx_ref, ncx, ncc))

    def loop_body(it, carry):
        _run_interleaved(stage1(it, it % 2, False), stage2(it - 1, (it - 1) % 2))
        return carry

    n_it = nch // unroll
    lax.fori_loop(1, n_it, loop_body, 0)
    _run_interleaved(stage2(n_it - 1, (n_it - 1) % 2))

    hn = hng_ref[...]
    for hl in range(hg):
        cols = slice(hl * D, (hl + 1) * D)
        o = o_s[tc:tc + tx, cols]
        zg = gate_ref[0, :, cols].astype(F32)
        out_ref[0, :, cols] = (_rms(o, hn) * _silu(zg)).astype(BF16)


def _dn_call(qkv_x, qkv_c, ab_x, ab_c, gate, alog_g, dtb_g, hng, *, hg, unroll):
    b, tx, w3 = qkv_x.shape
    tc = qkv_c.shape[1]
    assert tc // DN_CHUNK == unroll and (tx // DN_CHUNK) % unroll == 0
    nslot = 2 * unroll
    nhg = DN_HEADS // hg
    wd = hg * DN_HEAD_DIM
    nch = (tx + tc) // DN_CHUNK
    nchain = 2 * hg
    kern = functools.partial(_dn_kernel, hg=hg, tx=tx, tc=tc, unroll=unroll)

    def col(t, off):
        return pl.BlockSpec((1, t, wd), lambda bi, gi: (bi, 0, off * nhg + gi))

    def ab(t):
        return pl.BlockSpec((1, t, LANE), lambda bi, gi: (bi, 0, gi), pipeline_mode=pl.Buffered(1))

    small = pl.BlockSpec((1, 1, LANE), lambda bi, gi: (gi, 0, 0))
    return pl.pallas_call(
        kern,
        grid=(b, nhg),
        in_specs=[col(tx, 0), col(tx, 1), col(tx, 2), col(tc, 0), col(tc, 1), col(tc, 2),
                  ab(tx), ab(tc),
                  pl.BlockSpec((1, tx, wd), lambda bi, gi: (bi, 0, gi), pipeline_mode=pl.Buffered(1)),
                  small, small,
                  pl.BlockSpec((1, LANE), lambda bi, gi: (0, 0))],
        out_specs=pl.BlockSpec((1, tx, wd), lambda bi, gi: (bi, 0, gi)),
        out_shape=jax.ShapeDtypeStruct((b, tx, DN_HEADS * DN_HEAD_DIM), BF16),
        scratch_shapes=[
            pltpu.VMEM((nch, 3, DN_CHUNK, LANE), F32),
            pltpu.VMEM((nch, -(-nchain // SUBLANE) * SUBLANE, DN_CHUNK), F32),
            pltpu.VMEM((nchain, DN_HEAD_DIM, DN_HEAD_DIM), F32),
            pltpu.VMEM((tc + tx, wd), F32),
            pltpu.VMEM((nslot, nchain, DN_CHUNK, DN_HEAD_DIM), BF16),
            pltpu.VMEM((nslot, nchain, 2 * DN_CHUNK, DN_HEAD_DIM), BF16),
            pltpu.VMEM((nslot, nchain, DN_HEAD_DIM, DN_CHUNK), BF16),
            pltpu.VMEM((nslot, nchain, DN_CHUNK, DN_CHUNK), BF16),
            pltpu.VMEM((nslot, nchain, SUBLANE, LANE), F32)],
        compiler_params=_cparams(("parallel", "arbitrary")),
        name="deltanet",
    )(qkv_x, qkv_x, qkv_x, qkv_c, qkv_c, qkv_c, ab_x, ab_c, gate, alog_g, dtb_g, hng)


def _outproj_kernel(x_ref, oa_ref, ob_ref, w_ref, m_ref, out_ref):
    half = oa_ref.shape[1]
    y = _dot(oa_ref[...], w_ref[:half, :]) + _dot(ob_ref[...], w_ref[half:, :])
    out_ref[...] = x_ref[...] + m_ref[0, 5:6, :] * y


def _outproj_call(x2d, oa, ob, w, m3, *, rows_per_mod, tm):
    n, d = x2d.shape
    half = oa.shape[1]
    tiles_per_mod = rows_per_mod // tm
    return pl.pallas_call(
        _outproj_kernel,
        grid=(n // tm,),
        in_specs=[pl.BlockSpec((tm, d), lambda i: (i, 0)),
                  pl.BlockSpec((tm, half), lambda i: (i, 0)),
                  pl.BlockSpec((tm, half), lambda i: (i, 0)),
                  pl.BlockSpec((2 * half, d), lambda i: (0, 0), pipeline_mode=pl.Buffered(1)),
                  pl.BlockSpec((1, N_MOD, d), lambda i: (i // tiles_per_mod, 0, 0))],
        out_specs=pl.BlockSpec((tm, d), lambda i: (i, 0)),
        out_shape=jax.ShapeDtypeStruct((n, d), F32),
        compiler_params=_cparams(("parallel",)),
        name="outproj",
    )(x2d, oa, ob, w, m3)


def _chain_rows(p, hg):
    nhg = DN_HEADS // hg
    rows = jnp.moveaxis(p.reshape(2, nhg, hg), 1, 0).reshape(nhg, 1, 2 * hg)
    return jnp.pad(rows, ((0, 0), (0, 0), (0, LANE - 2 * hg)))


FFN_TM = 512
FFN_TF = 512
PROJ_TM = 512
OUT_TM = 512
DN_HG = 4
DN_UNROLL = 4
MOD_ROWS = 16
MOD_TN = 1024


def kernel(x, c, ctx, c_ctx, w_mod, b_mod, norm_g, ffn1_w_in, ffn1_w_out, w_in, conv_w, a_log, dt_bias, head_norm_g, spatial_w, spatial_b, mlp_norm_g, w_out, ffn2_w_in, ffn2_w_out, final_g):
    bsz, t, d = x.shape
    tc = ctx.shape[1]
    assert w_mod.shape[0] == 1, "single-layer block"
    assert bsz + 1 <= MOD_ROWS

    c_all = jnp.concatenate([c, c_ctx[None, :], jnp.zeros((MOD_ROWS - bsz - 1, d), F32)], axis=0)
    m3 = _mod_call(c_all, w_mod[0], b_mod, MOD_TN).reshape(MOD_ROWS, N_MOD, d)

    g0, g1, g2 = norm_g[0, 0:1], norm_g[0, 1:2], norm_g[0, 2:3]
    w_ffn1 = _prep_ffn_weights(ffn1_w_in[0], ffn1_w_out[0], FFN_TF)
    w_ffn2 = _prep_ffn_weights(ffn2_w_in[0], ffn2_w_out[0], FFN_TF)

    x1, h_x = _ffn_call(x.reshape(bsz * t, d), m3, g0, w_ffn1, g1, mode="ffn1", mod_base=0,
                        rows_per_mod=t, mod_offset=0, tm=FFN_TM, tf=FFN_TF)
    (h_c,) = _ffn_call(ctx.reshape(bsz * tc, d), m3, g0, w_ffn1, g1, mode="ffn1ctx", mod_base=0,
                       rows_per_mod=bsz * tc, mod_offset=bsz, tm=FFN_TM, tf=FFN_TF)

    w_main, w_ab = _prep_inproj_weights(w_in[0], DN_HG)
    sb = jnp.broadcast_to(spatial_b[0][:, :, None], (MLP_GROUPS, MLP_CHUNK, LANE))
    qkv_x, ab_x, gate_x, o_b = _inproj_call(h_x, w_main, w_ab, conv_w[0], spatial_w[0].astype(BF16), sb, mlp_norm_g,
                                            tm=PROJ_TM, seq_len=t, with_mlp=True)
    qkv_c, ab_c = _inproj_call(h_c, w_main, w_ab, conv_w[0], None, None, None,
                               tm=min(PROJ_TM, tc), seq_len=tc, with_mlp=False)

    wq = DN_HEADS * DN_HEAD_DIM
    o_a = _dn_call(qkv_x.reshape(bsz, t, 3 * wq), qkv_c.reshape(bsz, tc, 3 * wq),
                   ab_x.reshape(bsz, t, -1), ab_c.reshape(bsz, tc, -1), gate_x.reshape(bsz, t, wq),
                   _chain_rows(a_log[0], DN_HG), _chain_rows(dt_bias[0], DN_HG), head_norm_g,
                   hg=DN_HG, unroll=DN_UNROLL)

    x2 = _outproj_call(x1, o_a.reshape(bsz * t, wq), o_b, w_out[0].astype(BF16), m3, rows_per_mod=t, tm=OUT_TM)
    (out,) = _ffn_call(x2, m3, g2, w_ffn2, final_g[None, :], mode="ffn2", mod_base=6,
                       rows_per_mod=t, mod_offset=0, tm=FFN_TM, tf=FFN_TF)
    return out.reshape(bsz, t, d)
```

```python
import functools

import jax
import jax.numpy as jnp
from jax import lax
from jax.experimental import pallas as pl
from jax.experimental.pallas import tpu as pltpu

F32 = jnp.float32
BF16 = jnp.bfloat16

EPS = 1e-6
LANE = 128
SUBLANE = 8
N_MOD = 9
DN_HEADS = 8
DN_HEAD_DIM = 128
DN_CHUNK = 64
CONV_K = 5
MLP_GROUPS = 8
MLP_CHUNK = 128
PROJ_CB = 1024
HALO = 16
V7X_VMEM_LIMIT = 60000 * 1024
CAST_TB = 256
ROW_CHUNK = 16
ROW_UNROLL = 8


def _cparams(sem):
    return pltpu.CompilerParams(dimension_semantics=sem, vmem_limit_bytes=V7X_VMEM_LIMIT)


def _rms(x, g):
    return x * lax.rsqrt(jnp.mean(x * x, axis=-1, keepdims=True) + EPS) * g


def _silu(x):
    return x * jax.nn.sigmoid(x)


def _dot(a, b):
    return jnp.dot(a, b, preferred_element_type=F32)


def _dot_nt(a, b):
    return lax.dot_general(a, b, (((1,), (1,)), ((), ())), preferred_element_type=F32)


def _mod_kernel(c_ref, w_ref, b_ref, o_ref):
    s = _silu(c_ref[...]).astype(BF16)
    o_ref[...] = _dot(s, w_ref[...].astype(BF16)) + b_ref[...]


def _mod_call(c_all, w_mod, b_mod, tn):
    rows, d = c_all.shape
    n = w_mod.shape[1]
    return pl.pallas_call(
        _mod_kernel,
        grid=(n // tn,),
        in_specs=[pl.BlockSpec((rows, d), lambda j: (0, 0)),
                  pl.BlockSpec((d, tn), lambda j: (0, j)),
                  pl.BlockSpec((1, tn), lambda j: (0, j))],
        out_specs=pl.BlockSpec((rows, tn), lambda j: (0, j)),
        out_shape=jax.ShapeDtypeStruct((rows, n), F32),
        compiler_params=_cparams(("arbitrary",)),
        name="mod",
    )(c_all, w_mod, b_mod)


def _ffn_kernel(*refs, mode, mod_base, nf):
    if mode == "ffn2":
        x_ref, m_ref, g_ref, wa_ref, wb_ref, wo_ref, g2_ref, out_ref, h_s, acc_s, vec_s = refs
    elif mode == "ffn1":
        x_ref, m_ref, g_ref, wa_ref, wb_ref, wo_ref, g2_ref, out_ref, h2_ref, h_s, acc_s, vec_s = refs
    else:
        x_ref, m_ref, g_ref, wa_ref, wb_ref, wo_ref, g2_ref, h2_ref, h_s, acc_s, vec_s = refs
    j = pl.program_id(1)
    tm, d = h_s.shape

    def mvec(k):
        return m_ref[0, k:k + 1, :]

    def tile_rows(v):
        return jnp.concatenate([v] * (ROW_CHUNK // SUBLANE), axis=0)

    def norm_scale(x, gain, shift=None):
        y = x * lax.rsqrt(jnp.mean(x * x, axis=-1, keepdims=True) + EPS) * tile_rows(gain)
        return y if shift is None else y + tile_rows(shift)

    def row_loop(fn):
        def body(r, carry):
            fn(pl.ds(pl.multiple_of(r * ROW_CHUNK, ROW_CHUNK), ROW_CHUNK))
            return carry
        lax.fori_loop(0, tm // ROW_CHUNK, body, 0, unroll=ROW_UNROLL)

    @pl.when(j == 0)
    def _():
        def put(k, v):
            vec_s[k] = jnp.broadcast_to(v, (SUBLANE, d))
        put(0, g_ref[...] * (1.0 + mvec(mod_base + 1)))
        put(1, mvec(mod_base))
        put(2, 0.5 * mvec(mod_base + 2))
        if mode == "ffn2":
            put(3, g2_ref[...])
        else:
            put(3, g2_ref[...] * (1.0 + mvec(mod_base + 4)))
            put(4, mvec(mod_base + 3))

        def prologue(rows):
            h_s[rows, :] = norm_scale(x_ref[rows, :], vec_s[0], vec_s[1]).astype(BF16)
            acc_s[rows, :] = jnp.zeros((ROW_CHUNK, d), F32)

        row_loop(prologue)

    h = h_s[...]
    act = (_silu(_dot(h, wa_ref[...])) * _dot(h, wb_ref[...])).astype(BF16)
    acc_s[...] += _dot(act, wo_ref[...])

    @pl.when(j == nf - 1)
    def _():
        def epilogue(rows):
            xn = x_ref[rows, :] + tile_rows(vec_s[2]) * acc_s[rows, :]
            if mode == "ffn2":
                out_ref[rows, :] = norm_scale(xn, vec_s[3])
            else:
                if mode == "ffn1":
                    out_ref[rows, :] = xn
                h2_ref[rows, :] = norm_scale(xn, vec_s[3], vec_s[4]).astype(BF16)

        row_loop(epilogue)


def _split_cast_kernel(a_ref, b_ref, wa_ref, wb_ref, tail_s, *, f, tb, nb):
    s = pl.program_id(0)
    rem = f % tb
    lane = lax.broadcasted_iota(jnp.int32, a_ref.shape, 1)
    a_blk = jnp.minimum(s, nb - 1)
    wa_ref[...] = jnp.where(a_blk * tb + lane < f, a_ref[...], 0.0).astype(BF16)

    @pl.when(s == 0)
    def _():
        wb_ref[...] = jnp.zeros_like(wb_ref)

    @pl.when(s > 0)
    def _():
        b = jnp.concatenate([tail_s[...], b_ref[:, :rem]], axis=1) if rem else tail_s[...]
        wb_ref[...] = jnp.where((s - 1) * tb + lane < f, b, 0.0).astype(BF16)

    tail_s[...] = b_ref[:, rem:]


def _split_cast_call(w_in, f_pad, tb):
    d, f2 = w_in.shape
    f = f2 // 2
    assert f2 % tb == 0 and f_pad % tb == 0 and (f % tb) % LANE == 0
    nb = f_pad // tb
    last = f2 // tb - 1
    return pl.pallas_call(
        functools.partial(_split_cast_kernel, f=f, tb=tb, nb=nb),
        grid=(nb + 1,),
        in_specs=[pl.BlockSpec((d, tb), lambda s: (0, jnp.minimum(s, nb - 1))),
                  pl.BlockSpec((d, tb), lambda s: (0, jnp.minimum(f // tb + s, last)))],
        out_specs=[pl.BlockSpec((d, tb), lambda s: (0, jnp.minimum(s, nb - 1))),
                   pl.BlockSpec((d, tb), lambda s: (0, jnp.maximum(s - 1, 0)))],
        out_shape=[jax.ShapeDtypeStruct((d, f_pad), BF16)] * 2,
        scratch_shapes=[pltpu.VMEM((d, tb - f % tb), F32)],
        compiler_params=_cparams(("arbitrary",)),
        name="ffn_weight_cast",
    )(w_in, w_in)


def _prep_ffn_weights(w_in, w_out, tf):
    f = w_in.shape[1] // 2
    pad = -f % tf
    wa, wb = _split_cast_call(w_in, f + pad, CAST_TB)
    wo = jnp.pad(w_out.astype(BF16), ((0, pad), (0, 0)))
    return wa, wb, wo


def _ffn_call(x2d, m3, g, w3, g2, *, mode, mod_base, rows_per_mod, mod_offset, tm, tf):
    r, d = x2d.shape
    wa, wb, wo = w3
    nf = wo.shape[0] // tf
    tiles_per_mod = rows_per_mod // tm
    kern = functools.partial(_ffn_kernel, mode=mode, mod_base=mod_base, nf=nf)
    x_spec = pl.BlockSpec((tm, d), lambda i, j: (i, 0))
    in_specs = [x_spec,
                pl.BlockSpec((1, N_MOD, d), lambda i, j: (mod_offset + i // tiles_per_mod, 0, 0)),
                pl.BlockSpec((1, d), lambda i, j: (0, 0)),
                pl.BlockSpec((d, tf), lambda i, j: (0, j)),
                pl.BlockSpec((d, tf), lambda i, j: (0, j)),
                pl.BlockSpec((tf, d), lambda i, j: (j, 0)),
                pl.BlockSpec((1, d), lambda i, j: (0, 0))]
    args = [x2d, m3, g, wa, wb, wo, g2]
    o_f32 = (jax.ShapeDtypeStruct((r, d), F32), x_spec)
    o_bf = (jax.ShapeDtypeStruct((r, d), BF16), x_spec)
    outs = {"ffn2": [o_f32], "ffn1": [o_f32, o_bf], "ffn1ctx": [o_bf]}[mode]
    return pl.pallas_call(
        kern,
        grid=(r // tm, nf),
        in_specs=in_specs,
        out_specs=[o[1] for o in outs],
        out_shape=[o[0] for o in outs],
        scratch_shapes=[pltpu.VMEM((tm, d), BF16), pltpu.VMEM((tm, d), F32), pltpu.VMEM((5, SUBLANE, d), F32)],
        compiler_params=_cparams(("parallel", "arbitrary")),
        name=mode,
    )(*args)


def _inproj_kernel(*refs, tm, seq_tiles, with_mlp):
    if with_mlp:
        (h_ref, hp_ref, hn_ref, w_ref, wab_ref, cw_ref, sw_ref, sb_ref, mng_ref,
         qkv_ref, ab_ref, gate_ref, ob_ref) = refs
    else:
        h_ref, hp_ref, hn_ref, w_ref, wab_ref, cw_ref, qkv_ref, ab_ref = refs
    cb = PROJ_CB
    h = h_ref[...]

    def block(jb):
        return _dot(h, w_ref[:, jb * cb:(jb + 1) * cb])

    ab_ref[...] = _dot(h, wab_ref[...])

    i = pl.program_id(0)
    pos = i % seq_tiles
    keep_prev = jnp.where(pos > 0, 1.0, 0.0).astype(BF16)
    keep_next = jnp.where(pos < seq_tiles - 1, 1.0, 0.0).astype(BF16)
    hext = jnp.concatenate([hp_ref[...] * keep_prev, h, hn_ref[...] * keep_next], axis=0)
    first = HALO - CONV_K // 2
    for jb in range(3):
        zext = _dot(hext, w_ref[:, jb * cb:(jb + 1) * cb])
        for rc in range(tm // DN_CHUNK):
            win = zext[rc * DN_CHUNK:rc * DN_CHUNK + DN_CHUNK + 2 * HALO]
            y = cw_ref[0:1, jb * cb:(jb + 1) * cb] * win[first:first + DN_CHUNK]
            for t in range(1, CONV_K):
                y = y + cw_ref[t:t + 1, jb * cb:(jb + 1) * cb] * win[first + t:first + t + DN_CHUNK]
            y = _silu(y)
            if jb < 2:
                parts = []
                for hd in range(cb // DN_HEAD_DIM):
                    yh = y[:, hd * DN_HEAD_DIM:(hd + 1) * DN_HEAD_DIM]
                    yh = yh * lax.rsqrt(jnp.sum(yh * yh, axis=-1, keepdims=True) + EPS)
                    parts.append(yh * (DN_HEAD_DIM ** -0.5) if jb == 0 else yh)
                y = jnp.concatenate(parts, axis=1)
            qkv_ref[rc * DN_CHUNK:(rc + 1) * DN_CHUNK, jb * cb:(jb + 1) * cb] = y.astype(BF16)
    if not with_mlp:
        return
    gate_ref[...] = block(3).astype(BF16)
    u = jax.nn.gelu(block(4))
    v = jax.nn.gelu(block(5))
    for g in range(MLP_GROUPS):
        cols = slice(g * LANE, (g + 1) * LANE)
        vn = _rms(v[:, cols], mng_ref[:, cols]).astype(BF16)
        for ci in range(tm // MLP_CHUNK):
            rows = slice(ci * MLP_CHUNK, (ci + 1) * MLP_CHUNK)
            s = _dot(sw_ref[g], vn[rows, :]) + sb_ref[g]
            ob_ref[rows, cols] = (u[rows, cols] * s).astype(BF16)


def _prep_inproj_weights(w_in, hg):
    wq = DN_HEADS * DN_HEAD_DIM
    n_ab = 4 * DN_HEADS
    w_main = jnp.concatenate([w_in[:, :4 * wq].astype(BF16), w_in[:, 4 * wq + n_ab:].astype(BF16)], axis=1)
    w = w_in[:, 4 * wq:4 * wq + n_ab].astype(BF16)
    wa = w[:, :2 * DN_HEADS].reshape(-1, 2, DN_HEADS // hg, hg)
    wb = w[:, 2 * DN_HEADS:].reshape(-1, 2, DN_HEADS // hg, hg)
    grp = jnp.concatenate([jnp.moveaxis(wa, 2, 1), jnp.moveaxis(wb, 2, 1)], axis=2)
    grp = grp.reshape(w.shape[0], DN_HEADS // hg, 4 * hg)
    w_ab = jnp.pad(grp, ((0, 0), (0, 0), (0, LANE - 4 * hg))).reshape(w.shape[0], -1)
    return w_main, w_ab


def _inproj_call(h2, w_main, w_ab, conv_w, sw, sb, mng, *, tm, seq_len, with_mlp):
    n, d = h2.shape
    cb = PROJ_CB
    nab = w_ab.shape[1]
    ncb = 6 if with_mlp else 3
    assert seq_len % tm == 0 and tm % HALO == 0
    kern = functools.partial(_inproj_kernel, tm=tm, seq_tiles=seq_len // tm, with_mlp=with_mlp)
    once = dict(pipeline_mode=pl.Buffered(1))
    hb = tm // HALO
    in_specs = [pl.BlockSpec((tm, d), lambda i: (i, 0)),
                pl.BlockSpec((HALO, d), lambda i: (jnp.maximum(i * hb - 1, 0), 0)),
                pl.BlockSpec((HALO, d), lambda i: (jnp.minimum((i + 1) * hb, n // HALO - 1), 0)),
                pl.BlockSpec((d, ncb * cb), lambda i: (0, 0), **once),
                pl.BlockSpec((d, nab), lambda i: (0, 0), **once),
                pl.BlockSpec((CONV_K, 3 * cb), lambda i: (0, 0), **once)]
    out_specs = [pl.BlockSpec((tm, 3 * cb), lambda i: (i, 0)),
                 pl.BlockSpec((tm, nab), lambda i: (i, 0))]
    out_shape = [jax.ShapeDtypeStruct((n, 3 * cb), BF16), jax.ShapeDtypeStruct((n, nab), F32)]
    args = [h2, h2, h2, w_main, w_ab, conv_w]
    if with_mlp:
        in_specs += [pl.BlockSpec((MLP_GROUPS, MLP_CHUNK, MLP_CHUNK), lambda i: (0, 0, 0), **once),
                     pl.BlockSpec((MLP_GROUPS, MLP_CHUNK, LANE), lambda i: (0, 0, 0), **once),
                     pl.BlockSpec((1, cb), lambda i: (0, 0))]
        out_specs += [pl.BlockSpec((tm, cb), lambda i: (i, 0))] * 2
        out_shape += [jax.ShapeDtypeStruct((n, cb), BF16)] * 2
        args += [sw, sb, mng]
    return pl.pallas_call(
        kern,
        grid=(n // tm,),
        in_specs=in_specs,
        out_specs=out_specs,
        out_shape=out_shape,
        compiler_params=_cparams(("parallel",)),
        name="inproj" if with_mlp else "inproj_ctx",
    )(*args)


def _tri_inverse(lmats, uppers, ii, jj):
    eye = (ii == jj).astype(F32)
    b16 = (ii // 16) == (jj // 16)
    b32 = (ii // 32) == (jj // 32)
    m1 = {False: b32 & ((ii // 16) > (jj // 16)), True: b32 & ((ii // 16) < (jj // 16))}
    m2 = {False: (ii // 32) > (jj // 32), True: (ii // 32) < (jj // 32)}
    a1 = [jnp.where(b16, l, 0.0) for l in lmats]
    a1b = [a.astype(BF16) for a in a1]
    a2b = [_dot(a, a).astype(BF16) for a in a1b]
    yield
    a4b = [_dot(a, a).astype(BF16) for a in a2b]
    yield
    a8b = [_dot(a, a).astype(BF16) for a in a4b]
    yield
    p = [eye - a for a in a1]
    for ab in (a2b, a4b, a8b):
        p = [pi + _dot(pi.astype(BF16), a) for pi, a in zip(p, ab)]
        yield
    for masks in (m1, m2):
        pb = [pi.astype(BF16) for pi in p]
        xm = [_dot(jnp.where(masks[u], l, 0.0).astype(BF16), b).astype(BF16)
              for l, u, b in zip(lmats, uppers, pb)]
        yield
        p = [pi - _dot(b, x) for pi, b, x in zip(p, pb, xm)]
        yield
    return p


def _run_interleaved(*gens):
    live = list(gens)
    while live:
        for g in list(live):
            try:
                next(g)
            except StopIteration:
                live.remove(g)


def _dn_kernel(qx_ref, kx_ref, vx_ref, qc_ref, kc_ref, vc_ref, abx_ref, abc_ref, gate_ref,
               alog_ref, dtb_ref, hng_ref,
               out_ref,
               col_s, row_s, st_s, o_s, pu_s, pwq_s, pkt_s, pqk_s, pgl_s,
               *, hg, tx, tc, unroll):
    C = DN_CHUNK
    D = DN_HEAD_DIM
    ncc, ncx = tc // C, tx // C
    nch = ncc + ncx
    nchain = 2 * hg

    ii = lax.broadcasted_iota(jnp.int32, (C, C), 0)
    jj = lax.broadcasted_iota(jnp.int32, (C, C), 1)
    lane_t = lax.broadcasted_iota(jnp.int32, (C, LANE), 1)

    tril = (ii >= jj).astype(F32)
    triu = (ii <= jj).astype(F32)
    ones = jnp.ones((C, C), F32)
    sum_mats = jnp.concatenate([tril, triu, ones], axis=0).astype(BF16)
    neg_a = -jnp.exp(alog_ref[0])
    dtb = dtb_ref[0]

    def gate_chunk(ab_ref, n, chunk0):
        t = ab_ref[0, n * C:(n + 1) * C, :]
        g = neg_a * jax.nn.softplus(t + dtb)
        g = jnp.where(lane_t < nchain, g, 0.0)
        hi = g.astype(BF16)
        r1 = g - hi.astype(F32)
        mid = r1.astype(BF16)
        lo = (r1 - mid.astype(F32)).astype(BF16)
        cs = _dot(sum_mats, jnp.concatenate([hi, mid, lo], axis=1))
        cs = cs[:, :LANE] + cs[:, LANE:2 * LANE] + cs[:, 2 * LANE:]
        gt = cs[2 * C:]
        gc = jnp.where(lane_t < hg, cs[:C], cs[C:2 * C])
        beta = pltpu.roll(jax.nn.sigmoid(t), LANE - nchain, 1)
        col_s[chunk0 + n, 0] = gc
        col_s[chunk0 + n, 1] = beta
        col_s[chunk0 + n, 2] = gt
        row_s[chunk0 + n] = gc.T[0:row_s.shape[1], :]

    def gate_gen(ab_ref, n_ch, chunk0, per_yield=4):
        for n in range(n_ch):
            gate_chunk(ab_ref, n, chunk0)
            if n % per_yield == per_yield - 1:
                yield

    _run_interleaved(gate_gen(abc_ref, ncc, 0))

    def chunk_of(s):
        cf = s
        cb = jnp.where(s < ncc, ncc - 1 - s, nch + ncc - 1 - s)
        return cf, cb

    def bcast(tile, r):
        return jnp.broadcast_to(tile[:, r:r + 1], (C, LANE))

    def stage1(it, par, in_context):
        q_ref, k_ref, v_ref, c0 = (qc_ref, kc_ref, vc_ref, 0) if in_context else (qx_ref, kx_ref, vx_ref, ncc)
        jobs = []
        for u in range(unroll):
            cf, cb = chunk_of(it * unroll + u)
            for d in range(2):
                c = cf if d == 0 else cb
                gc_t = col_s[c, 0]
                be_t = col_s[c, 1]
                gt_t = col_s[c, 2]
                e1_t = jnp.exp(gc_t)
                tiles = dict(gc=gc_t, be=be_t, e1=e1_t, ca=be_t * e1_t, e2=jnp.exp(gt_t - gc_t),
                             gl=jnp.exp(gt_t), rows=row_s[c])
                for hl in range(hg):
                    jobs.append((par * unroll + u, d * hg + hl, pl.multiple_of((c - c0) * C, C), tiles, d, hl))
        tri = {0: ii >= jj, 1: ii <= jj}
        strict = {0: ii > jj, 1: ii < jj}

        def rd(ref, job):
            return ref[0, pl.ds(job[2], C), job[5] * D:(job[5] + 1) * D]

        qn = [rd(q_ref, j) for j in jobs]
        kn = [rd(k_ref, j) for j in jobs]
        gmat = [_dot_nt(jnp.concatenate([q, k], axis=0), k) for q, k in zip(qn, kn)]
        yield
        dec = [jnp.exp(jnp.where(tri[j[4]], bcast(j[3]["gc"], j[1])[:, :C] - j[3]["rows"][j[1]:j[1] + 1, :], -1e30))
               for j in jobs]
        lmat = [jnp.where(strict[j[4]], g[C:] * dc, 0.0) * bcast(j[3]["be"], j[1])[:, :C]
                for j, g, dc in zip(jobs, gmat, dec)]
        for j, g, dc in zip(jobs, gmat, dec):
            pqk_s[j[0], j[1]] = (g[:C] * dc).astype(BF16)
        tinv = yield from _tri_inverse(lmat, [bool(j[4]) for j in jobs], ii, jj)
        knf = [k.astype(F32) for k in kn]
        rhs = [jnp.concatenate([(rd(v_ref, j).astype(F32) * bcast(j[3]["be"], j[1])).astype(BF16),
                                (kf * bcast(j[3]["ca"], j[1])).astype(BF16)], axis=1)
               for j, kf in zip(jobs, knf)]
        uw = [_dot(t.astype(BF16), r) for t, r in zip(tinv, rhs)]
        yield
        for j, x, q, kf in zip(jobs, uw, qn, knf):
            slot, r = j[0], j[1]
            qd = (q.astype(F32) * bcast(j[3]["e1"], r)).astype(BF16)
            pu_s[slot, r] = x[:, :D].astype(BF16)
            pwq_s[slot, r] = jnp.concatenate([x[:, D:].astype(BF16), qd], axis=0)
            pkt_s[slot, r] = (kf * bcast(j[3]["e2"], r)).T.astype(BF16)
            pgl_s[slot, r] = jnp.broadcast_to(j[3]["gl"][0:SUBLANE, r:r + 1], (SUBLANE, LANE))

    def stage2(it, par):
        for u in range(unroll):
            slot = par * unroll + u
            cf, cb = chunk_of(it * unroll + u)
            chains = [(d * hg + hl, pl.multiple_of((cb if d else cf) * C, C), hl)
                      for d in range(2) for hl in range(hg)]
            st = [st_s[r] for r, _, _ in chains]
            ws = [_dot(pwq_s[slot, r], s.astype(BF16)) for (r, _, _), s in zip(chains, st)]
            yield
            vn = [(pu_s[slot, r].astype(F32) - w[:C]).astype(BF16) for (r, _, _), w in zip(chains, ws)]
            o = [w[C:] + _dot(pqk_s[slot, r], v) for (r, _, _), w, v in zip(chains, ws, vn)]
            kv = [_dot(pkt_s[slot, r], v) for (r, _, _), v in zip(chains, vn)]
            yield
            for (r, row0, hl), s, x, y in zip(chains, st, kv, o):
                st_s[r] = s * jnp.concatenate([pgl_s[slot, r]] * (D // SUBLANE), axis=0) + x
                o_s[pl.ds(row0, C), hl * D:(hl + 1) * D] += y

    st_s[...] = jnp.zeros_like(st_s)
    o_s[...] = jnp.zeros_like(o_s)
    _run_interleaved(stage1(0, 0, True), gate_gen(abx_ref, ncx, ncc))

    def loop_body(it, carry):
        _run_interleaved(stage1(it, it % 2, False), stage2(it - 1, (it - 1) % 2))
        return carry

    n_it = nch // unroll
    lax.fori_loop(1, n_it, loop_body, 0)
    _run_interleaved(stage2(n_it - 1, (n_it - 1) % 2))

    hn = hng_ref[...]
    for hl in range(hg):
        cols = slice(hl * D, (hl + 1) * D)
        o = o_s[tc:tc + tx, cols]
        zg = gate_ref[0, :, cols].astype(F32)
        out_ref[0, :, cols] = (_rms(o, hn) * _silu(zg)).astype(BF16)


def _dn_call(qkv_x, qkv_c, ab_x, ab_c, gate, alog_g, dtb_g, hng, *, hg, unroll):
    b, tx, w3 = qkv_x.shape
    tc = qkv_c.shape[1]
    assert tc // DN_CHUNK == unroll and (tx // DN_CHUNK) % unroll == 0
    nslot = 2 * unroll
    nhg = DN_HEADS // hg
    wd = hg * DN_HEAD_DIM
    nch = (tx + tc) // DN_CHUNK
    nchain = 2 * hg
    kern = functools.partial(_dn_kernel, hg=hg, tx=tx, tc=tc, unroll=unroll)

    def col(t, off):
        return pl.BlockSpec((1, t, wd), lambda bi, gi: (bi, 0, off * nhg + gi))

    def ab(t):
        return pl.BlockSpec((1, t, LANE), lambda bi, gi: (bi, 0, gi))

    small = pl.BlockSpec((1, 1, LANE), lambda bi, gi: (gi, 0, 0))
    return pl.pallas_call(
        kern,
        grid=(b, nhg),
        in_specs=[col(tx, 0), col(tx, 1), col(tx, 2), col(tc, 0), col(tc, 1), col(tc, 2),
                  ab(tx), ab(tc),
                  pl.BlockSpec((1, tx, wd), lambda bi, gi: (bi, 0, gi)),
                  small, small,
                  pl.BlockSpec((1, LANE), lambda bi, gi: (0, 0))],
        out_specs=pl.BlockSpec((1, tx, wd), lambda bi, gi: (bi, 0, gi)),
        out_shape=jax.ShapeDtypeStruct((b, tx, DN_HEADS * DN_HEAD_DIM), BF16),
        scratch_shapes=[
            pltpu.VMEM((nch, 3, DN_CHUNK, LANE), F32),
            pltpu.VMEM((nch, -(-nchain // SUBLANE) * SUBLANE, DN_CHUNK), F32),
            pltpu.VMEM((nchain, DN_HEAD_DIM, DN_HEAD_DIM), F32),
            pltpu.VMEM((tc + tx, wd), F32),
            pltpu.VMEM((nslot, nchain, DN_CHUNK, DN_HEAD_DIM), BF16),
            pltpu.VMEM((nslot, nchain, 2 * DN_CHUNK, DN_HEAD_DIM), BF16),
            pltpu.VMEM((nslot, nchain, DN_HEAD_DIM, DN_CHUNK), BF16),
            pltpu.VMEM((nslot, nchain, DN_CHUNK, DN_CHUNK), BF16),
            pltpu.VMEM((nslot, nchain, SUBLANE, LANE), F32)],
        compiler_params=_cparams(("parallel", "arbitrary")),
        name="deltanet",
    )(qkv_x, qkv_x, qkv_x, qkv_c, qkv_c, qkv_c, ab_x, ab_c, gate, alog_g, dtb_g, hng)


def _outproj_kernel(x_ref, oa_ref, ob_ref, w_ref, m_ref, out_ref):
    half = oa_ref.shape[1]
    y = _dot(oa_ref[...], w_ref[:half, :]) + _dot(ob_ref[...], w_ref[half:, :])
    out_ref[...] = x_ref[...] + m_ref[0, 5:6, :] * y


def _outproj_call(x2d, oa, ob, w, m3, *, rows_per_mod, tm):
    n, d = x2d.shape
    half = oa.shape[1]
    tiles_per_mod = rows_per_mod // tm
    return pl.pallas_call(
        _outproj_kernel,
        grid=(n // tm,),
        in_specs=[pl.BlockSpec((tm, d), lambda i: (i, 0)),
                  pl.BlockSpec((tm, half), lambda i: (i, 0)),
                  pl.BlockSpec((tm, half), lambda i: (i, 0)),
                  pl.BlockSpec((2 * half, d), lambda i: (0, 0), pipeline_mode=pl.Buffered(1)),
                  pl.BlockSpec((1, N_MOD, d), lambda i: (i // tiles_per_mod, 0, 0))],
        out_specs=pl.BlockSpec((tm, d), lambda i: (i, 0)),
        out_shape=jax.ShapeDtypeStruct((n, d), F32),
        compiler_params=_cparams(("parallel",)),
        name="outproj",
    )(x2d, oa, ob, w, m3)


def _chain_rows(p, hg):
    nhg = DN_HEADS // hg
    rows = jnp.moveaxis(p.reshape(2, nhg, hg), 1, 0).reshape(nhg, 1, 2 * hg)
    return jnp.pad(rows, ((0, 0), (0, 0), (0, LANE - 2 * hg)))


FFN_TM = 512
FFN_TF = 512
PROJ_TM = 512
OUT_TM = 512
DN_HG = 4
DN_UNROLL = 4
MOD_ROWS = 16
MOD_TN = 1024


def kernel(x, c, ctx, c_ctx, w_mod, b_mod, norm_g, ffn1_w_in, ffn1_w_out, w_in, conv_w, a_log, dt_bias, head_norm_g, spatial_w, spatial_b, mlp_norm_g, w_out, ffn2_w_in, ffn2_w_out, final_g):
    bsz, t, d = x.shape
    tc = ctx.shape[1]
    assert w_mod.shape[0] == 1, "single-layer block"
    assert bsz + 1 <= MOD_ROWS

    c_all = jnp.concatenate([c, c_ctx[None, :], jnp.zeros((MOD_ROWS - bsz - 1, d), F32)], axis=0)
    m3 = _mod_call(c_all, w_mod[0], b_mod, MOD_TN).reshape(MOD_ROWS, N_MOD, d)

    g0, g1, g2 = norm_g[0, 0:1], norm_g[0, 1:2], norm_g[0, 2:3]
    w_ffn1 = _prep_ffn_weights(ffn1_w_in[0], ffn1_w_out[0], FFN_TF)
    w_ffn2 = _prep_ffn_weights(ffn2_w_in[0], ffn2_w_out[0], FFN_TF)

    x1, h_x = _ffn_call(x.reshape(bsz * t, d), m3, g0, w_ffn1, g1, mode="ffn1", mod_base=0,
                        rows_per_mod=t, mod_offset=0, tm=FFN_TM, tf=FFN_TF)
    (h_c,) = _ffn_call(ctx.reshape(bsz * tc, d), m3, g0, w_ffn1, g1, mode="ffn1ctx", mod_base=0,
                       rows_per_mod=bsz * tc, mod_offset=bsz, tm=FFN_TM, tf=FFN_TF)

    w_main, w_ab = _prep_inproj_weights(w_in[0], DN_HG)
    sb = jnp.broadcast_to(spatial_b[0][:, :, None], (MLP_GROUPS, MLP_CHUNK, LANE))
    qkv_x, ab_x, gate_x, o_b = _inproj_call(h_x, w_main, w_ab, conv_w[0], spatial_w[0].astype(BF16), sb, mlp_norm_g,
                                            tm=PROJ_TM, seq_len=t, with_mlp=True)
    qkv_c, ab_c = _inproj_call(h_c, w_main, w_ab, conv_w[0], None, None, None,
                               tm=min(PROJ_TM, tc), seq_len=tc, with_mlp=False)

    wq = DN_HEADS * DN_HEAD_DIM
    o_a = _dn_call(qkv_x.reshape(bsz, t, 3 * wq), qkv_c.reshape(bsz, tc, 3 * wq),
                   ab_x.reshape(bsz, t, -1), ab_c.reshape(bsz, tc, -1), gate_x.reshape(bsz, t, wq),
                   _chain_rows(a_log[0], DN_HG), _chain_rows(dt_bias[0], DN_HG), head_norm_g,
                   hg=DN_HG, unroll=DN_UNROLL)

    x2 = _outproj_call(x1, o_a.reshape(bsz * t, wq), o_b, w_out[0].astype(BF16), m3, rows_per_mod=t, tm=OUT_TM)
    (out,) = _ffn_call(x2, m3, g2, w_ffn2, final_g[None, :], mode="ffn2", mod_base=6,
                       rows_per_mod=t, mod_offset=0, tm=FFN_TM, tf=FFN_TF)
    return out.reshape(bsz, t, d)
```
